```python
import math
import jax, jax.numpy as jnp
from jax import lax
import numpy as np

D_MODEL = 1024
BATCH = 8
SEQ = 2048
DEPTH = 4

MEM_LEN = 256
HEAD_DIM = 64
N_HEADS = D_MODEL // HEAD_DIM
N_KV_GROUPS = 4
HEADS_PER_GROUP = N_HEADS // N_KV_GROUPS
CMP_BLOCK = 32
CMP_STRIDE = 16
SEL_BLOCK = 64
SEL_TOP_N = 16
WINDOW = 512
Q_BLOCK = 128
SEL_Q_CHUNK = 16
CONV_CH = D_MODEL
CONV_WIDTH = 31
X_HEADS = 4
X_HEAD_DIM = D_MODEL // X_HEADS
D_FF = int(math.ceil(8 * D_MODEL / 3 / 256)) * 256
REL_BUCKETS = 32
REL_MAX_DIST = 128
EPS = 1e-6

SPLIT_SIZES = (
    2 * CONV_CH,
    N_HEADS * HEAD_DIM,
    N_KV_GROUPS * HEAD_DIM,
    N_KV_GROUPS * HEAD_DIM,
    N_KV_GROUPS * HEAD_DIM,
    N_KV_GROUPS * HEAD_DIM,
    N_KV_GROUPS * HEAD_DIM,
    N_KV_GROUPS * HEAD_DIM,
    3 * N_HEADS,
    D_MODEL,
    D_MODEL,
)
N_IN = sum(SPLIT_SIZES)
SPLIT_POINTS = tuple(int(v) for v in np.cumsum(SPLIT_SIZES)[:-1])

kernel_name = "hybrid_conformer_nsa_block"


def rms_norm(x, g):
    xf = x.astype(jnp.float32)
    y = xf * lax.rsqrt(jnp.mean(xf * xf, axis=-1, keepdims=True) + EPS)
    return (y * g.astype(jnp.float32)).astype(x.dtype)


def layer_norm(x, g, b):
    xf = x.astype(jnp.float32)
    mu = jnp.mean(xf, axis=-1, keepdims=True)
    xc = xf - mu
    y = xc * lax.rsqrt(jnp.mean(xc * xc, axis=-1, keepdims=True) + EPS)
    return (y * g.astype(jnp.float32) + b.astype(jnp.float32)).astype(x.dtype)


def t5_bucket(dist):
    n = jnp.maximum(dist, 0)
    max_exact = REL_BUCKETS // 2
    nf = jnp.maximum(n, 1).astype(jnp.float32)
    large = max_exact + (jnp.log(nf / max_exact) / math.log(REL_MAX_DIST / max_exact)
                         * (REL_BUCKETS - max_exact)).astype(jnp.int32)
    large = jnp.minimum(large, REL_BUCKETS - 1)
    return jnp.where(n < max_exact, n, large)


def masked_softmax(s, mask):
    s = jnp.where(mask, s, -1e30)
    m = jnp.max(s, axis=-1, keepdims=True)
    p = jnp.where(mask, jnp.exp(s - m), 0.0)
    return p / jnp.maximum(jnp.sum(p, axis=-1, keepdims=True), 1e-30)


def conv_module(u, dw_w, dw_b, ln_g, ln_b, pw_w):
    a, gt = jnp.split(u, 2, axis=-1)
    v = a * jax.nn.sigmoid(gt)
    v = lax.conv_general_dilated(
        v, dw_w[:, None, :], window_strides=(1,),
        padding=((CONV_WIDTH - 1, 0),),
        dimension_numbers=("NWC", "WIO", "NWC"),
        feature_group_count=CONV_CH) + dw_b
    v = jax.nn.silu(layer_norm(v, ln_g, ln_b))
    return v @ pw_w


def compress_blocks(blocks, pos, w1, w2):
    z = blocks + pos[None, None, :, None, :]
    z = jnp.einsum("bnlgd,lde->bnge", z, w1)
    return jax.nn.silu(z) @ w2


def nsa_attention(q, kc, vc, ks, vs, kw, vw, gate_logits, cmp_pos, cmp_w1, cmp_w2, rel_bias):
    B, S = q.shape[0], q.shape[1]
    G, R, Dh = N_KV_GROUPS, HEADS_PER_GROUP, HEAD_DIM
    scale = 1.0 / math.sqrt(Dh)
    qg = q.reshape(B, S, G, R, Dh)
    t = jnp.arange(S)

    n_cmp = (S - CMP_BLOCK) // CMP_STRIDE + 1
    cidx = np.arange(n_cmp)[:, None] * CMP_STRIDE + np.arange(CMP_BLOCK)[None, :]
    k_cmp = compress_blocks(kc[:, cidx], cmp_pos[0], cmp_w1[0], cmp_w2[0])
    v_cmp = compress_blocks(vc[:, cidx], cmp_pos[1], cmp_w1[1], cmp_w2[1])
    c_end = jnp.arange(n_cmp) * CMP_STRIDE + CMP_BLOCK - 1
    dist_c = t[:, None] - c_end[None, :]
    bias_c = rel_bias[t5_bucket(dist_c)].reshape(S, n_cmp, G, R).transpose(2, 3, 0, 1)
    s_c = jnp.einsum("bsgrd,bngd->bgrsn", qg, k_cmp).astype(jnp.float32) * scale + bias_c
    p_cmp = masked_softmax(s_c, dist_c >= 0)
    o_cmp = jnp.einsum("bgrsn,bngd->bsgrd", p_cmp.astype(v_cmp.dtype), v_cmp)

    n_sel = S // SEL_BLOCK
    top_n = min(SEL_TOP_N, n_sel)
    jj = np.arange(n_cmp)[:, None] * CMP_STRIDE
    mm0 = np.arange(n_sel)[None, :] * SEL_BLOCK
    overlap = ((jj < mm0 + SEL_BLOCK) & (jj + CMP_BLOCK > mm0)).astype(np.float32)
    imp = jnp.einsum("bgrsn,nm->bsgm", p_cmp, jnp.asarray(overlap))
    blk_t = (t // SEL_BLOCK)[:, None]
    mm = jnp.arange(n_sel)[None, :]
    forced = (mm == 0) | (mm == blk_t) | (mm == blk_t - 1)
    valid = mm <= blk_t
    score = jnp.where(forced[None, :, None, :], jnp.inf,
                      jnp.where(valid[None, :, None, :], imp, -jnp.inf))
    _, sel_idx = lax.top_k(score, top_n)

    ksg = ks.reshape(B, n_sel, SEL_BLOCK, G, Dh).transpose(0, 3, 1, 2, 4)
    vsg = vs.reshape(B, n_sel, SEL_BLOCK, G, Dh).transpose(0, 3, 1, 2, 4)
    bias_g = rel_bias.reshape(REL_BUCKETS, G, R)
    nc = S // SEL_Q_CHUNK
    q_ch = qg.reshape(B, nc, SEL_Q_CHUNK, G, R, Dh).swapaxes(0, 1)
    i_ch = sel_idx.reshape(B, nc, SEL_Q_CHUNK, G, top_n).swapaxes(0, 1)
    starts = jnp.arange(nc) * SEL_Q_CHUNK
    b_ix = jnp.arange(B)[:, None, None, None]
    g_ix = jnp.arange(G)[None, None, :, None]
    n_keys = top_n * SEL_BLOCK

    def sel_chunk(args):
        qc, ic, st = args
        kg = ksg[b_ix, g_ix, ic].reshape(B, SEL_Q_CHUNK, G, n_keys, Dh)
        vg = vsg[b_ix, g_ix, ic].reshape(B, SEL_Q_CHUNK, G, n_keys, Dh)
        kpos = (ic[..., None] * SEL_BLOCK + jnp.arange(SEL_BLOCK)).reshape(B, SEL_Q_CHUNK, G, n_keys)
        tq = st + jnp.arange(SEL_Q_CHUNK)
        dist = tq[None, :, None, None] - kpos
        bias = bias_g[t5_bucket(dist), g_ix]
        s = jnp.einsum("bqgrd,bqgkd->bqgrk", qc, kg).astype(jnp.float32) * scale \
            + bias.swapaxes(-1, -2)
        p = masked_softmax(s, (dist >= 0)[:, :, :, None, :])
        return jnp.einsum("bqgrk,bqgkd->bqgrd", p.astype(vg.dtype), vg)

    o_slc = lax.map(sel_chunk, (q_ch, i_ch, starts)).swapaxes(0, 1).reshape(B, S, G, R, Dh)

    nq = S // Q_BLOCK
    kwp = jnp.pad(kw, ((0, 0), (WINDOW, 0), (0, 0), (0, 0)))
    vwp = jnp.pad(vw, ((0, 0), (WINDOW, 0), (0, 0), (0, 0)))
    q_bl = qg.reshape(B, nq, Q_BLOCK, G, R, Dh).swapaxes(0, 1)

    def win_block(args):
        qb, i = args
        start = i * Q_BLOCK
        kb = lax.dynamic_slice_in_dim(kwp, start, Q_BLOCK + WINDOW, axis=1)
        vb = lax.dynamic_slice_in_dim(vwp, start, Q_BLOCK + WINDOW, axis=1)
        tq = start + jnp.arange(Q_BLOCK)
        kpos = start - WINDOW + jnp.arange(Q_BLOCK + WINDOW)
        dist = tq[:, None] - kpos[None, :]
        mask = (dist >= 0) & (dist < WINDOW) & (kpos[None, :] >= 0)
        bias = rel_bias[t5_bucket(dist)].reshape(Q_BLOCK, Q_BLOCK + WINDOW, G, R).transpose(2, 3, 0, 1)
        s = jnp.einsum("bqgrd,bkgd->bgrqk", qb, kb).astype(jnp.float32) * scale + bias
        p = masked_softmax(s, mask)
        return jnp.einsum("bgrqk,bkgd->bqgrd", p.astype(vb.dtype), vb)

    o_win = lax.map(win_block, (q_bl, jnp.arange(nq))).swapaxes(0, 1).reshape(B, S, G, R, Dh)

    g = jax.nn.sigmoid(gate_logits).reshape(B, S, 3, G, R, 1)
    o = g[:, :, 0] * o_cmp + g[:, :, 1] * o_slc + g[:, :, 2] * o_win
    return o.reshape(B, S, N_HEADS * HEAD_DIM)


def hybrid_mixer(h, w_in, conv_dw_w, conv_dw_b, conv_ln_g, conv_ln_b, conv_pw_w,
                 cmp_pos, cmp_w1, cmp_w2, w_out, rel_bias):
    B, S = h.shape[0], h.shape[1]
    (u_conv, q, kc, vc, ks, vs, kw, vw, g_nsa_heads, g_conv, g_attn) = jnp.split(
        h @ w_in, SPLIT_POINTS, axis=-1)
    kv_shape = (B, S, N_KV_GROUPS, HEAD_DIM)
    conv_out = conv_module(u_conv, conv_dw_w, conv_dw_b, conv_ln_g, conv_ln_b, conv_pw_w)
    nsa_out = nsa_attention(
        q.reshape(B, S, N_HEADS, HEAD_DIM),
        kc.reshape(kv_shape), vc.reshape(kv_shape), ks.reshape(kv_shape),
        vs.reshape(kv_shape), kw.reshape(kv_shape), vw.reshape(kv_shape),
        g_nsa_heads.reshape(B, S, 3, N_HEADS), cmp_pos, cmp_w1, cmp_w2, rel_bias)
    y = jax.nn.sigmoid(g_conv) * conv_out + jax.nn.sigmoid(g_attn) * nsa_out
    return y @ w_out


def cross_attention(h, mem, wq, wkv, wo):
    B, S = h.shape[0], h.shape[1]
    M = mem.shape[1]
    q = (h @ wq).reshape(B, S, X_HEADS, X_HEAD_DIM)
    k, v = jnp.split(mem @ wkv, 2, axis=-1)
    k = k.reshape(B, M, X_HEADS, X_HEAD_DIM)
    v = v.reshape(B, M, X_HEADS, X_HEAD_DIM)
    s = jnp.einsum("bshd,bmhd->bhsm", q, k).astype(jnp.float32) / math.sqrt(X_HEAD_DIM)
    p = jax.nn.softmax(s, axis=-1).astype(v.dtype)
    o = jnp.einsum("bhsm,bmhd->bshd", p, v).reshape(B, S, X_HEADS * X_HEAD_DIM)
    return o @ wo


def swiglu(h, w_in, w_out):
    a, b = jnp.split(h @ w_in, 2, axis=-1)
    return (jax.nn.silu(a) * b) @ w_out


def setup_inputs(seed: int = 0) -> dict:
    key = jax.random.key(seed)
    ks = jax.random.split(key, 24)
    f32 = jnp.float32

    def nrm(k, shape, scale):
        return jax.random.normal(k, shape, f32) * scale

    def gain(k, shape):
        return 1.0 + 0.02 * jax.random.normal(k, shape, f32)

    return {
        "x": jax.random.normal(ks[0], (BATCH, SEQ, D_MODEL), f32),
        "mem": jax.random.normal(ks[1], (BATCH, MEM_LEN, D_MODEL), f32),
        "norm_mix_g": gain(ks[2], (DEPTH, D_MODEL)),
        "w_in": nrm(ks[3], (DEPTH, D_MODEL, N_IN), D_MODEL ** -0.5),
        "conv_dw_w": nrm(ks[4], (DEPTH, CONV_WIDTH, CONV_CH), CONV_WIDTH ** -0.5),
        "conv_dw_b": nrm(ks[5], (DEPTH, CONV_CH), 0.02),
        "conv_ln_g": gain(ks[6], (DEPTH, CONV_CH)),
        "conv_ln_b": nrm(ks[7], (DEPTH, CONV_CH), 0.02),
        "conv_pw_w": nrm(ks[8], (DEPTH, CONV_CH, D_MODEL), CONV_CH ** -0.5),
        "cmp_pos": nrm(ks[9], (DEPTH, 2, CMP_BLOCK, HEAD_DIM), 0.1),
        "cmp_w1": nrm(ks[10], (DEPTH, 2, CMP_BLOCK, HEAD_DIM, HEAD_DIM), (CMP_BLOCK * HEAD_DIM) ** -0.5),
        "cmp_w2": nrm(ks[11], (DEPTH, 2, HEAD_DIM, HEAD_DIM), HEAD_DIM ** -0.5),
        "w_out": nrm(ks[12], (DEPTH, D_MODEL, D_MODEL), D_MODEL ** -0.5),
        "norm_x_g": gain(ks[13], (DEPTH, D_MODEL)),
        "xq_w": nrm(ks[14], (DEPTH, D_MODEL, X_HEADS * X_HEAD_DIM), D_MODEL ** -0.5),
        "xkv_w": nrm(ks[15], (DEPTH, D_MODEL, 2 * X_HEADS * X_HEAD_DIM), D_MODEL ** -0.5),
        "xo_w": nrm(ks[16], (DEPTH, X_HEADS * X_HEAD_DIM, D_MODEL), (X_HEADS * X_HEAD_DIM) ** -0.5),
        "norm_ffn_g": gain(ks[17], (DEPTH, D_MODEL)),
        "ffn_in_w": nrm(ks[18], (DEPTH, D_MODEL, 2 * D_FF), D_MODEL ** -0.5),
        "ffn_out_w": nrm(ks[19], (DEPTH, D_FF, D_MODEL), D_FF ** -0.5),
        "rel_bias": nrm(ks[20], (REL_BUCKETS, N_HEADS), 0.5),
        "final_norm_g": gain(ks[21], (D_MODEL,)),
    }


def reference(x, mem, norm_mix_g, w_in, conv_dw_w, conv_dw_b, conv_ln_g, conv_ln_b,
              conv_pw_w, cmp_pos, cmp_w1, cmp_w2, w_out, norm_x_g, xq_w, xkv_w, xo_w,
              norm_ffn_g, ffn_in_w, ffn_out_w, rel_bias, final_norm_g):
    for l in range(DEPTH):
        h = rms_norm(x, norm_mix_g[l])
        x = x + hybrid_mixer(h, w_in[l], conv_dw_w[l], conv_dw_b[l], conv_ln_g[l], conv_ln_b[l],
                             conv_pw_w[l], cmp_pos[l], cmp_w1[l], cmp_w2[l], w_out[l], rel_bias)
        h = rms_norm(x, norm_x_g[l])
        x = x + cross_attention(h, mem, xq_w[l], xkv_w[l], xo_w[l])
        h = rms_norm(x, norm_ffn_g[l])
        x = x + swiglu(h, ffn_in_w[l], ffn_out_w[l])
    return rms_norm(x, final_norm_g)
```

```python
import functools
import math

import numpy as np
import jax
import jax.numpy as jnp
from jax import lax
from jax.experimental import pallas as pl
from jax.experimental.pallas import tpu as pltpu

F32 = jnp.float32
BF16 = jnp.bfloat16

HEAD_DIM = 64
N_KV_GROUPS = 4
HEADS_PER_GROUP = 4
N_HEADS = N_KV_GROUPS * HEADS_PER_GROUP
GROUP_WIDTH = HEADS_PER_GROUP * HEAD_DIM
CMP_BLOCK = 32
CMP_STRIDE = 16
SEL_BLOCK = 64
SEL_TOP_N = 16
WINDOW = 512
CONV_WIDTH = 31
X_HEADS = 4
REL_BUCKETS = 32
REL_MAX_DIST = 128
EPS = 1e-6
NEG = -1e30

LANES = 128
CONV_HALO = 32
N_CMP_PAD = 128
IN_PROJ_N_TILES = 6
VMEM_LIMIT = 56 * 1024 * 1024


def _sigmoid(x):
    return 1.0 / (1.0 + jnp.exp(-x))


def _dot(a, b):
    return jnp.dot(a, b, preferred_element_type=F32)


def _dot_nt(a, b):
    return lax.dot_general(a, b, (((1,), (1,)), ((), ())), preferred_element_type=F32)


def _rms_norm(x, g):
    return x * lax.rsqrt(jnp.mean(x * x, axis=-1, keepdims=True) + EPS) * g


def _params(n_axes):
    return pltpu.CompilerParams(dimension_semantics=("arbitrary",) * n_axes,
                                vmem_limit_bytes=VMEM_LIMIT)


def _norm_matmul_kernel(x_ref, g_ref, w_ref, o_ref, xn_ref):
    @pl.when(pl.program_id(1) == 0)
    def _():
        xn_ref[...] = _rms_norm(x_ref[...], g_ref[...]).astype(BF16)

    o_ref[...] = _dot(xn_ref[...], w_ref[...]).astype(o_ref.dtype)


def norm_matmul(x, g, w, *, tm, tn):
    m, k = x.shape
    n = w.shape[1]
    return pl.pallas_call(
        _norm_matmul_kernel,
        grid=(m // tm, n // tn),
        in_specs=[pl.BlockSpec((tm, k), lambda i, j: (i, 0)),
                  pl.BlockSpec((1, k), lambda i, j: (0, 0)),
                  pl.BlockSpec((k, tn), lambda i, j: (0, j))],
        out_specs=pl.BlockSpec((tm, tn), lambda i, j: (i, j)),
        out_shape=jax.ShapeDtypeStruct((m, n), BF16),
        scratch_shapes=[pltpu.VMEM((tm, k), BF16)],
        compiler_params=_params(2),
        name="norm_matmul",
    )(x, g, w)


def _matmul_kernel(x_ref, w_ref, o_ref):
    o_ref[...] = _dot(x_ref[...].astype(BF16), w_ref[...]).astype(o_ref.dtype)


def matmul(x, w, *, tm):
    m, k = x.shape
    n = w.shape[1]
    return pl.pallas_call(
        _matmul_kernel,
        grid=(m // tm,),
        in_specs=[pl.BlockSpec((tm, k), lambda i: (i, 0)),
                  pl.BlockSpec((k, n), lambda i: (0, 0))],
        out_specs=pl.BlockSpec((tm, n), lambda i: (i, 0)),
        out_shape=jax.ShapeDtypeStruct((m, n), BF16),
        compiler_params=_params(1),
        name="matmul",
    )(x, w)


def _conv_kernel(a_ref, gt_ref, dww_ref, dwb_ref, lng_ref, lnb_ref, pw_ref, o_ref, buf_ref, *, t):
    s = pl.program_id(1)

    @pl.when(s == 0)
    def _():
        buf_ref[0:CONV_HALO, :] = jnp.zeros((CONV_HALO, buf_ref.shape[1]), F32)

    @pl.when(s > 0)
    def _():
        buf_ref[0:CONV_HALO, :] = buf_ref[t:t + CONV_HALO, :]

    a = a_ref[0].astype(F32)
    gt = gt_ref[0].astype(F32)
    buf_ref[CONV_HALO:CONV_HALO + t, :] = a * _sigmoid(gt)

    base = CONV_HALO - (CONV_WIDTH - 1)
    acc = dww_ref[0:1, :] * buf_ref[base:base + t, :]
    for k in range(1, CONV_WIDTH):
        acc = acc + dww_ref[k:k + 1, :] * buf_ref[base + k:base + k + t, :]
    v = acc + dwb_ref[...]

    mu = jnp.mean(v, axis=-1, keepdims=True)
    vc = v - mu
    y = vc * lax.rsqrt(jnp.mean(vc * vc, axis=-1, keepdims=True) + EPS)
    y = y * lng_ref[...] + lnb_ref[...]
    y = y * _sigmoid(y)
    o_ref[0] = _dot(y.astype(BF16), pw_ref[...]).astype(o_ref.dtype)


def conv_module(p, dww, dwb, lng, lnb, pw, *, t):
    b, s, _ = p.shape
    c = pw.shape[0]
    vec = lambda: pl.BlockSpec((1, c), lambda bi, si: (0, 0))
    return pl.pallas_call(
        functools.partial(_conv_kernel, t=t),
        grid=(b, s // t),
        in_specs=[pl.BlockSpec((1, t, c), lambda bi, si: (bi, si, 0)),
                  pl.BlockSpec((1, t, c), lambda bi, si: (bi, si, 1)),
                  pl.BlockSpec((CONV_WIDTH, c), lambda bi, si: (0, 0)),
                  vec(), vec(), vec(),
                  pl.BlockSpec((c, c), lambda bi, si: (0, 0))],
        out_specs=pl.BlockSpec((1, t, c), lambda bi, si: (bi, si, 0)),
        out_shape=jax.ShapeDtypeStruct((b, s, c), BF16),
        scratch_shapes=[pltpu.VMEM((CONV_HALO + t, c), F32)],
        compiler_params=_params(2),
        name="conv_module",
    )(p, p, dww, dwb, lng, lnb, pw)


def _compress_kernel(x_ref, pos_ref, w1_ref, w2_ref, o_ref, xf_ref):
    half = CMP_BLOCK // 2
    gw = N_KV_GROUPS * HEAD_DIM
    n_chunk = gw // LANES
    for c in range(2 * n_chunk):
        xf_ref[c] = x_ref[0, :, c * LANES:(c + 1) * LANES].astype(F32)
    for kv in range(2):
        acc_a = jnp.zeros((N_CMP_PAD, gw), F32)
        acc_b = jnp.zeros((N_CMP_PAD, gw), F32)
        for l in range(half):
            rows = jnp.concatenate(
                [xf_ref[kv * n_chunk + c, pl.ds(l, N_CMP_PAD, stride=CMP_STRIDE), :] for c in range(n_chunk)],
                axis=1)
            ra = (rows + pos_ref[kv, l:l + 1, :]).astype(BF16)
            rb = (rows + pos_ref[kv, half + l:half + l + 1, :]).astype(BF16)
            acc_a = acc_a + _dot(ra, w1_ref[kv, l])
            acc_b = acc_b + _dot(rb, w1_ref[kv, half + l])
        z = acc_a + pltpu.roll(acc_b, N_CMP_PAD - 1, 0)
        h = z * _sigmoid(z)
        out = _dot(h.astype(BF16), w2_ref[kv]).astype(o_ref.dtype)
        for g in range(N_KV_GROUPS):
            o_ref[0, g, :, kv * HEAD_DIM:(kv + 1) * HEAD_DIM] = out[:, g * HEAD_DIM:(g + 1) * HEAD_DIM]


def compress(p, col_block, pos4, w1bd, w2bd):
    b, s, _ = p.shape
    assert s == N_CMP_PAD * CMP_STRIDE
    gw2 = 2 * N_KV_GROUPS * HEAD_DIM
    return pl.pallas_call(
        _compress_kernel,
        grid=(b,),
        in_specs=[pl.BlockSpec((1, s, gw2), lambda bi: (bi, 0, col_block)),
                  pl.BlockSpec(pos4.shape, lambda bi: (0, 0, 0)),
                  pl.BlockSpec(w1bd.shape, lambda bi: (0, 0, 0, 0)),
                  pl.BlockSpec(w2bd.shape, lambda bi: (0, 0, 0))],
        out_specs=pl.BlockSpec((1, N_KV_GROUPS, N_CMP_PAD, 2 * HEAD_DIM), lambda bi: (bi, 0, 0, 0)),
        out_shape=jax.ShapeDtypeStruct((b, N_KV_GROUPS, N_CMP_PAD, 2 * HEAD_DIM), BF16),
        scratch_shapes=[pltpu.VMEM((gw2 // LANES, s, LANES), F32)],
        compiler_params=_params(1),
        name="nsa_compress",
    )(p, pos4, w1bd, w2bd)


def _nsa_kernel(q_ref, kv_ref, kvc_ref, gl_ref, selg_ref, biasn_ref, biasc_ref, ovt_ref, exp_ref,
                o_ref, selexp_ref, m_ref, l_ref, acc_ref, *, tq):
    qi = pl.program_id(2)
    t0 = pl.multiple_of(qi * tq, tq)
    r = HEADS_PER_GROUP
    rows = r * tq
    hd = HEAD_DIM

    q = q_ref[0]
    qs = jnp.concatenate([q[:, i * hd:(i + 1) * hd] for i in range(r)], axis=0)
    qs = (qs.astype(F32) * (1.0 / math.sqrt(hd))).astype(BF16)

    kvc = kvc_ref[0, 0]
    s = _dot_nt(qs, kvc[:, :hd]) + biasc_ref[0, 0]
    row = lax.broadcasted_iota(jnp.int32, (rows, N_CMP_PAD), 0)
    col = lax.broadcasted_iota(jnp.int32, (rows, N_CMP_PAD), 1)
    t_row = t0 + (row & (tq - 1))
    vis = (col * CMP_STRIDE + (CMP_BLOCK - 1)) <= t_row
    s = jnp.where(vis, s, NEG)
    m = jnp.max(s, axis=-1, keepdims=True)
    p = jnp.where(vis, jnp.exp(s - m), 0.0)
    p = p / jnp.maximum(jnp.sum(p, axis=-1, keepdims=True), 1e-30)
    o_cmp = _dot(p.astype(BF16), kvc[:, hd:])

    ps = p[0:tq] + p[tq:2 * tq] + p[2 * tq:3 * tq] + p[3 * tq:4 * tq]
    ps_hi = ps.astype(BF16)
    ps_lo = (ps - ps_hi.astype(F32)).astype(BF16)
    ovt = ovt_ref[...]
    n_sel = ovt.shape[0]
    imp_t = _dot_nt(ovt, ps_hi) + _dot_nt(ovt, ps_lo)
    mi = lax.broadcasted_iota(jnp.int32, (n_sel, tq), 0)
    blk = (t0 + lax.broadcasted_iota(jnp.int32, (n_sel, tq), 1)) >> int(math.log2(SEL_BLOCK))
    forced = (mi == 0) | (mi == blk) | (mi == blk - 1)
    score = jnp.where(forced, jnp.inf, jnp.where(mi <= blk, imp_t, -jnp.inf))
    cnt = jnp.zeros((n_sel, tq), F32)
    for mp in range(n_sel):
        other = score[mp:mp + 1, :]
        ahead = (other > score) | ((other == score) & (mi > mp))
        cnt = cnt + jnp.where(ahead, 1.0, 0.0)
    sel_t = jnp.where(cnt < float(min(SEL_TOP_N, n_sel)), 1.0, 0.0)
    sel_t = jnp.concatenate([sel_t, jnp.zeros((LANES - n_sel, tq), F32)], axis=0)
    sel = sel_t.T.astype(BF16)
    selexp_ref[...] = _dot(sel, exp_ref[...])

    def scores(k0, kcol, use_sel):
        k = kv_ref[0, pl.ds(k0, tq), kcol * hd:(kcol + 1) * hd]
        sc = _dot_nt(qs, k)
        if use_sel:
            keep = selexp_ref[:, pl.ds(k0, tq)] > 0.5
            sc = jnp.where(keep[None], sc.reshape(r, tq, tq), NEG).reshape(rows, tq)
        return sc

    def first_tile(sc, k0, vcol):
        v = kv_ref[0, pl.ds(k0, tq), vcol * hd:(vcol + 1) * hd]
        m_new = jnp.max(sc, axis=-1, keepdims=True)
        pp = jnp.exp(sc - m_new)
        m_ref[...] = m_new
        l_ref[...] = jnp.sum(pp, axis=-1, keepdims=True)
        acc_ref[...] = _dot(pp.astype(BF16), v)

    def next_tile(sc, k0, vcol):
        v = kv_ref[0, pl.ds(k0, tq), vcol * hd:(vcol + 1) * hd]
        m_old = m_ref[...]
        m_new = jnp.maximum(m_old, jnp.max(sc, axis=-1, keepdims=True))
        alpha = jnp.exp(m_old - m_new)
        pp = jnp.exp(sc - m_new)
        m_ref[...] = m_new
        l_ref[...] = alpha * l_ref[...] + jnp.sum(pp, axis=-1, keepdims=True)
        acc_ref[...] = alpha * acc_ref[...] + _dot(pp.astype(BF16), v)

    k_prev = pl.multiple_of(jnp.maximum(t0 - tq, 0), tq)

    first_tile(scores(t0, 0, True) + biasn_ref[0, 0], t0, 1)

    @pl.when(qi >= 1)
    def _():
        next_tile(scores(k_prev, 0, True) + biasn_ref[0, 1], k_prev, 1)

    def far_body(j, carry):
        k0 = pl.multiple_of(j * tq, tq)
        next_tile(scores(k0, 0, True), k0, 1)
        return carry

    lax.fori_loop(0, jnp.maximum(qi - 1, 0), far_body, 0)
    o_slc = acc_ref[...] / l_ref[...]

    n_win = WINDOW // tq
    first_tile(scores(t0, 2, False) + biasn_ref[0, 0], t0, 3)

    @pl.when(qi >= 1)
    def _():
        next_tile(scores(k_prev, 2, False) + biasn_ref[0, 1], k_prev, 3)

    def full_body(j, carry):
        k0 = pl.multiple_of(j * tq, tq)
        next_tile(scores(k0, 2, False), k0, 3)
        return carry

    lax.fori_loop(jnp.maximum(qi - (n_win - 1), 0), jnp.maximum(qi - 1, 0), full_body, 0)

    @pl.when(qi >= n_win)
    def _():
        k0 = pl.multiple_of(jnp.maximum(t0 - WINDOW, 0), tq)
        ri = lax.broadcasted_iota(jnp.int32, (rows, tq), 0) & (tq - 1)
        ci = lax.broadcasted_iota(jnp.int32, (rows, tq), 1)
        next_tile(jnp.where(ci > ri, scores(k0, 2, False), NEG), k0, 3)

    o_win = acc_ref[...] / l_ref[...]

    sig = _sigmoid(_dot(gl_ref[0], selg_ref[0]))

    def gate(j):
        return jnp.concatenate([sig[:, j * r + i:j * r + i + 1] for i in range(r)], axis=0)

    o = gate(0) * o_cmp + gate(1) * o_slc + gate(2) * o_win
    o_ref[0] = jnp.concatenate([o[i * tq:(i + 1) * tq] for i in range(r)], axis=1).astype(o_ref.dtype)


def nsa_attention(p, kvc, selg, biasn, biasc, ovt, expand, *, tq, q_block, kv_block, gl_block):
    b, s, _ = p.shape
    g = N_KV_GROUPS
    nq = s // tq
    rows = HEADS_PER_GROUP * tq
    assert WINDOW % tq == 0 and WINDOW // tq >= 2 and tq % LANES == 0
    return pl.pallas_call(
        functools.partial(_nsa_kernel, tq=tq),
        grid=(b, g, nq),
        in_specs=[pl.BlockSpec((1, tq, GROUP_WIDTH), lambda bi, gi, qi: (bi, qi, q_block + gi)),
                  pl.BlockSpec((1, s, GROUP_WIDTH), lambda bi, gi, qi: (bi, 0, kv_block + gi)),
                  pl.BlockSpec((1, 1, N_CMP_PAD, 2 * HEAD_DIM), lambda bi, gi, qi: (bi, gi, 0, 0)),
                  pl.BlockSpec((1, tq, LANES), lambda bi, gi, qi: (bi, qi, gl_block)),
                  pl.BlockSpec((1, LANES, LANES), lambda bi, gi, qi: (gi, 0, 0)),
                  pl.BlockSpec((1, 2, rows, tq), lambda bi, gi, qi: (gi, 0, 0, 0)),
                  pl.BlockSpec((1, 1, rows, N_CMP_PAD), lambda bi, gi, qi: (gi, qi, 0, 0)),
                  pl.BlockSpec(ovt.shape, lambda bi, gi, qi: (0, 0)),
                  pl.BlockSpec(expand.shape, lambda bi, gi, qi: (0, 0))],
        out_specs=pl.BlockSpec((1, tq, GROUP_WIDTH), lambda bi, gi, qi: (bi, qi, gi)),
        out_shape=jax.ShapeDtypeStruct((b, s, g * GROUP_WIDTH), BF16),
        scratch_shapes=[pltpu.VMEM((tq, s), F32),
                        pltpu.VMEM((rows, 1), F32),
                        pltpu.VMEM((rows, 1), F32),
                        pltpu.VMEM((rows, HEAD_DIM), F32)],
        compiler_params=_params(3),
        name="nsa_attention",
    )(p, p, kvc, p, selg, biasn, biasc, ovt, expand)


def _merge_out_kernel(gc_ref, ga_ref, conv_ref, nsa_ref, w_ref, x_ref, o_ref):
    y = (_sigmoid(gc_ref[...].astype(F32)) * conv_ref[...].astype(F32)
         + _sigmoid(ga_ref[...].astype(F32)) * nsa_ref[...].astype(F32))
    o_ref[...] = x_ref[...] + _dot(y.astype(BF16), w_ref[...])


def merge_out(p2, conv, nsa, w, x, *, tm, gc_block, ga_block):
    m, d = x.shape
    tile = lambda blk: pl.BlockSpec((tm, d), lambda i: (i, blk))
    return pl.pallas_call(
        _merge_out_kernel,
        grid=(m // tm,),
        in_specs=[tile(gc_block), tile(ga_block), tile(0), tile(0),
                  pl.BlockSpec((d, d), lambda i: (0, 0)), tile(0)],
        out_specs=tile(0),
        out_shape=jax.ShapeDtypeStruct((m, d), F32),
        compiler_params=_params(1),
        name="merge_out",
    )(p2, p2, conv, nsa, w, x)


def _xattn_kernel(x_ref, g_ref, wq_ref, kv_ref, wo_ref, o_ref):
    x = x_ref[0]
    d = x.shape[-1]
    hd = d // X_HEADS
    h = _rms_norm(x, g_ref[...]).astype(BF16)
    q = (_dot(h, wq_ref[...]) * (1.0 / math.sqrt(hd))).astype(BF16)
    kv = kv_ref[0]
    outs = []
    for i in range(X_HEADS):
        s = _dot_nt(q[:, i * hd:(i + 1) * hd], kv[:, i * hd:(i + 1) * hd])
        p = jnp.exp(s - jnp.max(s, axis=-1, keepdims=True))
        l = jnp.sum(p, axis=-1, keepdims=True)
        outs.append(_dot(p.astype(BF16), kv[:, d + i * hd:d + (i + 1) * hd]) / l)
    o = jnp.concatenate(outs, axis=1).astype(BF16)
    o_ref[0] = x + _dot(o, wo_ref[...])


def cross_attention_block(x, g, wq, kv, wo, *, tm):
    b, s, d = x.shape
    mlen = kv.shape[1]
    return pl.pallas_call(
        _xattn_kernel,
        grid=(b, s // tm),
        in_specs=[pl.BlockSpec((1, tm, d), lambda bi, si: (bi, si, 0)),
                  pl.BlockSpec((1, d), lambda bi, si: (0, 0)),
                  pl.BlockSpec((d, d), lambda bi, si: (0, 0)),
                  pl.BlockSpec((1, mlen, 2 * d), lambda bi, si: (bi, 0, 0)),
                  pl.BlockSpec((d, d), lambda bi, si: (0, 0))],
        out_specs=pl.BlockSpec((1, tm, d), lambda bi, si: (bi, si, 0)),
        out_shape=jax.ShapeDtypeStruct((b, s, d), F32),
        compiler_params=_params(2),
        name="cross_attention",
    )(x, g, wq, kv, wo)


def _ffn_kernel(x_ref, g_ref, wa_ref, wb_ref, wo_ref, o_ref, xn_ref, acc_ref):
    f = pl.program_id(1)

    @pl.when(f == 0)
    def _():
        xn_ref[...] = _rms_norm(x_ref[...], g_ref[...]).astype(BF16)

    xn = xn_ref[...]
    a = _dot(xn, wa_ref[...])
    bb = _dot(xn, wb_ref[...])
    part = _dot((a * _sigmoid(a) * bb).astype(BF16), wo_ref[...])

    @pl.when(f == 0)
    def _():
        acc_ref[...] = part

    @pl.when(f > 0)
    def _():
        acc_ref[...] = acc_ref[...] + part

    @pl.when(f == pl.num_programs(1) - 1)
    def _():
        o_ref[...] = x_ref[...] + acc_ref[...]


def ffn_block(x, g, w_in, w_out, *, tm, tf):
    m, d = x.shape
    f = w_out.shape[0]
    nf = f // tf
    return pl.pallas_call(
        _ffn_kernel,
        grid=(m // tm, nf),
        in_specs=[pl.BlockSpec((tm, d), lambda i, j: (i, 0)),
                  pl.BlockSpec((1, d), lambda i, j: (0, 0)),
                  pl.BlockSpec((d, tf), lambda i, j: (0, j)),
                  pl.BlockSpec((d, tf), lambda i, j: (0, nf + j)),
                  pl.BlockSpec((tf, d), lambda i, j: (j, 0))],
        out_specs=pl.BlockSpec((tm, d), lambda i, j: (i, 0)),
        out_shape=jax.ShapeDtypeStruct((m, d), F32),
        scratch_shapes=[pltpu.VMEM((tm, d), BF16), pltpu.VMEM((tm, d), F32)],
        compiler_params=_params(2),
        name="ffn_block",
    )(x, g, w_in, w_in, w_out)


def _final_norm_kernel(x_ref, g_ref, o_ref):
    o_ref[...] = _rms_norm(x_ref[...], g_ref[...])


def final_norm(x, g, *, tm):
    m, d = x.shape
    return pl.pallas_call(
        _final_norm_kernel,
        grid=(m // tm,),
        in_specs=[pl.BlockSpec((tm, d), lambda i: (i, 0)), pl.BlockSpec((1, d), lambda i: (0, 0))],
        out_specs=pl.BlockSpec((tm, d), lambda i: (i, 0)),
        out_shape=jax.ShapeDtypeStruct((m, d), F32),
        compiler_params=_params(1),
        name="final_norm",
    )(x, g)


def _t5_bucket(dist):
    n = jnp.maximum(dist, 0)
    max_exact = REL_BUCKETS // 2
    nf = jnp.maximum(n, 1).astype(F32)
    large = max_exact + (jnp.log(nf / max_exact) / math.log(REL_MAX_DIST / max_exact)
                         * (REL_BUCKETS - max_exact)).astype(jnp.int32)
    large = jnp.minimum(large, REL_BUCKETS - 1)
    return jnp.where(n < max_exact, n, large)


def _bias_tables(rel_bias, s, tq):
    g, r = N_KV_GROUPS, HEADS_PER_GROUP
    nq = s // tq
    far = rel_bias[_t5_bucket(jnp.asarray(s, jnp.int32))]
    i = jnp.arange(tq)[:, None]
    j = jnp.arange(tq)[None, :]

    def near(dist):
        bt = (rel_bias[_t5_bucket(dist)] - far).transpose(2, 0, 1)
        bt = jnp.where((dist >= 0)[None], bt, NEG)
        return bt.reshape(g, r * tq, tq)

    biasn = jnp.stack([near(i - j), near(i - j + tq)], axis=1)

    t = jnp.arange(s)[:, None]
    c_end = jnp.arange(N_CMP_PAD)[None, :] * CMP_STRIDE + CMP_BLOCK - 1
    bc = rel_bias[_t5_bucket(t - c_end)]
    bc = bc.reshape(nq, tq, N_CMP_PAD, g, r).transpose(3, 0, 4, 1, 2)
    biasc = bc.reshape(g, nq, r * tq, N_CMP_PAD)
    return biasn.astype(F32), biasc.astype(F32)


def _static_tables(s):
    n_sel = s // SEL_BLOCK
    jj = np.arange(N_CMP_PAD)[None, :] * CMP_STRIDE
    mm0 = np.arange(n_sel)[:, None] * SEL_BLOCK
    n_cmp = (s - CMP_BLOCK) // CMP_STRIDE + 1
    ovt = ((jj < mm0 + SEL_BLOCK) & (jj + CMP_BLOCK > mm0) & (np.arange(N_CMP_PAD)[None, :] < n_cmp))
    expand = np.zeros((LANES, s), np.float32)
    expand[np.arange(s) // SEL_BLOCK, np.arange(s)] = 1.0
    selg = np.zeros((N_KV_GROUPS, LANES, LANES), np.float32)
    for g in range(N_KV_GROUPS):
        for c in range(3 * HEADS_PER_GROUP):
            selg[g, 32 * g + c, c] = 1.0
    d = np.arange(LANES + 1, s + 1).astype(np.float32)
    big = 16 + (np.log(d / 16) / math.log(REL_MAX_DIST / 16) * 16).astype(np.int32)
    assert np.all(np.minimum(big, REL_BUCKETS - 1) == REL_BUCKETS - 1)
    return (jnp.asarray(ovt.astype(np.float32), BF16), jnp.asarray(expand, BF16), jnp.asarray(selg, BF16))


def _in_proj_perm(d):
    g, hd, hpg = N_KV_GROUPS, HEAD_DIM, HEADS_PER_GROUP
    kvw = g * hd
    o_conv, o_q = 0, 2 * d
    o_kc = o_q + N_HEADS * hd
    o_vc, o_ks, o_vs, o_kw, o_vw = (o_kc + i * kvw for i in range(1, 6))
    o_gate = o_kc + 6 * kvw
    o_gc = o_gate + 3 * N_HEADS
    o_ga = o_gc + d
    cols = list(range(o_conv, o_conv + 2 * d)) + list(range(o_q, o_q + N_HEADS * hd))
    for gi in range(g):
        for base in (o_ks, o_vs, o_kw, o_vw):
            cols += list(range(base + gi * hd, base + (gi + 1) * hd))
    cols += list(range(o_gc, o_gc + d)) + list(range(o_ga, o_ga + d))
    cols += list(range(o_kc, o_kc + kvw)) + list(range(o_vc, o_vc + kvw))
    gate_cols = [-1] * LANES
    for gi in range(g):
        for j in range(3):
            for i in range(hpg):
                gate_cols[32 * gi + j * hpg + i] = o_gate + j * N_HEADS + gi * hpg + i
    cols += gate_cols
    cols += [-1] * ((-len(cols)) % (IN_PROJ_N_TILES * LANES))
    cols = np.asarray(cols)
    return np.where(cols < 0, 0, cols), (cols >= 0)


def kernel(x, mem, norm_mix_g, w_in, conv_dw_w, conv_dw_b, conv_ln_g, conv_ln_b, conv_pw_w, cmp_pos,
           cmp_w1, cmp_w2, w_out, norm_x_g, xq_w, xkv_w, xo_w, norm_ffn_g, ffn_in_w, ffn_out_w,
           rel_bias, final_norm_g):
    b, s, d = x.shape
    depth = w_in.shape[0]
    mlen = mem.shape[1]
    m = b * s
    tq = 128

    perm, keep = _in_proj_perm(d)
    n_p = perm.shape[0]
    w_in_p = jnp.where(jnp.asarray(keep), jnp.take(w_in.astype(BF16), jnp.asarray(perm), axis=2), 0)
    q_block = (2 * d) // GROUP_WIDTH
    kv_block = q_block + N_KV_GROUPS
    gc_block = (3 * d + N_KV_GROUPS * GROUP_WIDTH) // d
    ga_block = gc_block + 1
    cmp_block = (6 * d) // (2 * N_KV_GROUPS * HEAD_DIM)
    gl_block = (6 * d + 2 * N_KV_GROUPS * HEAD_DIM) // LANES
    tn = n_p // IN_PROJ_N_TILES

    ovt, expand, selg = _static_tables(s)
    biasn, biasc = _bias_tables(rel_bias, s, tq)

    eye_g = jnp.eye(N_KV_GROUPS, dtype=F32)
    row = lambda v: v.reshape(1, -1)

    xf = x.reshape(m, d)
    for l in range(depth):
        p = norm_matmul(xf, row(norm_mix_g[l]), w_in_p[l], tm=1024, tn=tn)
        p3 = p.reshape(b, s, n_p)
        conv = conv_module(p3, conv_dw_w[l], row(conv_dw_b[l]), row(conv_ln_g[l]), row(conv_ln_b[l]),
                           conv_pw_w[l].astype(BF16), t=256)
        w1bd = jnp.einsum("gh,kldc->klgdhc", eye_g, cmp_w1[l]).reshape(
            2, CMP_BLOCK, N_KV_GROUPS * HEAD_DIM, N_KV_GROUPS * HEAD_DIM).astype(BF16)
        w2bd = jnp.einsum("gh,kdc->kgdhc", eye_g, cmp_w2[l]).reshape(
            2, N_KV_GROUPS * HEAD_DIM, N_KV_GROUPS * HEAD_DIM).astype(BF16)
        pos4 = jnp.tile(cmp_pos[l], (1, 1, N_KV_GROUPS))
        kvc = compress(p3, cmp_block, pos4, w1bd, w2bd)
        nsa = nsa_attention(p3, kvc, selg, biasn, biasc, ovt, expand, tq=tq,
                            q_block=q_block, kv_block=kv_block, gl_block=gl_block)
        xf = merge_out(p, conv.reshape(m, d), nsa.reshape(m, d), w_out[l].astype(BF16), xf,
                       tm=512, gc_block=gc_block, ga_block=ga_block)
        kvx = matmul(mem.reshape(b * mlen, d), xkv_w[l].astype(BF16), tm=mlen)
        xf = cross_attention_block(xf.reshape(b, s, d), row(norm_x_g[l]), xq_w[l].astype(BF16),
                                   kvx.reshape(b, mlen, 2 * d), xo_w[l].astype(BF16), tm=512).reshape(m, d)
        xf = ffn_block(xf, row(norm_ffn_g[l]), ffn_in_w[l].astype(BF16), ffn_out_w[l].astype(BF16),
                       tm=512, tf=1408)
    return final_norm(xf, row(final_norm_g), tm=1024).reshape(b, s, d)
```

```python
import functools
import math

import numpy as np
import jax
import jax.numpy as jnp
from jax import lax
from jax.experimental import pallas as pl
from jax.experimental.pallas import tpu as pltpu

F32 = jnp.float32
BF16 = jnp.bfloat16

HEAD_DIM = 64
N_KV_GROUPS = 4
HEADS_PER_GROUP = 4
N_HEADS = N_KV_GROUPS * HEADS_PER_GROUP
GROUP_WIDTH = HEADS_PER_GROUP * HEAD_DIM
CMP_BLOCK = 32
CMP_STRIDE = 16
SEL_BLOCK = 64
SEL_TOP_N = 16
WINDOW = 512
CONV_WIDTH = 31
X_HEADS = 4
REL_BUCKETS = 32
REL_MAX_DIST = 128
EPS = 1e-6
NEG = -1e30

LANES = 128
CONV_HALO = 32
N_CMP_PAD = 128
IN_PROJ_N_TILES = 6
VMEM_LIMIT = 56 * 1024 * 1024


def _sigmoid(x):
    return 1.0 / (1.0 + jnp.exp(-x))


def _dot(a, b):
    return jnp.dot(a, b, preferred_element_type=F32)


def _dot_nt(a, b):
    return lax.dot_general(a, b, (((1,), (1,)), ((), ())), preferred_element_type=F32)


def _rms_norm(x, g):
    return x * lax.rsqrt(jnp.mean(x * x, axis=-1, keepdims=True) + EPS) * g


def _params(n_axes):
    return pltpu.CompilerParams(dimension_semantics=("arbitrary",) * n_axes,
                                vmem_limit_bytes=VMEM_LIMIT)


def _norm_matmul_kernel(x_ref, g_ref, w_ref, o_ref, xn_ref):
    @pl.when(pl.program_id(1) == 0)
    def _():
        xn_ref[...] = _rms_norm(x_ref[...], g_ref[...]).astype(BF16)

    o_ref[...] = _dot(xn_ref[...], w_ref[...]).astype(o_ref.dtype)


def norm_matmul(x, g, w, *, tm, tn):
    m, k = x.shape
    n = w.shape[1]
    return pl.pallas_call(
        _norm_matmul_kernel,
        grid=(m // tm, n // tn),
        in_specs=[pl.BlockSpec((tm, k), lambda i, j: (i, 0)),
                  pl.BlockSpec((1, k), lambda i, j: (0, 0)),
                  pl.BlockSpec((k, tn), lambda i, j: (0, j))],
        out_specs=pl.BlockSpec((tm, tn), lambda i, j: (i, j)),
        out_shape=jax.ShapeDtypeStruct((m, n), BF16),
        scratch_shapes=[pltpu.VMEM((tm, k), BF16)],
        compiler_params=_params(2),
        name="norm_matmul",
    )(x, g, w)


def _matmul_kernel(x_ref, w_ref, o_ref):
    o_ref[...] = _dot(x_ref[...].astype(BF16), w_ref[...]).astype(o_ref.dtype)


def matmul(x, w, *, tm):
    m, k = x.shape
    n = w.shape[1]
    return pl.pallas_call(
        _matmul_kernel,
        grid=(m // tm,),
        in_specs=[pl.BlockSpec((tm, k), lambda i: (i, 0)),
                  pl.BlockSpec((k, n), lambda i: (0, 0))],
        out_specs=pl.BlockSpec((tm, n), lambda i: (i, 0)),
        out_shape=jax.ShapeDtypeStruct((m, n), BF16),
        compiler_params=_params(1),
        name="matmul",
    )(x, w)


def _conv_kernel(a_ref, gt_ref, dww_ref, dwb_ref, lng_ref, lnb_ref, pw_ref, o_ref, buf_ref, *, t):
    s = pl.program_id(1)

    @pl.when(s == 0)
    def _():
        buf_ref[0:CONV_HALO, :] = jnp.zeros((CONV_HALO, buf_ref.shape[1]), F32)

    @pl.when(s > 0)
    def _():
        buf_ref[0:CONV_HALO, :] = buf_ref[t:t + CONV_HALO, :]

    a = a_ref[0].astype(F32)
    gt = gt_ref[0].astype(F32)
    buf_ref[CONV_HALO:CONV_HALO + t, :] = a * _sigmoid(gt)

    base = CONV_HALO - (CONV_WIDTH - 1)
    acc = dww_ref[0:1, :] * buf_ref[base:base + t, :]
    for k in range(1, CONV_WIDTH):
        acc = acc + dww_ref[k:k + 1, :] * buf_ref[base + k:base + k + t, :]
    v = acc + dwb_ref[...]

    mu = jnp.mean(v, axis=-1, keepdims=True)
    vc = v - mu
    y = vc * lax.rsqrt(jnp.mean(vc * vc, axis=-1, keepdims=True) + EPS)
    y = y * lng_ref[...] + lnb_ref[...]
    y = y * _sigmoid(y)
    o_ref[0] = _dot(y.astype(BF16), pw_ref[...]).astype(o_ref.dtype)


def conv_module(p, dww, dwb, lng, lnb, pw, *, t):
    b, s, _ = p.shape
    c = pw.shape[0]
    vec = lambda: pl.BlockSpec((1, c), lambda bi, si: (0, 0))
    return pl.pallas_call(
        functools.partial(_conv_kernel, t=t),
        grid=(b, s // t),
        in_specs=[pl.BlockSpec((1, t, c), lambda bi, si: (bi, si, 0)),
                  pl.BlockSpec((1, t, c), lambda bi, si: (bi, si, 1)),
                  pl.BlockSpec((CONV_WIDTH, c), lambda bi, si: (0, 0)),
                  vec(), vec(), vec(),
                  pl.BlockSpec((c, c), lambda bi, si: (0, 0))],
        out_specs=pl.BlockSpec((1, t, c), lambda bi, si: (bi, si, 0)),
        out_shape=jax.ShapeDtypeStruct((b, s, c), BF16),
        scratch_shapes=[pltpu.VMEM((CONV_HALO + t, c), F32)],
        compiler_params=_params(2),
        name="conv_module",
    )(p, p, dww, dwb, lng, lnb, pw)


def _compress_kernel(x_ref, pos_ref, w1_ref, w2_ref, o_ref, xf_ref):
    half = CMP_BLOCK // 2
    gw = N_KV_GROUPS * HEAD_DIM
    n_chunk = gw // LANES
    for c in range(2 * n_chunk):
        xf_ref[c] = x_ref[0, :, c * LANES:(c + 1) * LANES].astype(F32)
    for kv in range(2):
        acc_a = jnp.zeros((N_CMP_PAD, gw), F32)
        acc_b = jnp.zeros((N_CMP_PAD, gw), F32)
        for l in range(half):
            rows = jnp.concatenate(
                [xf_ref[kv * n_chunk + c, pl.ds(l, N_CMP_PAD, stride=CMP_STRIDE), :] for c in range(n_chunk)],
                axis=1)
            ra = (rows + pos_ref[kv, l:l + 1, :]).astype(BF16)
            rb = (rows + pos_ref[kv, half + l:half + l + 1, :]).astype(BF16)
            acc_a = acc_a + _dot(ra, w1_ref[kv, l])
            acc_b = acc_b + _dot(rb, w1_ref[kv, half + l])
        z = acc_a + pltpu.roll(acc_b, N_CMP_PAD - 1, 0)
        h = z * _sigmoid(z)
        out = _dot(h.astype(BF16), w2_ref[kv]).astype(o_ref.dtype)
        for g in range(N_KV_GROUPS):
            o_ref[0, g, :, kv * HEAD_DIM:(kv + 1) * HEAD_DIM] = out[:, g * HEAD_DIM:(g + 1) * HEAD_DIM]


def compress(p, col_block, pos4, w1bd, w2bd):
    b, s, _ = p.shape
    assert s == N_CMP_PAD * CMP_STRIDE
    gw2 = 2 * N_KV_GROUPS * HEAD_DIM
    return pl.pallas_call(
        _compress_kernel,
        grid=(b,),
        in_specs=[pl.BlockSpec((1, s, gw2), lambda bi: (bi, 0, col_block)),
                  pl.BlockSpec(pos4.shape, lambda bi: (0, 0, 0)),
                  pl.BlockSpec(w1bd.shape, lambda bi: (0, 0, 0, 0)),
                  pl.BlockSpec(w2bd.shape, lambda bi: (0, 0, 0))],
        out_specs=pl.BlockSpec((1, N_KV_GROUPS, N_CMP_PAD, 2 * HEAD_DIM), lambda bi: (bi, 0, 0, 0)),
        out_shape=jax.ShapeDtypeStruct((b, N_KV_GROUPS, N_CMP_PAD, 2 * HEAD_DIM), BF16),
        scratch_shapes=[pltpu.VMEM((gw2 // LANES, s, LANES), F32)],
        compiler_params=_params(1),
        name="nsa_compress",
    )(p, pos4, w1bd, w2bd)


def _nsa_kernel(q_ref, kv_ref, kvc_ref, gl_ref, selg_ref, biasn_ref, biasc_ref, ovt_ref, exp_ref,
                o_ref, seladd_ref, vaug_ref, sbuf_ref, mrun_ref, acc_ref, *, tq):
    qi = pl.program_id(2)
    t0 = pl.multiple_of(qi * tq, tq)
    r = HEADS_PER_GROUP
    rows = r * tq
    hd = HEAD_DIM

    q = q_ref[0]
    qs = jnp.concatenate([q[:, i * hd:(i + 1) * hd] for i in range(r)], axis=0)
    qs = (qs.astype(F32) * (1.0 / math.sqrt(hd))).astype(BF16)

    kvc = kvc_ref[0, 0]
    s = _dot_nt(qs, kvc[:, :hd]) + biasc_ref[0, 0]
    row = lax.broadcasted_iota(jnp.int32, (rows, N_CMP_PAD), 0)
    col = lax.broadcasted_iota(jnp.int32, (rows, N_CMP_PAD), 1)
    t_row = t0 + (row & (tq - 1))
    vis = (col * CMP_STRIDE + (CMP_BLOCK - 1)) <= t_row
    s = jnp.where(vis, s, NEG)
    m = jnp.max(s, axis=-1, keepdims=True)
    p = jnp.where(vis, jnp.exp(s - m), 0.0)
    p = p / jnp.maximum(jnp.sum(p, axis=-1, keepdims=True), 1e-30)
    o_cmp = _dot(p.astype(BF16), kvc[:, hd:])

    ps = p[0:tq] + p[tq:2 * tq] + p[2 * tq:3 * tq] + p[3 * tq:4 * tq]
    ps_hi = ps.astype(BF16)
    ps_lo = (ps - ps_hi.astype(F32)).astype(BF16)
    ovt = ovt_ref[...]
    n_sel = ovt.shape[0]
    imp_t = _dot_nt(ovt, ps_hi) + _dot_nt(ovt, ps_lo)
    mi = lax.broadcasted_iota(jnp.int32, (n_sel, tq), 0)
    blk = (t0 + lax.broadcasted_iota(jnp.int32, (n_sel, tq), 1)) >> int(math.log2(SEL_BLOCK))
    forced = (mi == 0) | (mi == blk) | (mi == blk - 1)
    score = jnp.where(forced, jnp.inf, jnp.where(mi <= blk, imp_t, -jnp.inf))
    cnt = jnp.zeros((n_sel, tq), F32)
    for mp in range(n_sel):
        other = score[mp:mp + 1, :]
        ahead = (other > score) | ((other == score) & (mi > mp))
        cnt = cnt + jnp.where(ahead, 1.0, 0.0)
    sel_t = jnp.where(cnt < float(min(SEL_TOP_N, n_sel)), 1.0, 0.0)
    sel_t = jnp.concatenate([sel_t, jnp.zeros((LANES - n_sel, tq), F32)], axis=0)
    sel = sel_t.T.astype(BF16)
    seladd_ref[...] = (_dot(sel, exp_ref[...]) - 1.0) * (-NEG)

    @pl.when(qi == 0)
    def _():
        ones = jnp.ones((kv_ref.shape[1], hd), BF16)
        vaug_ref[...] = jnp.concatenate(
            [kv_ref[0, :, hd:2 * hd], ones, kv_ref[0, :, 3 * hd:4 * hd], ones], axis=1)

    def pass1(k0, kcol, add):
        k = kv_ref[0, pl.ds(k0, tq), kcol * hd:(kcol + 1) * hd]
        sc = _dot_nt(qs, k)
        if add is not None:
            sc = sc + add
        sbuf_ref[:, pl.ds(k0, tq)] = sc
        mx = sc[:, 0:LANES]
        for c in range(1, tq // LANES):
            mx = jnp.maximum(mx, sc[:, c * LANES:(c + 1) * LANES])
        mrun_ref[...] = jnp.maximum(mrun_ref[...], mx)

    def sel_add(k0):
        a = seladd_ref[:, pl.ds(k0, tq)]
        return jnp.broadcast_to(a[None], (r, tq, tq)).reshape(rows, tq)

    def start_branch():
        mrun_ref[...] = jnp.full((rows, LANES), NEG, F32)
        acc_ref[...] = jnp.zeros((rows, LANES), F32)

    def finish_max():
        m = jnp.max(mrun_ref[...], axis=-1, keepdims=True)
        mrun_ref[...] = jnp.broadcast_to(m, (rows, LANES))

    def pass2(k0, vcol):
        v = vaug_ref[pl.ds(k0, tq), vcol * LANES:(vcol + 1) * LANES]
        m = mrun_ref[...]
        sc = sbuf_ref[:, pl.ds(k0, tq)]
        pp = jnp.concatenate([jnp.exp(sc[:, c * LANES:(c + 1) * LANES] - m)
                              for c in range(tq // LANES)], axis=1)
        acc_ref[...] = acc_ref[...] + _dot(pp.astype(BF16), v)

    def finish_branch():
        acc = acc_ref[...]
        return acc[:, :hd] / acc[:, hd:]

    def tile_loop(lo, hi, fn):
        def body(j, carry):
            fn(pl.multiple_of(j * tq, tq))
            return carry
        lax.fori_loop(lo, hi, body, 0)

    k_prev = pl.multiple_of(jnp.maximum(t0 - tq, 0), tq)

    start_branch()
    pass1(t0, 0, sel_add(t0) + biasn_ref[0, 0])

    @pl.when(qi >= 1)
    def _():
        pass1(k_prev, 0, sel_add(k_prev) + biasn_ref[0, 1])

    tile_loop(0, jnp.maximum(qi - 1, 0), lambda k0: pass1(k0, 0, sel_add(k0)))
    finish_max()
    tile_loop(0, qi + 1, lambda k0: pass2(k0, 0))
    o_slc = finish_branch()

    n_win = WINDOW // tq
    start_branch()
    pass1(t0, 2, biasn_ref[0, 0])

    @pl.when(qi >= 1)
    def _():
        pass1(k_prev, 2, biasn_ref[0, 1])

    tile_loop(jnp.maximum(qi - (n_win - 1), 0), jnp.maximum(qi - 1, 0), lambda k0: pass1(k0, 2, None))

    @pl.when(qi >= n_win)
    def _():
        k0 = pl.multiple_of(jnp.maximum(t0 - WINDOW, 0), tq)
        ri = lax.broadcasted_iota(jnp.int32, (rows, tq), 0) & (tq - 1)
        ci = lax.broadcasted_iota(jnp.int32, (rows, tq), 1)
        pass1(k0, 2, jnp.where(ci > ri, 0.0, NEG))

    finish_max()
    tile_loop(jnp.maximum(qi - n_win, 0), qi + 1, lambda k0: pass2(k0, 1))
    o_win = finish_branch()

    sig = _sigmoid(_dot(gl_ref[0], selg_ref[0]))

    def gate(j):
        return jnp.concatenate([sig[:, j * r + i:j * r + i + 1] for i in range(r)], axis=0)

    o = gate(0) * o_cmp + gate(1) * o_slc + gate(2) * o_win
    o_ref[0] = jnp.concatenate([o[i * tq:(i + 1) * tq] for i in range(r)], axis=1).astype(o_ref.dtype)


def nsa_attention(p, kvc, selg, biasn, biasc, ovt, expand, *, tq, q_block, kv_block, gl_block):
    b, s, _ = p.shape
    g = N_KV_GROUPS
    nq = s // tq
    rows = HEADS_PER_GROUP * tq
    assert WINDOW % tq == 0 and WINDOW // tq >= 2 and tq % LANES == 0
    return pl.pallas_call(
        functools.partial(_nsa_kernel, tq=tq),
        grid=(b, g, nq),
        in_specs=[pl.BlockSpec((1, tq, GROUP_WIDTH), lambda bi, gi, qi: (bi, qi, q_block + gi)),
                  pl.BlockSpec((1, s, GROUP_WIDTH), lambda bi, gi, qi: (bi, 0, kv_block + gi)),
                  pl.BlockSpec((1, 1, N_CMP_PAD, 2 * HEAD_DIM), lambda bi, gi, qi: (bi, gi, 0, 0)),
                  pl.BlockSpec((1, tq, LANES), lambda bi, gi, qi: (bi, qi, gl_block)),
                  pl.BlockSpec((1, LANES, LANES), lambda bi, gi, qi: (gi, 0, 0)),
                  pl.BlockSpec((1, 2, rows, tq), lambda bi, gi, qi: (gi, 0, 0, 0)),
                  pl.BlockSpec((1, 1, rows, N_CMP_PAD), lambda bi, gi, qi: (gi, qi, 0, 0)),
                  pl.BlockSpec(ovt.shape, lambda bi, gi, qi: (0, 0)),
                  pl.BlockSpec(expand.shape, lambda bi, gi, qi: (0, 0))],
        out_specs=pl.BlockSpec((1, tq, GROUP_WIDTH), lambda bi, gi, qi: (bi, qi, gi)),
        out_shape=jax.ShapeDtypeStruct((b, s, g * GROUP_WIDTH), BF16),
        scratch_shapes=[pltpu.VMEM((tq, s), F32),
                        pltpu.VMEM((s, 2 * LANES), BF16),
                        pltpu.VMEM((rows, s), F32),
                        pltpu.VMEM((rows, LANES), F32),
                        pltpu.VMEM((rows, LANES), F32)],
        compiler_params=_params(3),
        name="nsa_attention",
    )(p, p, kvc, p, selg, biasn, biasc, ovt, expand)


def _merge_out_kernel(gc_ref, ga_ref, conv_ref, nsa_ref, w_ref, x_ref, o_ref):
    y = (_sigmoid(gc_ref[...].astype(F32)) * conv_ref[...].astype(F32)
         + _sigmoid(ga_ref[...].astype(F32)) * nsa_ref[...].astype(F32))
    o_ref[...] = x_ref[...] + _dot(y.astype(BF16), w_ref[...])


def merge_out(p2, conv, nsa, w, x, *, tm, gc_block, ga_block):
    m, d = x.shape
    tile = lambda blk: pl.BlockSpec((tm, d), lambda i: (i, blk))
    return pl.pallas_call(
        _merge_out_kernel,
        grid=(m // tm,),
        in_specs=[tile(gc_block), tile(ga_block), tile(0), tile(0),
                  pl.BlockSpec((d, d), lambda i: (0, 0)), tile(0)],
        out_specs=tile(0),
        out_shape=jax.ShapeDtypeStruct((m, d), F32),
        compiler_params=_params(1),
        name="merge_out",
    )(p2, p2, conv, nsa, w, x)


def _xattn_kernel(x_ref, g_ref, wq_ref, kv_ref, wo_ref, o_ref):
    x = x_ref[0]
    d = x.shape[-1]
    hd = d // X_HEADS
    h = _rms_norm(x, g_ref[...]).astype(BF16)
    q = (_dot(h, wq_ref[...]) * (1.0 / math.sqrt(hd))).astype(BF16)
    kv = kv_ref[0]
    outs = []
    for i in range(X_HEADS):
        s = _dot_nt(q[:, i * hd:(i + 1) * hd], kv[:, i * hd:(i + 1) * hd])
        p = jnp.exp(s - jnp.max(s, axis=-1, keepdims=True))
        l = jnp.sum(p, axis=-1, keepdims=True)
        outs.append(_dot(p.astype(BF16), kv[:, d + i * hd:d + (i + 1) * hd]) / l)
    o = jnp.concatenate(outs, axis=1).astype(BF16)
    o_ref[0] = x + _dot(o, wo_ref[...])


def cross_attention_block(x, g, wq, kv, wo, *, tm):
    b, s, d = x.shape
    mlen = kv.shape[1]
    return pl.pallas_call(
        _xattn_kernel,
        grid=(b, s // tm),
        in_specs=[pl.BlockSpec((1, tm, d), lambda bi, si: (bi, si, 0)),
                  pl.BlockSpec((1, d), lambda bi, si: (0, 0)),
                  pl.BlockSpec((d, d), lambda bi, si: (0, 0)),
                  pl.BlockSpec((1, mlen, 2 * d), lambda bi, si: (bi, 0, 0)),
                  pl.BlockSpec((d, d), lambda bi, si: (0, 0))],
        out_specs=pl.BlockSpec((1, tm, d), lambda bi, si: (bi, si, 0)),
        out_shape=jax.ShapeDtypeStruct((b, s, d), F32),
        compiler_params=_params(2),
        name="cross_attention",
    )(x, g, wq, kv, wo)


def _ffn_kernel(x_ref, g_ref, wa_ref, wb_ref, wo_ref, o_ref, xn_ref, acc_ref):
    f = pl.program_id(1)

    @pl.when(f == 0)
    def _():
        xn_ref[...] = _rms_norm(x_ref[...], g_ref[...]).astype(BF16)

    xn = xn_ref[...]
    a = _dot(xn, wa_ref[...])
    bb = _dot(xn, wb_ref[...])
    part = _dot((a * _sigmoid(a) * bb).astype(BF16), wo_ref[...])

    @pl.when(f == 0)
    def _():
        acc_ref[...] = part

    @pl.when(f > 0)
    def _():
        acc_ref[...] = acc_ref[...] + part

    @pl.when(f == pl.num_programs(1) - 1)
    def _():
        o_ref[...] = x_ref[...] + acc_ref[...]


def ffn_block(x, g, w_in, w_out, *, tm, tf):
    m, d = x.shape
    f = w_out.shape[0]
    nf = f // tf
    return pl.pallas_call(
        _ffn_kernel,
        grid=(m // tm, nf),
        in_specs=[pl.BlockSpec((tm, d), lambda i, j: (i, 0)),
                  pl.BlockSpec((1, d), lambda i, j: (0, 0)),
                  pl.BlockSpec((d, tf), lambda i, j: (0, j)),
                  pl.BlockSpec((d, tf), lambda i, j: (0, nf + j)),
                  pl.BlockSpec((tf, d), lambda i, j: (j, 0))],
        out_specs=pl.BlockSpec((tm, d), lambda i, j: (i, 0)),
        out_shape=jax.ShapeDtypeStruct((m, d), F32),
        scratch_shapes=[pltpu.VMEM((tm, d), BF16), pltpu.VMEM((tm, d), F32)],
        compiler_params=_params(2),
        name="ffn_block",
    )(x, g, w_in, w_in, w_out)


def _final_norm_kernel(x_ref, g_ref, o_ref):
    o_ref[...] = _rms_norm(x_ref[...], g_ref[...])


def final_norm(x, g, *, tm):
    m, d = x.shape
    return pl.pallas_call(
        _final_norm_kernel,
        grid=(m // tm,),
        in_specs=[pl.BlockSpec((tm, d), lambda i: (i, 0)), pl.BlockSpec((1, d), lambda i: (0, 0))],
        out_specs=pl.BlockSpec((tm, d), lambda i: (i, 0)),
        out_shape=jax.ShapeDtypeStruct((m, d), F32),
        compiler_params=_params(1),
        name="final_norm",
    )(x, g)


def _t5_bucket(dist):
    n = jnp.maximum(dist, 0)
    max_exact = REL_BUCKETS // 2
    nf = jnp.maximum(n, 1).astype(F32)
    large = max_exact + (jnp.log(nf / max_exact) / math.log(REL_MAX_DIST / max_exact)
                         * (REL_BUCKETS - max_exact)).astype(jnp.int32)
    large = jnp.minimum(large, REL_BUCKETS - 1)
    return jnp.where(n < max_exact, n, large)


def _bias_tables(rel_bias, s, tq):
    g, r = N_KV_GROUPS, HEADS_PER_GROUP
    nq = s // tq
    far = rel_bias[_t5_bucket(jnp.asarray(s, jnp.int32))]
    i = jnp.arange(tq)[:, None]
    j = jnp.arange(tq)[None, :]

    def near(dist):
        bt = (rel_bias[_t5_bucket(dist)] - far).transpose(2, 0, 1)
        bt = jnp.where((dist >= 0)[None], bt, NEG)
        return bt.reshape(g, r * tq, tq)

    biasn = jnp.stack([near(i - j), near(i - j + tq)], axis=1)

    t = jnp.arange(s)[:, None]
    c_end = jnp.arange(N_CMP_PAD)[None, :] * CMP_STRIDE + CMP_BLOCK - 1
    bc = rel_bias[_t5_bucket(t - c_end)]
    bc = bc.reshape(nq, tq, N_CMP_PAD, g, r).transpose(3, 0, 4, 1, 2)
    biasc = bc.reshape(g, nq, r * tq, N_CMP_PAD)
    return biasn.astype(F32), biasc.astype(F32)


def _static_tables(s):
    n_sel = s // SEL_BLOCK
    jj = np.arange(N_CMP_PAD)[None, :] * CMP_STRIDE
    mm0 = np.arange(n_sel)[:, None] * SEL_BLOCK
    n_cmp = (s - CMP_BLOCK) // CMP_STRIDE + 1
    ovt = ((jj < mm0 + SEL_BLOCK) & (jj + CMP_BLOCK > mm0) & (np.arange(N_CMP_PAD)[None, :] < n_cmp))
    expand = np.zeros((LANES, s), np.float32)
    expand[np.arange(s) // SEL_BLOCK, np.arange(s)] = 1.0
    selg = np.zeros((N_KV_GROUPS, LANES, LANES), np.float32)
    for g in range(N_KV_GROUPS):
        for j in range(3):
            for i in range(HEADS_PER_GROUP):
                selg[g, j * N_HEADS + g * HEADS_PER_GROUP + i, j * HEADS_PER_GROUP + i] = 1.0
    d = np.arange(LANES + 1, s + 1).astype(np.float32)
    big = 16 + (np.log(d / 16) / math.log(REL_MAX_DIST / 16) * 16).astype(np.int32)
    assert np.all(np.minimum(big, REL_BUCKETS - 1) == REL_BUCKETS - 1)
    return (jnp.asarray(ovt.astype(np.float32), BF16), jnp.asarray(expand, BF16), jnp.asarray(selg, BF16))


def _in_proj_perm(d):
    g, hd, hpg = N_KV_GROUPS, HEAD_DIM, HEADS_PER_GROUP
    kvw = g * hd
    o_conv, o_q = 0, 2 * d
    o_kc = o_q + N_HEADS * hd
    o_vc, o_ks, o_vs, o_kw, o_vw = (o_kc + i * kvw for i in range(1, 6))
    o_gate = o_kc + 6 * kvw
    o_gc = o_gate + 3 * N_HEADS
    o_ga = o_gc + d
    segs = [(o_conv, 2 * d + N_HEADS * hd)]
    for gi in range(g):
        segs += [(base + gi * hd, hd) for base in (o_ks, o_vs, o_kw, o_vw)]
    segs += [(o_gc, 2 * d), (o_kc, 2 * kvw), (o_gate, 3 * N_HEADS)]
    n_used = sum(n for _, n in segs)
    n_pad = LANES - 3 * N_HEADS
    n_pad += (-(n_used + n_pad)) % (IN_PROJ_N_TILES * LANES)
    return segs, n_pad


def _permute_in_proj(w_in):
    segs, n_pad = _in_proj_perm(w_in.shape[1])
    w = w_in.astype(BF16)
    parts = [w[:, :, a:a + n] for a, n in segs]
    parts.append(jnp.zeros(w.shape[:2] + (n_pad,), BF16))
    return jnp.concatenate(parts, axis=2)


def kernel(x, mem, norm_mix_g, w_in, conv_dw_w, conv_dw_b, conv_ln_g, conv_ln_b, conv_pw_w, cmp_pos,
           cmp_w1, cmp_w2, w_out, norm_x_g, xq_w, xkv_w, xo_w, norm_ffn_g, ffn_in_w, ffn_out_w,
           rel_bias, final_norm_g):
    b, s, d = x.shape
    depth = w_in.shape[0]
    mlen = mem.shape[1]
    m = b * s
    tq = 256

    w_in_p = _permute_in_proj(w_in)
    n_p = w_in_p.shape[2]
    q_block = (2 * d) // GROUP_WIDTH
    kv_block = q_block + N_KV_GROUPS
    gc_block = (3 * d + N_KV_GROUPS * GROUP_WIDTH) // d
    ga_block = gc_block + 1
    cmp_block = (6 * d) // (2 * N_KV_GROUPS * HEAD_DIM)
    gl_block = (6 * d + 2 * N_KV_GROUPS * HEAD_DIM) // LANES
    tn = n_p // IN_PROJ_N_TILES

    ovt, expand, selg = _static_tables(s)
    biasn, biasc = _bias_tables(rel_bias, s, tq)

    eye_g = jnp.eye(N_KV_GROUPS, dtype=F32)
    row = lambda v: v.reshape(1, -1)

    xf = x.reshape(m, d)
    for l in range(depth):
        p = norm_matmul(xf, row(norm_mix_g[l]), w_in_p[l], tm=1024, tn=tn)
        p3 = p.reshape(b, s, n_p)
        conv = conv_module(p3, conv_dw_w[l], row(conv_dw_b[l]), row(conv_ln_g[l]), row(conv_ln_b[l]),
                           conv_pw_w[l].astype(BF16), t=256)
        w1bd = jnp.einsum("gh,kldc->klgdhc", eye_g, cmp_w1[l]).reshape(
            2, CMP_BLOCK, N_KV_GROUPS * HEAD_DIM, N_KV_GROUPS * HEAD_DIM).astype(BF16)
        w2bd = jnp.einsum("gh,kdc->kgdhc", eye_g, cmp_w2[l]).reshape(
            2, N_KV_GROUPS * HEAD_DIM, N_KV_GROUPS * HEAD_DIM).astype(BF16)
        pos4 = jnp.tile(cmp_pos[l], (1, 1, N_KV_GROUPS))
        kvc = compress(p3, cmp_block, pos4, w1bd, w2bd)
        nsa = nsa_attention(p3, kvc, selg, biasn, biasc, ovt, expand, tq=tq,
                            q_block=q_block, kv_block=kv_block, gl_block=gl_block)
        xf = merge_out(p, conv.reshape(m, d), nsa.reshape(m, d), w_out[l].astype(BF16), xf,
                       tm=512, gc_block=gc_block, ga_block=ga_block)
        kvx = matmul(mem.reshape(b * mlen, d), xkv_w[l].astype(BF16), tm=mlen)
        xf = cross_attention_block(xf.reshape(b, s, d), row(norm_x_g[l]), xq_w[l].astype(BF16),
                                   kvx.reshape(b, mlen, 2 * d), xo_w[l].astype(BF16), tm=512).reshape(m, d)
        xf = ffn_block(xf, row(norm_ffn_g[l]), ffn_in_w[l].astype(BF16), ffn_out_w[l].astype(BF16),
                       tm=512, tf=1408)
    return final_norm(xf, row(final_norm_g), tm=1024).reshape(b, s, d)
```

```python
import functools
import math

import numpy as np
import jax
import jax.numpy as jnp
from jax import lax
from jax.experimental import pallas as pl
from jax.experimental.pallas import tpu as pltpu

F32 = jnp.float32
BF16 = jnp.bfloat16

HEAD_DIM = 64
N_KV_GROUPS = 4
HEADS_PER_GROUP = 4
N_HEADS = N_KV_GROUPS * HEADS_PER_GROUP
GROUP_WIDTH = HEADS_PER_GROUP * HEAD_DIM
CMP_BLOCK = 32
CMP_STRIDE = 16
SEL_BLOCK = 64
SEL_TOP_N = 16
WINDOW = 512
CONV_WIDTH = 31
X_HEADS = 4
REL_BUCKETS = 32
REL_MAX_DIST = 128
EPS = 1e-6
NEG = -1e30
LOG2E = math.log2(math.e)

LANES = 128
SUBLANES = 8
CONV_ROWS = 64
CONV_HALO = 32
N_CMP_PAD = 128
IN_PROJ_N_TILES = 6
VMEM_LIMIT = 56 * 1024 * 1024


def _sigmoid(x):
    return 1.0 / (1.0 + jnp.exp(-x))


def _dot(a, b):
    return jnp.dot(a, b, preferred_element_type=F32)


def _dot_nt(a, b):
    return lax.dot_general(a, b, (((1,), (1,)), ((), ())), preferred_element_type=F32)


def _rms_norm(x, g):
    return x * lax.rsqrt(jnp.mean(x * x, axis=-1, keepdims=True) + EPS) * g


def _params(n_axes):
    return pltpu.CompilerParams(dimension_semantics=("arbitrary",) * n_axes,
                                vmem_limit_bytes=VMEM_LIMIT)


def _norm_matmul_kernel(x_ref, g_ref, w_ref, o_ref, xn_ref):
    @pl.when(pl.program_id(1) == 0)
    def _():
        xn_ref[...] = _rms_norm(x_ref[...], g_ref[...]).astype(BF16)

    o_ref[...] = _dot(xn_ref[...], w_ref[...]).astype(o_ref.dtype)


def norm_matmul(x, g, w, *, tm, tn):
    m, k = x.shape
    n = w.shape[1]
    return pl.pallas_call(
        _norm_matmul_kernel,
        grid=(m // tm, n // tn),
        in_specs=[pl.BlockSpec((tm, k), lambda i, j: (i, 0)),
                  pl.BlockSpec((1, k), lambda i, j: (0, 0)),
                  pl.BlockSpec((k, tn), lambda i, j: (0, j))],
        out_specs=pl.BlockSpec((tm, tn), lambda i, j: (i, j)),
        out_shape=jax.ShapeDtypeStruct((m, n), BF16),
        scratch_shapes=[pltpu.VMEM((tm, k), BF16)],
        compiler_params=_params(2),
        name="norm_matmul",
    )(x, g, w)


def _matmul_kernel(x_ref, w_ref, o_ref):
    o_ref[...] = _dot(x_ref[...].astype(BF16), w_ref[...]).astype(o_ref.dtype)


def matmul(x, w, *, tm):
    m, k = x.shape
    n = w.shape[1]
    return pl.pallas_call(
        _matmul_kernel,
        grid=(m // tm,),
        in_specs=[pl.BlockSpec((tm, k), lambda i: (i, 0)),
                  pl.BlockSpec((k, n), lambda i: (0, 0))],
        out_specs=pl.BlockSpec((tm, n), lambda i: (i, 0)),
        out_shape=jax.ShapeDtypeStruct((m, n), BF16),
        compiler_params=_params(1),
        name="matmul",
    )(x, w)


def _conv_kernel(a_ref, gt_ref, dww_ref, dwb_ref, lng_ref, lnb_ref, pw_ref, o_ref, buf_ref, cbuf_ref, *, t):
    s = pl.program_id(1)

    @pl.when(s == 0)
    def _():
        buf_ref[0:CONV_HALO, :] = jnp.zeros((CONV_HALO, buf_ref.shape[1]), F32)

    @pl.when(s > 0)
    def _():
        buf_ref[0:CONV_HALO, :] = buf_ref[t:t + CONV_HALO, :]

    a = a_ref[0].astype(F32)
    gt = gt_ref[0].astype(F32)
    buf_ref[CONV_HALO:CONV_HALO + t, :] = a * _sigmoid(gt)

    base = CONV_HALO - (CONV_WIDTH - 1)

    def conv_rows(ci, carry):
        r0 = pl.multiple_of(ci * CONV_ROWS, CONV_ROWS)
        for c in range(buf_ref.shape[1] // LANES):
            lanes = slice(c * LANES, (c + 1) * LANES)
            acc = dwb_ref[:, lanes]
            for b in range(SUBLANES):
                n = CONV_ROWS + (SUBLANES if b else 0)
                part = None
                for k in range(CONV_WIDTH):
                    if (base + k) % SUBLANES != b:
                        continue
                    term = dww_ref[k:k + 1, lanes] * buf_ref[pl.ds(r0 + (base + k - b), n), lanes]
                    part = term if part is None else part + term
                acc = acc + part[b:b + CONV_ROWS]
            cbuf_ref[pl.ds(r0, CONV_ROWS), lanes] = acc
        return carry

    lax.fori_loop(0, t // CONV_ROWS, conv_rows, 0)
    v = cbuf_ref[...]

    mu = jnp.mean(v, axis=-1, keepdims=True)
    vc = v - mu
    y = vc * lax.rsqrt(jnp.mean(vc * vc, axis=-1, keepdims=True) + EPS)
    y = y * lng_ref[...] + lnb_ref[...]
    y = y * _sigmoid(y)
    o_ref[0] = _dot(y.astype(BF16), pw_ref[...]).astype(o_ref.dtype)


def conv_module(p, dww, dwb, lng, lnb, pw, *, t):
    b, s, _ = p.shape
    c = pw.shape[0]
    vec = lambda: pl.BlockSpec((1, c), lambda bi, si: (0, 0))
    return pl.pallas_call(
        functools.partial(_conv_kernel, t=t),
        grid=(b, s // t),
        in_specs=[pl.BlockSpec((1, t, c), lambda bi, si: (bi, si, 0)),
                  pl.BlockSpec((1, t, c), lambda bi, si: (bi, si, 1)),
                  pl.BlockSpec((CONV_WIDTH, c), lambda bi, si: (0, 0)),
                  vec(), vec(), vec(),
                  pl.BlockSpec((c, c), lambda bi, si: (0, 0))],
        out_specs=pl.BlockSpec((1, t, c), lambda bi, si: (bi, si, 0)),
        out_shape=jax.ShapeDtypeStruct((b, s, c), BF16),
        scratch_shapes=[pltpu.VMEM((CONV_HALO + t, c), F32), pltpu.VMEM((t, c), F32)],
        compiler_params=_params(2),
        name="conv_module",
    )(p, p, dww, dwb, lng, lnb, pw)


def _compress_kernel(x_ref, pos_ref, w1_ref, w2_ref, o_ref, xf_ref):
    half = CMP_BLOCK // 2
    gw = N_KV_GROUPS * HEAD_DIM
    n_chunk = gw // LANES
    for c in range(2 * n_chunk):
        xf_ref[c] = x_ref[0, :, c * LANES:(c + 1) * LANES].astype(F32)
    for kv in range(2):
        acc_a = jnp.zeros((N_CMP_PAD, gw), F32)
        acc_b = jnp.zeros((N_CMP_PAD, gw), F32)
        for l in range(half):
            rows = jnp.concatenate(
                [xf_ref[kv * n_chunk + c, pl.ds(l, N_CMP_PAD, stride=CMP_STRIDE), :] for c in range(n_chunk)],
                axis=1)
            ra = (rows + pos_ref[kv, l:l + 1, :]).astype(BF16)
            rb = (rows + pos_ref[kv, half + l:half + l + 1, :]).astype(BF16)
            acc_a = acc_a + _dot(ra, w1_ref[kv, l])
            acc_b = acc_b + _dot(rb, w1_ref[kv, half + l])
        z = acc_a + pltpu.roll(acc_b, N_CMP_PAD - 1, 0)
        h = z * _sigmoid(z)
        out = _dot(h.astype(BF16), w2_ref[kv]).astype(o_ref.dtype)
        for g in range(N_KV_GROUPS):
            o_ref[0, g, :, (1 - kv) * HEAD_DIM:(2 - kv) * HEAD_DIM] = out[:, g * HEAD_DIM:(g + 1) * HEAD_DIM]


def compress(p, col_block, pos4, w1bd, w2bd):
    b, s, _ = p.shape
    assert s == N_CMP_PAD * CMP_STRIDE
    gw2 = 2 * N_KV_GROUPS * HEAD_DIM
    return pl.pallas_call(
        _compress_kernel,
        grid=(b,),
        in_specs=[pl.BlockSpec((1, s, gw2), lambda bi: (bi, 0, col_block)),
                  pl.BlockSpec(pos4.shape, lambda bi: (0, 0, 0)),
                  pl.BlockSpec(w1bd.shape, lambda bi: (0, 0, 0, 0)),
                  pl.BlockSpec(w2bd.shape, lambda bi: (0, 0, 0))],
        out_specs=pl.BlockSpec((1, N_KV_GROUPS, N_CMP_PAD, 2 * HEAD_DIM), lambda bi: (bi, 0, 0, 0)),
        out_shape=jax.ShapeDtypeStruct((b, N_KV_GROUPS, N_CMP_PAD, 2 * HEAD_DIM), BF16),
        scratch_shapes=[pltpu.VMEM((gw2 // LANES, s, LANES), F32)],
        compiler_params=_params(1),
        name="nsa_compress",
    )(p, pos4, w1bd, w2bd)


def _nsa_kernel(q_ref, kv_ref, kvc_ref, gl_ref, gexp_ref, biasn_ref, biasc_ref, ovt_ref, exp_ref,
                o_ref, seladd_ref, vaug_ref, sbuf_ref, wbuf_ref, mrun_ref, acc_ref, gate_ref, part_ref,
                *, tq):
    qi = pl.program_id(2)
    t0 = pl.multiple_of(qi * tq, tq)
    r = HEADS_PER_GROUP
    rows = r * tq
    hd = HEAD_DIM
    n_lt = tq // LANES

    q = q_ref[0]
    qs = jnp.concatenate([q[:, i * hd:(i + 1) * hd] for i in range(r)], axis=0)

    sig = _sigmoid(gl_ref[0].astype(F32))
    sig_hi = sig.astype(BF16)
    sig_lo = (sig - sig_hi.astype(F32)).astype(BF16)
    for j in range(3):
        gate_ref[j] = jnp.concatenate(
            [_dot(sig_hi, gexp_ref[0, j * r + i]) + _dot(sig_lo, gexp_ref[0, j * r + i]) for i in range(r)],
            axis=0)

    kvc = kvc_ref[0, 0]
    s = _dot_nt(qs, kvc[:, hd:]) + biasc_ref[0, 0]
    row = lax.broadcasted_iota(jnp.int32, (rows, N_CMP_PAD), 0)
    col = lax.broadcasted_iota(jnp.int32, (rows, N_CMP_PAD), 1)
    t_row = t0 + (row & (tq - 1))
    vis = (col * CMP_STRIDE + (CMP_BLOCK - 1)) <= t_row
    s = jnp.where(vis, s, NEG)
    m = jnp.max(s, axis=-1, keepdims=True)
    p = jnp.where(vis, jnp.exp2(s - m), 0.0)
    p = p / jnp.maximum(jnp.sum(p, axis=-1, keepdims=True), 1e-30)
    part_ref[...] = gate_ref[0] * _dot(p.astype(BF16), kvc)

    ps = p[0:tq] + p[tq:2 * tq] + p[2 * tq:3 * tq] + p[3 * tq:4 * tq]
    ps_hi = ps.astype(BF16)
    ps_lo = (ps - ps_hi.astype(F32)).astype(BF16)
    ovt = ovt_ref[...]
    n_sel = ovt.shape[0]
    imp_t = _dot_nt(ovt, ps_hi) + _dot_nt(ovt, ps_lo)
    mi = lax.broadcasted_iota(jnp.int32, (n_sel, tq), 0)
    blk = (t0 + lax.broadcasted_iota(jnp.int32, (n_sel, tq), 1)) >> int(math.log2(SEL_BLOCK))
    forced = (mi == 0) | (mi == blk) | (mi == blk - 1)
    score = jnp.where(forced, jnp.inf, jnp.where(mi <= blk, imp_t, -jnp.inf))
    cnt = jnp.zeros((n_sel, tq), F32)
    for mp in range(n_sel):
        other = score[mp:mp + 1, :]
        ahead = (other > score) | ((other == score) & (mi > mp))
        cnt = cnt + jnp.where(ahead, 1.0, 0.0)
    drop_t = jnp.where(cnt < float(min(SEL_TOP_N, n_sel)), 0.0, 1.0)
    drop_t = jnp.concatenate([drop_t, jnp.zeros((LANES - n_sel, tq), F32)], axis=0)
    seladd_ref[...] = _dot(drop_t.T.astype(BF16), exp_ref[...])

    @pl.when(qi == 0)
    def _():
        ones = jnp.ones((kv_ref.shape[1], hd), BF16)
        vaug_ref[...] = jnp.concatenate(
            [kv_ref[0, :, hd:2 * hd], ones, kv_ref[0, :, 3 * hd:4 * hd], ones], axis=1)

    def score_tile(k0, kcol, add):
        k = kv_ref[0, pl.ds(k0, tq), kcol * hd:(kcol + 1) * hd]
        return _dot_nt(qs, k) + add

    def lane_max(sc):
        mx = sc[:, 0:LANES]
        for c in range(1, n_lt):
            mx = jnp.maximum(mx, sc[:, c * LANES:(c + 1) * LANES])
        return mx

    def over_heads(a):
        return jnp.broadcast_to(a[None], (r, tq, tq)).reshape(rows, tq)

    def probs(sc, m_rep):
        return jnp.concatenate([jnp.exp2(sc[:, c * LANES:(c + 1) * LANES] - m_rep)
                                for c in range(n_lt)], axis=1).astype(BF16)

    def row_max(mx):
        return jnp.broadcast_to(jnp.max(mx, axis=-1, keepdims=True), (rows, LANES))

    k_prev = pl.multiple_of(jnp.maximum(t0 - tq, 0), tq)
    k_edge = pl.multiple_of(jnp.maximum(t0 - 2 * tq, 0), tq)
    off_prev = jnp.where(qi >= 1, 0.0, NEG)
    off_edge = jnp.where(qi >= 2, 0.0, NEG)
    bias_diag = biasn_ref[0, 0]
    bias_prev = biasn_ref[0, 1] + off_prev

    s_prev = score_tile(k_prev, 0, over_heads(seladd_ref[:, pl.ds(k_prev, tq)]) + bias_prev)
    sbuf_ref[:, pl.ds(k_prev, tq)] = s_prev
    s_diag = score_tile(t0, 0, over_heads(seladd_ref[:, pl.ds(t0, tq)]) + bias_diag)
    sbuf_ref[:, pl.ds(t0, tq)] = s_diag
    mrun_ref[0] = jnp.maximum(lane_max(s_prev), lane_max(s_diag))

    ri = lax.broadcasted_iota(jnp.int32, (rows, tq), 0) & (tq - 1)
    ci = lax.broadcasted_iota(jnp.int32, (rows, tq), 1)
    w_edge = score_tile(k_edge, 2, jnp.where(ci > ri, off_edge, NEG))
    w_prev = score_tile(k_prev, 2, bias_prev)
    w_diag = score_tile(t0, 2, bias_diag)
    wbuf_ref[:, 0:tq] = w_edge
    wbuf_ref[:, tq:2 * tq] = w_prev
    wbuf_ref[:, 2 * tq:3 * tq] = w_diag
    mrun_ref[1] = jnp.maximum(jnp.maximum(lane_max(w_edge), lane_max(w_prev)), lane_max(w_diag))

    def far1(k0):
        sc = score_tile(k0, 0, over_heads(seladd_ref[:, pl.ds(k0, tq)]))
        sbuf_ref[:, pl.ds(k0, tq)] = sc
        return lane_max(sc)

    n_far = jnp.maximum(qi - 1, 0)

    def far1_pair(jj, carry):
        k0 = pl.multiple_of(2 * jj * tq, tq)
        mrun_ref[0] = jnp.maximum(mrun_ref[0], jnp.maximum(far1(k0), far1(k0 + tq)))
        return carry

    lax.fori_loop(0, n_far // 2, far1_pair, 0)

    @pl.when(n_far % 2 == 1)
    def _():
        mrun_ref[0] = jnp.maximum(mrun_ref[0], far1(pl.multiple_of((n_far - 1) * tq, tq)))

    m_sel = row_max(mrun_ref[0])
    mrun_ref[0] = m_sel
    m_win = row_max(mrun_ref[1])
    acc_w = (_dot(probs(wbuf_ref[:, 0:tq], m_win), vaug_ref[pl.ds(k_edge, tq), LANES:2 * LANES])
             + _dot(probs(wbuf_ref[:, tq:2 * tq], m_win), vaug_ref[pl.ds(k_prev, tq), LANES:2 * LANES])
             + _dot(probs(wbuf_ref[:, 2 * tq:3 * tq], m_win), vaug_ref[pl.ds(t0, tq), LANES:2 * LANES]))
    part_ref[...] = part_ref[...] + gate_ref[2] * acc_w / pltpu.roll(acc_w, hd, 1)
    acc_ref[...] = _dot(probs(sbuf_ref[:, pl.ds(t0, tq)], m_sel), vaug_ref[pl.ds(t0, tq), 0:LANES])

    def sel2(k0):
        return _dot(probs(sbuf_ref[:, pl.ds(k0, tq)], mrun_ref[0]), vaug_ref[pl.ds(k0, tq), 0:LANES])

    @pl.when(qi >= 1)
    def _():
        acc_ref[...] = acc_ref[...] + sel2(k_prev)

    def far2_pair(jj, carry):
        k0 = pl.multiple_of(2 * jj * tq, tq)
        acc_ref[...] = acc_ref[...] + (sel2(k0) + sel2(k0 + tq))
        return carry

    lax.fori_loop(0, n_far // 2, far2_pair, 0)

    @pl.when(n_far % 2 == 1)
    def _():
        acc_ref[...] = acc_ref[...] + sel2(pl.multiple_of((n_far - 1) * tq, tq))

    acc_s = acc_ref[...]
    o = part_ref[...] + gate_ref[1] * acc_s / pltpu.roll(acc_s, hd, 1)
    o_ref[0] = jnp.concatenate([o[i * tq:(i + 1) * tq, :hd] for i in range(r)], axis=1).astype(o_ref.dtype)


def nsa_attention(p, kvc, gexp, biasn, biasc, ovt, expand, *, tq, q_block, kv_block, gl_block):
    b, s, _ = p.shape
    g = N_KV_GROUPS
    nq = s // tq
    rows = HEADS_PER_GROUP * tq
    assert WINDOW == 2 * tq and tq % LANES == 0
    return pl.pallas_call(
        functools.partial(_nsa_kernel, tq=tq),
        grid=(b, g, nq),
        in_specs=[pl.BlockSpec((1, tq, GROUP_WIDTH), lambda bi, gi, qi: (bi, qi, q_block + gi)),
                  pl.BlockSpec((1, s, GROUP_WIDTH), lambda bi, gi, qi: (bi, 0, kv_block + gi)),
                  pl.BlockSpec((1, 1, N_CMP_PAD, 2 * HEAD_DIM), lambda bi, gi, qi: (bi, gi, 0, 0)),
                  pl.BlockSpec((1, tq, LANES), lambda bi, gi, qi: (bi, qi, gl_block)),
                  pl.BlockSpec((1,) + gexp.shape[1:], lambda bi, gi, qi: (gi, 0, 0, 0)),
                  pl.BlockSpec((1, 2, rows, tq), lambda bi, gi, qi: (gi, 0, 0, 0)),
                  pl.BlockSpec((1, 1, rows, N_CMP_PAD), lambda bi, gi, qi: (gi, qi, 0, 0)),
                  pl.BlockSpec(ovt.shape, lambda bi, gi, qi: (0, 0)),
                  pl.BlockSpec(expand.shape, lambda bi, gi, qi: (0, 0))],
        out_specs=pl.BlockSpec((1, tq, GROUP_WIDTH), lambda bi, gi, qi: (bi, qi, gi)),
        out_shape=jax.ShapeDtypeStruct((b, s, g * GROUP_WIDTH), BF16),
        scratch_shapes=[pltpu.VMEM((tq, s), F32),
                        pltpu.VMEM((s, 2 * LANES), BF16),
                        pltpu.VMEM((rows, s), F32),
                        pltpu.VMEM((rows, 3 * tq), F32),
                        pltpu.VMEM((2, rows, LANES), F32),
                        pltpu.VMEM((rows, LANES), F32),
                        pltpu.VMEM((3, rows, LANES), F32),
                        pltpu.VMEM((rows, LANES), F32)],
        compiler_params=_params(3),
        name="nsa_attention",
    )(p, p, kvc, p, gexp, biasn, biasc, ovt, expand)


def _merge_out_kernel(gc_ref, ga_ref, conv_ref, nsa_ref, w_ref, x_ref, o_ref):
    y = (_sigmoid(gc_ref[...].astype(F32)) * conv_ref[...].astype(F32)
         + _sigmoid(ga_ref[...].astype(F32)) * nsa_ref[...].astype(F32))
    o_ref[...] = x_ref[...] + _dot(y.astype(BF16), w_ref[...])


def merge_out(p2, conv, nsa, w, x, *, tm, gc_block, ga_block):
    m, d = x.shape
    tile = lambda blk: pl.BlockSpec((tm, d), lambda i: (i, blk))
    return pl.pallas_call(
        _merge_out_kernel,
        grid=(m // tm,),
        in_specs=[tile(gc_block), tile(ga_block), tile(0), tile(0),
                  pl.BlockSpec((d, d), lambda i: (0, 0)), tile(0)],
        out_specs=tile(0),
        out_shape=jax.ShapeDtypeStruct((m, d), F32),
        compiler_params=_params(1),
        name="merge_out",
    )(p2, p2, conv, nsa, w, x)


def _xattn_kernel(x_ref, g_ref, wq_ref, kv_ref, wo_ref, o_ref):
    x = x_ref[0]
    d = x.shape[-1]
    hd = d // X_HEADS
    h = _rms_norm(x, g_ref[...]).astype(BF16)
    q = (_dot(h, wq_ref[...]) * (1.0 / math.sqrt(hd))).astype(BF16)
    kv = kv_ref[0]
    outs = []
    for i in range(X_HEADS):
        s = _dot_nt(q[:, i * hd:(i + 1) * hd], kv[:, i * hd:(i + 1) * hd])
        p = jnp.exp(s - jnp.max(s, axis=-1, keepdims=True))
        l = jnp.sum(p, axis=-1, keepdims=True)
        outs.append(_dot(p.astype(BF16), kv[:, d + i * hd:d + (i + 1) * hd]) / l)
    o = jnp.concatenate(outs, axis=1).astype(BF16)
    o_ref[0] = x + _dot(o, wo_ref[...])


def cross_attention_block(x, g, wq, kv, wo, *, tm):
    b, s, d = x.shape
    mlen = kv.shape[1]
    return pl.pallas_call(
        _xattn_kernel,
        grid=(b, s // tm),
        in_specs=[pl.BlockSpec((1, tm, d), lambda bi, si: (bi, si, 0)),
                  pl.BlockSpec((1, d), lambda bi, si: (0, 0)),
                  pl.BlockSpec((d, d), lambda bi, si: (0, 0)),
                  pl.BlockSpec((1, mlen, 2 * d), lambda bi, si: (bi, 0, 0)),
                  pl.BlockSpec((d, d), lambda bi, si: (0, 0))],
        out_specs=pl.BlockSpec((1, tm, d), lambda bi, si: (bi, si, 0)),
        out_shape=jax.ShapeDtypeStruct((b, s, d), F32),
        compiler_params=_params(2),
        name="cross_attention",
    )(x, g, wq, kv, wo)


def _ffn_kernel(x_ref, g_ref, wa_ref, wb_ref, wo_ref, o_ref, xn_ref, acc_ref):
    f = pl.program_id(1)

    @pl.when(f == 0)
    def _():
        xn_ref[...] = _rms_norm(x_ref[...], g_ref[...]).astype(BF16)

    xn = xn_ref[...]
    a = _dot(xn, wa_ref[...])
    bb = _dot(xn, wb_ref[...])
    part = _dot((a * _sigmoid(a) * bb).astype(BF16), wo_ref[...])

    @pl.when(f == 0)
    def _():
        acc_ref[...] = part

    @pl.when(f > 0)
    def _():
        acc_ref[...] = acc_ref[...] + part

    @pl.when(f == pl.num_programs(1) - 1)
    def _():
        o_ref[...] = x_ref[...] + acc_ref[...]


def ffn_block(x, g, w_in, w_out, *, tm, tf):
    m, d = x.shape
    f = w_out.shape[0]
    nf = f // tf
    return pl.pallas_call(
        _ffn_kernel,
        grid=(m // tm, nf),
        in_specs=[pl.BlockSpec((tm, d), lambda i, j: (i, 0)),
                  pl.BlockSpec((1, d), lambda i, j: (0, 0)),
                  pl.BlockSpec((d, tf), lambda i, j: (0, j)),
                  pl.BlockSpec((d, tf), lambda i, j: (0, nf + j)),
                  pl.BlockSpec((tf, d), lambda i, j: (j, 0))],
        out_specs=pl.BlockSpec((tm, d), lambda i, j: (i, 0)),
        out_shape=jax.ShapeDtypeStruct((m, d), F32),
        scratch_shapes=[pltpu.VMEM((tm, d), BF16), pltpu.VMEM((tm, d), F32)],
        compiler_params=_params(2),
        name="ffn_block",
    )(x, g, w_in, w_in, w_out)


def _final_norm_kernel(x_ref, g_ref, o_ref):
    o_ref[...] = _rms_norm(x_ref[...], g_ref[...])


def final_norm(x, g, *, tm):
    m, d = x.shape
    return pl.pallas_call(
        _final_norm_kernel,
        grid=(m // tm,),
        in_specs=[pl.BlockSpec((tm, d), lambda i: (i, 0)), pl.BlockSpec((1, d), lambda i: (0, 0))],
        out_specs=pl.BlockSpec((tm, d), lambda i: (i, 0)),
        out_shape=jax.ShapeDtypeStruct((m, d), F32),
        compiler_params=_params(1),
        name="final_norm",
    )(x, g)


def _t5_bucket(dist):
    n = jnp.maximum(dist, 0)
    max_exact = REL_BUCKETS // 2
    nf = jnp.maximum(n, 1).astype(F32)
    large = max_exact + (jnp.log(nf / max_exact) / math.log(REL_MAX_DIST / max_exact)
                         * (REL_BUCKETS - max_exact)).astype(jnp.int32)
    large = jnp.minimum(large, REL_BUCKETS - 1)
    return jnp.where(n < max_exact, n, large)


def _bias_lookup(rel_bias, dist):
    onehot = jax.nn.one_hot(_t5_bucket(dist), REL_BUCKETS, dtype=F32)
    return jnp.einsum("...b,bh->...h", onehot, rel_bias, precision=lax.Precision.HIGHEST)


def _bias_tables(rel_bias, s, tq):
    g, r = N_KV_GROUPS, HEADS_PER_GROUP
    nq = s // tq
    rel_bias = rel_bias * LOG2E
    far = _bias_lookup(rel_bias, jnp.asarray(s, jnp.int32))
    i = jnp.arange(tq)[:, None]
    j = jnp.arange(tq)[None, :]

    def near(dist):
        bt = (_bias_lookup(rel_bias, dist) - far).transpose(2, 0, 1)
        bt = jnp.where((dist >= 0)[None], bt, NEG)
        return bt.reshape(g, r * tq, tq)

    biasn = jnp.stack([near(i - j), near(i - j + tq)], axis=1)

    t = jnp.arange(s)[:, None]
    c_end = jnp.arange(N_CMP_PAD)[None, :] * CMP_STRIDE + CMP_BLOCK - 1
    bc = _bias_lookup(rel_bias, t - c_end)
    bc = bc.reshape(nq, tq, N_CMP_PAD, g, r).transpose(3, 0, 4, 1, 2)
    biasc = bc.reshape(g, nq, r * tq, N_CMP_PAD)
    return biasn, biasc


def _static_tables(s):
    n_sel = s // SEL_BLOCK
    jj = np.arange(N_CMP_PAD)[None, :] * CMP_STRIDE
    mm0 = np.arange(n_sel)[:, None] * SEL_BLOCK
    n_cmp = (s - CMP_BLOCK) // CMP_STRIDE + 1
    ovt = ((jj < mm0 + SEL_BLOCK) & (jj + CMP_BLOCK > mm0) & (np.arange(N_CMP_PAD)[None, :] < n_cmp))
    expand = np.zeros((LANES, s), np.float32)
    expand[np.arange(s) // SEL_BLOCK, np.arange(s)] = NEG
    gexp = np.zeros((N_KV_GROUPS, 3 * HEADS_PER_GROUP, LANES, LANES), np.float32)
    for g in range(N_KV_GROUPS):
        for j in range(3):
            for i in range(HEADS_PER_GROUP):
                gexp[g, j * HEADS_PER_GROUP + i, j * N_HEADS + g * HEADS_PER_GROUP + i, :] = 1.0
    d = np.arange(LANES + 1, s + 1).astype(np.float32)
    big = 16 + (np.log(d / 16) / math.log(REL_MAX_DIST / 16) * 16).astype(np.int32)
    assert np.all(np.minimum(big, REL_BUCKETS - 1) == REL_BUCKETS - 1)
    return (jnp.asarray(ovt.astype(np.float32), BF16), jnp.asarray(expand, BF16), jnp.asarray(gexp, BF16))


def _in_proj_perm(d):
    g, hd = N_KV_GROUPS, HEAD_DIM
    kvw = g * hd
    o_conv, o_q = 0, 2 * d
    o_kc = o_q + N_HEADS * hd
    o_vc, o_ks, o_vs, o_kw, o_vw = (o_kc + i * kvw for i in range(1, 6))
    o_gate = o_kc + 6 * kvw
    o_gc = o_gate + 3 * N_HEADS
    o_ga = o_gc + d
    segs = [(o_conv, 2 * d + N_HEADS * hd)]
    for gi in range(g):
        segs += [(base + gi * hd, hd) for base in (o_ks, o_vs, o_kw, o_vw)]
    segs += [(o_gc, 2 * d), (o_kc, 2 * kvw), (o_gate, 3 * N_HEADS)]
    n_used = sum(n for _, n in segs)
    n_pad = LANES - 3 * N_HEADS
    n_pad += (-(n_used + n_pad)) % (IN_PROJ_N_TILES * LANES)
    return segs, n_pad


def _permute_in_proj(w_in):
    d = w_in.shape[1]
    segs, n_pad = _in_proj_perm(d)
    q_scale = jnp.ones((w_in.shape[2],), F32).at[2 * d:2 * d + N_HEADS * HEAD_DIM].set(LOG2E / math.sqrt(HEAD_DIM))
    w = (w_in * q_scale).astype(BF16)
    parts = [w[:, :, a:a + n] for a, n in segs]
    parts.append(jnp.zeros(w.shape[:2] + (n_pad,), BF16))
    return jnp.concatenate(parts, axis=2)


def kernel(x, mem, norm_mix_g, w_in, conv_dw_w, conv_dw_b, conv_ln_g, conv_ln_b, conv_pw_w, cmp_pos,
           cmp_w1, cmp_w2, w_out, norm_x_g, xq_w, xkv_w, xo_w, norm_ffn_g, ffn_in_w, ffn_out_w,
           rel_bias, final_norm_g):
    b, s, d = x.shape
    depth = w_in.shape[0]
    mlen = mem.shape[1]
    m = b * s
    tq = 256

    w_in_p = _permute_in_proj(w_in)
    n_p = w_in_p.shape[2]
    q_block = (2 * d) // GROUP_WIDTH
    kv_block = q_block + N_KV_GROUPS
    gc_block = (3 * d + N_KV_GROUPS * GROUP_WIDTH) // d
    ga_block = gc_block + 1
    cmp_block = (6 * d) // (2 * N_KV_GROUPS * HEAD_DIM)
    gl_block = (6 * d + 2 * N_KV_GROUPS * HEAD_DIM) // LANES
    tn = n_p // IN_PROJ_N_TILES

    ovt, expand, gexp = _static_tables(s)
    biasn, biasc = _bias_tables(rel_bias, s, tq)

    eye_g = jnp.eye(N_KV_GROUPS, dtype=F32)
    row = lambda v: v.reshape(1, -1)

    xf = x.reshape(m, d)
    for l in range(depth):
        p = norm_matmul(xf, row(norm_mix_g[l]), w_in_p[l], tm=1024, tn=tn)
        p3 = p.reshape(b, s, n_p)
        conv = conv_module(p3, conv_dw_w[l], row(conv_dw_b[l]), row(conv_ln_g[l]), row(conv_ln_b[l]),
                           conv_pw_w[l].astype(BF16), t=256)
        w1bd = jnp.einsum("gh,kldc->klgdhc", eye_g, cmp_w1[l]).reshape(
            2, CMP_BLOCK, N_KV_GROUPS * HEAD_DIM, N_KV_GROUPS * HEAD_DIM).astype(BF16)
        w2bd = jnp.einsum("gh,kdc->kgdhc", eye_g, cmp_w2[l]).reshape(
            2, N_KV_GROUPS * HEAD_DIM, N_KV_GROUPS * HEAD_DIM).astype(BF16)
        pos4 = jnp.tile(cmp_pos[l], (1, 1, N_KV_GROUPS))
        kvc = compress(p3, cmp_block, pos4, w1bd, w2bd)
        nsa = nsa_attention(p3, kvc, gexp, biasn, biasc, ovt, expand, tq=tq,
                            q_block=q_block, kv_block=kv_block, gl_block=gl_block)
        xf = merge_out(p, conv.reshape(m, d), nsa.reshape(m, d), w_out[l].astype(BF16), xf,
                       tm=512, gc_block=gc_block, ga_block=ga_block)
        kvx = matmul(mem.reshape(b * mlen, d), xkv_w[l].astype(BF16), tm=mlen)
        xf = cross_attention_block(xf.reshape(b, s, d), row(norm_x_g[l]), xq_w[l].astype(BF16),
                                   kvx.reshape(b, mlen, 2 * d), xo_w[l].astype(BF16), tm=512).reshape(m, d)
        xf = ffn_block(xf, row(norm_ffn_g[l]), ffn_in_w[l].astype(BF16), ffn_out_w[l].astype(BF16),
                       tm=512, tf=1408)
    return final_norm(xf, row(final_norm_g), tm=1024).reshape(b, s, d)
```

```python
import functools
import math

import numpy as np
import jax
import jax.numpy as jnp
from jax import lax
from jax.experimental import pallas as pl
from jax.experimental.pallas import tpu as pltpu

F32 = jnp.float32
BF16 = jnp.bfloat16

HEAD_DIM = 64
N_KV_GROUPS = 4
HEADS_PER_GROUP = 4
N_HEADS = N_KV_GROUPS * HEADS_PER_GROUP
GROUP_WIDTH = HEADS_PER_GROUP * HEAD_DIM
CMP_BLOCK = 32
CMP_STRIDE = 16
SEL_BLOCK = 64
SEL_TOP_N = 16
WINDOW = 512
CONV_WIDTH = 31
X_HEADS = 4
REL_BUCKETS = 32
REL_MAX_DIST = 128
EPS = 1e-6
NEG = -1e30
MAX_FLOOR = -1e20
LOG2E = math.log2(math.e)

LANES = 128
SUBLANES = 8
CONV_ROWS = 64
CONV_HALO = 32
N_CMP_PAD = 128
IN_PROJ_N_TILES = 6
VMEM_LIMIT = 56 * 1024 * 1024


def _sigmoid(x):
    return 1.0 / (1.0 + jnp.exp(-x))


def _dot(a, b):
    return jnp.dot(a, b, preferred_element_type=F32)


def _dot_nt(a, b):
    return lax.dot_general(a, b, (((1,), (1,)), ((), ())), preferred_element_type=F32)


def _rms_norm(x, g):
    return x * lax.rsqrt(jnp.mean(x * x, axis=-1, keepdims=True) + EPS) * g


def _params(n_axes):
    return pltpu.CompilerParams(dimension_semantics=("arbitrary",) * n_axes,
                                vmem_limit_bytes=VMEM_LIMIT)


def _norm_matmul_kernel(x_ref, g_ref, w_ref, o_ref, xn_ref):
    @pl.when(pl.program_id(1) == 0)
    def _():
        xn_ref[...] = _rms_norm(x_ref[...], g_ref[...]).astype(BF16)

    o_ref[...] = _dot(xn_ref[...], w_ref[...]).astype(o_ref.dtype)


def norm_matmul(x, g, w, *, tm, tn):
    m, k = x.shape
    n = w.shape[1]
    return pl.pallas_call(
        _norm_matmul_kernel,
        grid=(m // tm, n // tn),
        in_specs=[pl.BlockSpec((tm, k), lambda i, j: (i, 0)),
                  pl.BlockSpec((1, k), lambda i, j: (0, 0)),
                  pl.BlockSpec((k, tn), lambda i, j: (0, j))],
        out_specs=pl.BlockSpec((tm, tn), lambda i, j: (i, j)),
        out_shape=jax.ShapeDtypeStruct((m, n), BF16),
        scratch_shapes=[pltpu.VMEM((tm, k), BF16)],
        compiler_params=_params(2),
        name="norm_matmul",
    )(x, g, w)


def _matmul_kernel(x_ref, w_ref, o_ref):
    o_ref[...] = _dot(x_ref[...].astype(BF16), w_ref[...]).astype(o_ref.dtype)


def matmul(x, w, *, tm):
    m, k = x.shape
    n = w.shape[1]
    return pl.pallas_call(
        _matmul_kernel,
        grid=(m // tm,),
        in_specs=[pl.BlockSpec((tm, k), lambda i: (i, 0)),
                  pl.BlockSpec((k, n), lambda i: (0, 0))],
        out_specs=pl.BlockSpec((tm, n), lambda i: (i, 0)),
        out_shape=jax.ShapeDtypeStruct((m, n), BF16),
        compiler_params=_params(1),
        name="matmul",
    )(x, w)


def _conv_kernel(a_ref, gt_ref, dww_ref, dwb_ref, lng_ref, lnb_ref, pw_ref, o_ref, buf_ref, cbuf_ref, *, t):
    s = pl.program_id(1)

    @pl.when(s == 0)
    def _():
        buf_ref[0:CONV_HALO, :] = jnp.zeros((CONV_HALO, buf_ref.shape[1]), F32)

    @pl.when(s > 0)
    def _():
        buf_ref[0:CONV_HALO, :] = buf_ref[t:t + CONV_HALO, :]

    a = a_ref[0].astype(F32)
    gt = gt_ref[0].astype(F32)
    buf_ref[CONV_HALO:CONV_HALO + t, :] = a * _sigmoid(gt)

    base = CONV_HALO - (CONV_WIDTH - 1)

    def conv_rows(ci, carry):
        r0 = pl.multiple_of(ci * CONV_ROWS, CONV_ROWS)
        for c in range(buf_ref.shape[1] // LANES):
            lanes = slice(c * LANES, (c + 1) * LANES)
            acc = dwb_ref[:, lanes]
            for b in range(SUBLANES):
                n = CONV_ROWS + (SUBLANES if b else 0)
                part = None
                for k in range(CONV_WIDTH):
                    if (base + k) % SUBLANES != b:
                        continue
                    term = dww_ref[k:k + 1, lanes] * buf_ref[pl.ds(r0 + (base + k - b), n), lanes]
                    part = term if part is None else part + term
                acc = acc + part[b:b + CONV_ROWS]
            cbuf_ref[pl.ds(r0, CONV_ROWS), lanes] = acc
        return carry

    lax.fori_loop(0, t // CONV_ROWS, conv_rows, 0)
    v = cbuf_ref[...]

    mu = jnp.mean(v, axis=-1, keepdims=True)
    vc = v - mu
    y = vc * lax.rsqrt(jnp.mean(vc * vc, axis=-1, keepdims=True) + EPS)
    y = y * lng_ref[...] + lnb_ref[...]
    y = y * _sigmoid(y)
    o_ref[0] = _dot(y.astype(BF16), pw_ref[...]).astype(o_ref.dtype)


def conv_module(p, dww, dwb, lng, lnb, pw, *, t):
    b, s, _ = p.shape
    c = pw.shape[0]
    vec = lambda: pl.BlockSpec((1, c), lambda bi, si: (0, 0))
    return pl.pallas_call(
        functools.partial(_conv_kernel, t=t),
        grid=(b, s // t),
        in_specs=[pl.BlockSpec((1, t, c), lambda bi, si: (bi, si, 0)),
                  pl.BlockSpec((1, t, c), lambda bi, si: (bi, si, 1)),
                  pl.BlockSpec((CONV_WIDTH, c), lambda bi, si: (0, 0)),
                  vec(), vec(), vec(),
                  pl.BlockSpec((c, c), lambda bi, si: (0, 0))],
        out_specs=pl.BlockSpec((1, t, c), lambda bi, si: (bi, si, 0)),
        out_shape=jax.ShapeDtypeStruct((b, s, c), BF16),
        scratch_shapes=[pltpu.VMEM((CONV_HALO + t, c), F32), pltpu.VMEM((t, c), F32)],
        compiler_params=_params(2),
        name="conv_module",
    )(p, p, dww, dwb, lng, lnb, pw)


def _compress_kernel(x_ref, pos_ref, w1_ref, w2_ref, o_ref, xf_ref):
    half = CMP_BLOCK // 2
    gw = N_KV_GROUPS * HEAD_DIM
    n_chunk = gw // LANES
    for c in range(2 * n_chunk):
        xf_ref[c] = x_ref[0, :, c * LANES:(c + 1) * LANES].astype(F32)
    for kv in range(2):
        acc_a = jnp.zeros((N_CMP_PAD, gw), F32)
        acc_b = jnp.zeros((N_CMP_PAD, gw), F32)
        for l in range(half):
            rows = jnp.concatenate(
                [xf_ref[kv * n_chunk + c, pl.ds(l, N_CMP_PAD, stride=CMP_STRIDE), :] for c in range(n_chunk)],
                axis=1)
            ra = (rows + pos_ref[kv, l:l + 1, :]).astype(BF16)
            rb = (rows + pos_ref[kv, half + l:half + l + 1, :]).astype(BF16)
            acc_a = acc_a + _dot(ra, w1_ref[kv, l])
            acc_b = acc_b + _dot(rb, w1_ref[kv, half + l])
        z = acc_a + pltpu.roll(acc_b, N_CMP_PAD - 1, 0)
        h = z * _sigmoid(z)
        out = _dot(h.astype(BF16), w2_ref[kv]).astype(o_ref.dtype)
        for g in range(N_KV_GROUPS):
            o_ref[0, g, :, (1 - kv) * HEAD_DIM:(2 - kv) * HEAD_DIM] = out[:, g * HEAD_DIM:(g + 1) * HEAD_DIM]


def compress(p, col_block, pos4, w1bd, w2bd):
    b, s, _ = p.shape
    assert s == N_CMP_PAD * CMP_STRIDE
    gw2 = 2 * N_KV_GROUPS * HEAD_DIM
    return pl.pallas_call(
        _compress_kernel,
        grid=(b,),
        in_specs=[pl.BlockSpec((1, s, gw2), lambda bi: (bi, 0, col_block)),
                  pl.BlockSpec(pos4.shape, lambda bi: (0, 0, 0)),
                  pl.BlockSpec(w1bd.shape, lambda bi: (0, 0, 0, 0)),
                  pl.BlockSpec(w2bd.shape, lambda bi: (0, 0, 0))],
        out_specs=pl.BlockSpec((1, N_KV_GROUPS, N_CMP_PAD, 2 * HEAD_DIM), lambda bi: (bi, 0, 0, 0)),
        out_shape=jax.ShapeDtypeStruct((b, N_KV_GROUPS, N_CMP_PAD, 2 * HEAD_DIM), BF16),
        scratch_shapes=[pltpu.VMEM((gw2 // LANES, s, LANES), F32)],
        compiler_params=_params(1),
        name="nsa_compress",
    )(p, pos4, w1bd, w2bd)


def _nsa_kernel(q_ref, kv_ref, kvc_ref, gl_ref, gexp_ref, biasn_ref, biasc_ref, ovt_ref, exp_ref,
                o_ref, drop_ref, vaug_ref, sbuf_ref, wbuf_ref, gate_ref, part_ref,
                *, tq):
    qi = pl.program_id(2)
    t0 = pl.multiple_of(qi * tq, tq)
    r = HEADS_PER_GROUP
    rows = r * tq
    hd = HEAD_DIM
    n_lt = tq // LANES

    @pl.when(qi == 0)
    def _():
        ones = jnp.ones((kv_ref.shape[1], hd), BF16)
        vaug_ref[...] = jnp.concatenate(
            [kv_ref[0, :, hd:2 * hd], ones, kv_ref[0, :, 3 * hd:4 * hd], ones], axis=1)

    q = q_ref[0]
    qs = jnp.concatenate([q[:, i * hd:(i + 1) * hd] for i in range(r)], axis=0)

    sig = _sigmoid(_dot(gl_ref[0], gexp_ref[0]))
    for j in range(3):
        gate_ref[j] = jnp.concatenate(
            [jnp.broadcast_to(sig[:, j * r + i:j * r + i + 1], (tq, LANES)) for i in range(r)],
            axis=0)

    kvc = kvc_ref[0, 0]
    s = _dot_nt(qs, kvc[:, hd:]) + biasc_ref[0, 0]
    m = jnp.maximum(jnp.max(s, axis=-1, keepdims=True), MAX_FLOOR)
    p = jnp.exp2(s - m)
    p = p / jnp.maximum(jnp.sum(p, axis=-1, keepdims=True), 1e-30)
    part_ref[...] = gate_ref[0] * _dot(p.astype(BF16), kvc)

    ps = p[0:tq] + p[tq:2 * tq] + p[2 * tq:3 * tq] + p[3 * tq:4 * tq]
    ps_hi = ps.astype(BF16)
    ps_lo = (ps - ps_hi.astype(F32)).astype(BF16)
    ovt = ovt_ref[...]
    n_sel = ovt.shape[0]
    imp_t = _dot_nt(ovt, ps_hi) + _dot_nt(ovt, ps_lo)
    mi = lax.broadcasted_iota(jnp.int32, (n_sel, tq), 0)
    blk = (t0 + lax.broadcasted_iota(jnp.int32, (n_sel, tq), 1)) >> int(math.log2(SEL_BLOCK))
    forced = (mi == 0) | (mi == blk) | (mi == blk - 1)
    score = jnp.where(forced, jnp.inf, jnp.where(mi <= blk, imp_t, -jnp.inf))
    n_grp = n_sel // SUBLANES
    grp = [score[SUBLANES * a:SUBLANES * (a + 1)] for a in range(n_grp)]
    cnt = [jnp.zeros((SUBLANES, tq), F32) for _ in range(n_grp)]
    for mp in range(n_sel):
        other = score[mp:mp + 1, :]
        for a in range(n_grp):
            if SUBLANES * a > mp:
                ahead = other >= grp[a]
            elif SUBLANES * (a + 1) - 1 <= mp:
                ahead = other > grp[a]
            else:
                later = (lax.broadcasted_iota(jnp.int32, (SUBLANES, tq), 0) + SUBLANES * a) > mp
                ahead = (other > grp[a]) | ((other == grp[a]) & later)
            cnt[a] = cnt[a] + jnp.where(ahead, 1.0, 0.0)
    cnt = jnp.concatenate(cnt, axis=0)
    drop_t = jnp.where(cnt < float(min(SEL_TOP_N, n_sel)), 0.0, 1.0)
    drop_t = jnp.concatenate([drop_t, jnp.zeros((LANES - n_sel, tq), F32)], axis=0)
    drop_ref[...] = drop_t.T.astype(BF16)

    def sel_add(j):
        a = _dot(drop_ref[...], exp_ref[j])
        return jnp.broadcast_to(a[None], (r, tq, tq)).reshape(rows, tq)

    def lane_max(sc):
        mx = sc[:, 0:LANES]
        for c in range(1, n_lt):
            mx = jnp.maximum(mx, sc[:, c * LANES:(c + 1) * LANES])
        return mx

    def probs(sc, m_rep):
        return jnp.concatenate([jnp.exp2(sc[:, c * LANES:(c + 1) * LANES] - m_rep)
                                for c in range(n_lt)], axis=1).astype(BF16)

    def row_max(mx):
        return jnp.broadcast_to(jnp.max(mx, axis=-1, keepdims=True), (rows, LANES))

    ri = lax.broadcasted_iota(jnp.int32, (rows, tq), 0) & (tq - 1)
    ci = lax.broadcasted_iota(jnp.int32, (rows, tq), 1)

    def attend(n):
        def branch():
            tile = lambda j: slice(j * tq, (j + 1) * tq)
            mx_s = None
            for j in range(n + 1):
                add = sel_add(j)
                if j == n:
                    add = add + biasn_ref[0, 0]
                elif j == n - 1:
                    add = add + biasn_ref[0, 1]
                sc = _dot_nt(qs, kv_ref[0, tile(j), 0:hd]) + add
                sbuf_ref[:, tile(j)] = sc
                mx_s = lane_max(sc) if mx_s is None else jnp.maximum(mx_s, lane_max(sc))
            win_tiles = [j for j in (n - 2, n - 1, n) if j >= 0]
            mx_w = None
            for w, j in enumerate(win_tiles):
                if j == n:
                    add = biasn_ref[0, 0]
                elif j == n - 1:
                    add = biasn_ref[0, 1]
                else:
                    add = jnp.where(ci > ri, 0.0, NEG)
                sc = _dot_nt(qs, kv_ref[0, tile(j), 2 * hd:3 * hd]) + add
                wbuf_ref[:, tile(w)] = sc
                mx_w = lane_max(sc) if mx_w is None else jnp.maximum(mx_w, lane_max(sc))
            m_sel = row_max(mx_s)
            m_win = row_max(mx_w)
            acc_w = None
            for w, j in enumerate(win_tiles):
                d = _dot(probs(wbuf_ref[:, tile(w)], m_win), vaug_ref[tile(j), LANES:2 * LANES])
                acc_w = d if acc_w is None else acc_w + d
            acc_s = None
            for j in range(n + 1):
                d = _dot(probs(sbuf_ref[:, tile(j)], m_sel), vaug_ref[tile(j), 0:LANES])
                acc_s = d if acc_s is None else acc_s + d
            o = (part_ref[...] + gate_ref[2] * acc_w / pltpu.roll(acc_w, hd, 1)
                 + gate_ref[1] * acc_s / pltpu.roll(acc_s, hd, 1))
            o_ref[0] = jnp.concatenate([o[i * tq:(i + 1) * tq, :hd] for i in range(r)],
                                       axis=1).astype(o_ref.dtype)
        return branch

    lax.switch(qi, [attend(n) for n in range(kv_ref.shape[1] // tq)])


def nsa_attention(p, kvc, gexp, biasn, biasc, ovt, expand, *, tq, q_block, kv_block, gl_block):
    b, s, _ = p.shape
    g = N_KV_GROUPS
    nq = s // tq
    rows = HEADS_PER_GROUP * tq
    assert WINDOW == 2 * tq and tq % LANES == 0
    return pl.pallas_call(
        functools.partial(_nsa_kernel, tq=tq),
        grid=(b, g, nq),
        in_specs=[pl.BlockSpec((1, tq, GROUP_WIDTH), lambda bi, gi, qi: (bi, qi, q_block + gi)),
                  pl.BlockSpec((1, s, GROUP_WIDTH), lambda bi, gi, qi: (bi, 0, kv_block + gi)),
                  pl.BlockSpec((1, 1, N_CMP_PAD, 2 * HEAD_DIM), lambda bi, gi, qi: (bi, gi, 0, 0)),
                  pl.BlockSpec((1, tq, LANES), lambda bi, gi, qi: (bi, qi, gl_block)),
                  pl.BlockSpec((1,) + gexp.shape[1:], lambda bi, gi, qi: (gi, 0, 0)),
                  pl.BlockSpec((1, 2, rows, tq), lambda bi, gi, qi: (gi, 0, 0, 0)),
                  pl.BlockSpec((1, 1, rows, N_CMP_PAD), lambda bi, gi, qi: (gi, qi, 0, 0)),
                  pl.BlockSpec(ovt.shape, lambda bi, gi, qi: (0, 0)),
                  pl.BlockSpec(expand.shape, lambda bi, gi, qi: (0, 0, 0))],
        out_specs=pl.BlockSpec((1, tq, GROUP_WIDTH), lambda bi, gi, qi: (bi, qi, gi)),
        out_shape=jax.ShapeDtypeStruct((b, s, g * GROUP_WIDTH), BF16),
        scratch_shapes=[pltpu.VMEM((tq, LANES), BF16),
                        pltpu.VMEM((s, 2 * LANES), BF16),
                        pltpu.VMEM((rows, s), F32),
                        pltpu.VMEM((rows, 3 * tq), F32),
                        pltpu.VMEM((3, rows, LANES), F32),
                        pltpu.VMEM((rows, LANES), F32)],
        compiler_params=_params(3),
        name="nsa_attention",
    )(p, p, kvc, p, gexp, biasn, biasc, ovt, expand)


def _merge_out_kernel(gc_ref, ga_ref, conv_ref, nsa_ref, w_ref, x_ref, o_ref):
    y = (_sigmoid(gc_ref[...].astype(F32)) * conv_ref[...].astype(F32)
         + _sigmoid(ga_ref[...].astype(F32)) * nsa_ref[...].astype(F32))
    o_ref[...] = x_ref[...] + _dot(y.astype(BF16), w_ref[...])


def merge_out(p2, conv, nsa, w, x, *, tm, gc_block, ga_block):
    m, d = x.shape
    tile = lambda blk: pl.BlockSpec((tm, d), lambda i: (i, blk))
    return pl.pallas_call(
        _merge_out_kernel,
        grid=(m // tm,),
        in_specs=[tile(gc_block), tile(ga_block), tile(0), tile(0),
                  pl.BlockSpec((d, d), lambda i: (0, 0)), tile(0)],
        out_specs=tile(0),
        out_shape=jax.ShapeDtypeStruct((m, d), F32),
        compiler_params=_params(1),
        name="merge_out",
    )(p2, p2, conv, nsa, w, x)


def _xattn_kernel(x_ref, g_ref, wq_ref, kv_ref, wo_ref, o_ref):
    x = x_ref[0]
    d = x.shape[-1]
    hd = d // X_HEADS
    h = _rms_norm(x, g_ref[...]).astype(BF16)
    q = (_dot(h, wq_ref[...]) * (1.0 / math.sqrt(hd))).astype(BF16)
    kv = kv_ref[0]
    outs = []
    for i in range(X_HEADS):
        s = _dot_nt(q[:, i * hd:(i + 1) * hd], kv[:, i * hd:(i + 1) * hd])
        p = jnp.exp(s - jnp.max(s, axis=-1, keepdims=True))
        l = jnp.sum(p, axis=-1, keepdims=True)
        outs.append(_dot(p.astype(BF16), kv[:, d + i * hd:d + (i + 1) * hd]) / l)
    o = jnp.concatenate(outs, axis=1).astype(BF16)
    o_ref[0] = x + _dot(o, wo_ref[...])


def cross_attention_block(x, g, wq, kv, wo, *, tm):
    b, s, d = x.shape
    mlen = kv.shape[1]
    return pl.pallas_call(
        _xattn_kernel,
        grid=(b, s // tm),
        in_specs=[pl.BlockSpec((1, tm, d), lambda bi, si: (bi, si, 0)),
                  pl.BlockSpec((1, d), lambda bi, si: (0, 0)),
                  pl.BlockSpec((d, d), lambda bi, si: (0, 0)),
                  pl.BlockSpec((1, mlen, 2 * d), lambda bi, si: (bi, 0, 0)),
                  pl.BlockSpec((d, d), lambda bi, si: (0, 0))],
        out_specs=pl.BlockSpec((1, tm, d), lambda bi, si: (bi, si, 0)),
        out_shape=jax.ShapeDtypeStruct((b, s, d), F32),
        compiler_params=_params(2),
        name="cross_attention",
    )(x, g, wq, kv, wo)


def _ffn_kernel(x_ref, g_ref, wa_ref, wb_ref, wo_ref, o_ref, xn_ref, acc_ref):
    f = pl.program_id(1)

    @pl.when(f == 0)
    def _():
        xn_ref[...] = _rms_norm(x_ref[...], g_ref[...]).astype(BF16)

    xn = xn_ref[...]
    a = _dot(xn, wa_ref[...])
    bb = _dot(xn, wb_ref[...])
    part = _dot((a * _sigmoid(a) * bb).astype(BF16), wo_ref[...])

    @pl.when(f == 0)
    def _():
        acc_ref[...] = part

    @pl.when(f > 0)
    def _():
        acc_ref[...] = acc_ref[...] + part

    @pl.when(f == pl.num_programs(1) - 1)
    def _():
        o_ref[...] = x_ref[...] + acc_ref[...]


def ffn_block(x, g, w_in, w_out, *, tm, tf):
    m, d = x.shape
    f = w_out.shape[0]
    nf = f // tf
    return pl.pallas_call(
        _ffn_kernel,
        grid=(m // tm, nf),
        in_specs=[pl.BlockSpec((tm, d), lambda i, j: (i, 0)),
                  pl.BlockSpec((1, d), lambda i, j: (0, 0)),
                  pl.BlockSpec((d, tf), lambda i, j: (0, j)),
                  pl.BlockSpec((d, tf), lambda i, j: (0, nf + j)),
                  pl.BlockSpec((tf, d), lambda i, j: (j, 0))],
        out_specs=pl.BlockSpec((tm, d), lambda i, j: (i, 0)),
        out_shape=jax.ShapeDtypeStruct((m, d), F32),
        scratch_shapes=[pltpu.VMEM((tm, d), BF16), pltpu.VMEM((tm, d), F32)],
        compiler_params=_params(2),
        name="ffn_block",
    )(x, g, w_in, w_in, w_out)


def _final_norm_kernel(x_ref, g_ref, o_ref):
    o_ref[...] = _rms_norm(x_ref[...], g_ref[...])


def final_norm(x, g, *, tm):
    m, d = x.shape
    return pl.pallas_call(
        _final_norm_kernel,
        grid=(m // tm,),
        in_specs=[pl.BlockSpec((tm, d), lambda i: (i, 0)), pl.BlockSpec((1, d), lambda i: (0, 0))],
        out_specs=pl.BlockSpec((tm, d), lambda i: (i, 0)),
        out_shape=jax.ShapeDtypeStruct((m, d), F32),
        compiler_params=_params(1),
        name="final_norm",
    )(x, g)


def _t5_bucket(dist):
    n = jnp.maximum(dist, 0)
    max_exact = REL_BUCKETS // 2
    nf = jnp.maximum(n, 1).astype(F32)
    large = max_exact + (jnp.log(nf / max_exact) / math.log(REL_MAX_DIST / max_exact)
                         * (REL_BUCKETS - max_exact)).astype(jnp.int32)
    large = jnp.minimum(large, REL_BUCKETS - 1)
    return jnp.where(n < max_exact, n, large)


def _bias_lookup(rel_bias, dist):
    onehot = jax.nn.one_hot(_t5_bucket(dist), REL_BUCKETS, dtype=F32)
    return jnp.einsum("...b,bh->...h", onehot, rel_bias, precision=lax.Precision.HIGHEST)


def _bias_tables(rel_bias, s, tq):
    g, r = N_KV_GROUPS, HEADS_PER_GROUP
    nq = s // tq
    rel_bias = rel_bias * LOG2E
    far = _bias_lookup(rel_bias, jnp.asarray(s, jnp.int32))
    i = jnp.arange(tq)[:, None]
    j = jnp.arange(tq)[None, :]

    def near(dist):
        bt = (_bias_lookup(rel_bias, dist) - far).transpose(2, 0, 1)
        bt = jnp.where((dist >= 0)[None], bt, NEG)
        return bt.reshape(g, r * tq, tq)

    biasn = jnp.stack([near(i - j), near(i - j + tq)], axis=1)

    t = jnp.arange(s)[:, None]
    c_end = jnp.arange(N_CMP_PAD)[None, :] * CMP_STRIDE + CMP_BLOCK - 1
    bc = _bias_lookup(rel_bias, t - c_end) + jnp.where(t >= c_end, 0.0, NEG)[:, :, None]
    bc = bc.reshape(nq, tq, N_CMP_PAD, g, r).transpose(3, 0, 4, 1, 2)
    biasc = bc.reshape(g, nq, r * tq, N_CMP_PAD)
    return biasn, biasc


def _static_tables(s, tq):
    n_sel = s // SEL_BLOCK
    jj = np.arange(N_CMP_PAD)[None, :] * CMP_STRIDE
    mm0 = np.arange(n_sel)[:, None] * SEL_BLOCK
    n_cmp = (s - CMP_BLOCK) // CMP_STRIDE + 1
    ovt = ((jj < mm0 + SEL_BLOCK) & (jj + CMP_BLOCK > mm0) & (np.arange(N_CMP_PAD)[None, :] < n_cmp))
    expand = np.zeros((LANES, s), np.float32)
    expand[np.arange(s) // SEL_BLOCK, np.arange(s)] = NEG
    expand = expand.reshape(LANES, s // tq, tq).transpose(1, 0, 2)
    gexp = np.zeros((N_KV_GROUPS, LANES, LANES), np.float32)
    for g in range(N_KV_GROUPS):
        for j in range(3):
            for i in range(HEADS_PER_GROUP):
                gexp[g, j * N_HEADS + g * HEADS_PER_GROUP + i, j * HEADS_PER_GROUP + i] = 1.0
    d = np.arange(LANES + 1, s + 1).astype(np.float32)
    big = 16 + (np.log(d / 16) / math.log(REL_MAX_DIST / 16) * 16).astype(np.int32)
    assert np.all(np.minimum(big, REL_BUCKETS - 1) == REL_BUCKETS - 1)
    return (jnp.asarray(ovt.astype(np.float32), BF16), jnp.asarray(expand, BF16), jnp.asarray(gexp, BF16))


def _in_proj_perm(d):
    g, hd = N_KV_GROUPS, HEAD_DIM
    kvw = g * hd
    o_conv, o_q = 0, 2 * d
    o_kc = o_q + N_HEADS * hd
    o_vc, o_ks, o_vs, o_kw, o_vw = (o_kc + i * kvw for i in range(1, 6))
    o_gate = o_kc + 6 * kvw
    o_gc = o_gate + 3 * N_HEADS
    o_ga = o_gc + d
    segs = [(o_conv, 2 * d + N_HEADS * hd)]
    for gi in range(g):
        segs += [(base + gi * hd, hd) for base in (o_ks, o_vs, o_kw, o_vw)]
    segs += [(o_gc, 2 * d), (o_kc, 2 * kvw), (o_gate, 3 * N_HEADS)]
    n_used = sum(n for _, n in segs)
    n_pad = LANES - 3 * N_HEADS
    n_pad += (-(n_used + n_pad)) % (IN_PROJ_N_TILES * LANES)
    return segs, n_pad


def _permute_in_proj(w_in):
    d = w_in.shape[1]
    segs, n_pad = _in_proj_perm(d)
    q_scale = jnp.ones((w_in.shape[2],), F32).at[2 * d:2 * d + N_HEADS * HEAD_DIM].set(LOG2E / math.sqrt(HEAD_DIM))
    w = (w_in * q_scale).astype(BF16)
    parts = [w[:, :, a:a + n] for a, n in segs]
    parts.append(jnp.zeros(w.shape[:2] + (n_pad,), BF16))
    return jnp.concatenate(parts, axis=2)


def kernel(x, mem, norm_mix_g, w_in, conv_dw_w, conv_dw_b, conv_ln_g, conv_ln_b, conv_pw_w, cmp_pos,
           cmp_w1, cmp_w2, w_out, norm_x_g, xq_w, xkv_w, xo_w, norm_ffn_g, ffn_in_w, ffn_out_w,
           rel_bias, final_norm_g):
    b, s, d = x.shape
    depth = w_in.shape[0]
    mlen = mem.shape[1]
    m = b * s
    tq = 256

    w_in_p = _permute_in_proj(w_in)
    n_p = w_in_p.shape[2]
    q_block = (2 * d) // GROUP_WIDTH
    kv_block = q_block + N_KV_GROUPS
    gc_block = (3 * d + N_KV_GROUPS * GROUP_WIDTH) // d
    ga_block = gc_block + 1
    cmp_block = (6 * d) // (2 * N_KV_GROUPS * HEAD_DIM)
    gl_block = (6 * d + 2 * N_KV_GROUPS * HEAD_DIM) // LANES
    tn = n_p // IN_PROJ_N_TILES

    ovt, expand, gexp = _static_tables(s, tq)
    biasn, biasc = _bias_tables(rel_bias, s, tq)

    eye_g = jnp.eye(N_KV_GROUPS, dtype=F32)
    row = lambda v: v.reshape(1, -1)

    xf = x.reshape(m, d)
    for l in range(depth):
        p = norm_matmul(xf, row(norm_mix_g[l]), w_in_p[l], tm=1024, tn=tn)
        p3 = p.reshape(b, s, n_p)
        conv = conv_module(p3, conv_dw_w[l], row(conv_dw_b[l]), row(conv_ln_g[l]), row(conv_ln_b[l]),
                           conv_pw_w[l].astype(BF16), t=256)
        w1bd = jnp.einsum("gh,kldc->klgdhc", eye_g, cmp_w1[l]).reshape(
            2, CMP_BLOCK, N_KV_GROUPS * HEAD_DIM, N_KV_GROUPS * HEAD_DIM).astype(BF16)
        w2bd = jnp.einsum("gh,kdc->kgdhc", eye_g, cmp_w2[l]).reshape(
            2, N_KV_GROUPS * HEAD_DIM, N_KV_GROUPS * HEAD_DIM).astype(BF16)
        pos4 = jnp.tile(cmp_pos[l], (1, 1, N_KV_GROUPS))
        kvc = compress(p3, cmp_block, pos4, w1bd, w2bd)
        nsa = nsa_attention(p3, kvc, gexp, biasn, biasc, ovt, expand, tq=tq,
                            q_block=q_block, kv_block=kv_block, gl_block=gl_block)
        xf = merge_out(p, conv.reshape(m, d), nsa.reshape(m, d), w_out[l].astype(BF16), xf,
                       tm=512, gc_block=gc_block, ga_block=ga_block)
        kvx = matmul(mem.reshape(b * mlen, d), xkv_w[l].astype(BF16), tm=mlen)
        xf = cross_attention_block(xf.reshape(b, s, d), row(norm_x_g[l]), xq_w[l].astype(BF16),
                                   kvx.reshape(b, mlen, 2 * d), xo_w[l].astype(BF16), tm=512).reshape(m, d)
        xf = ffn_block(xf, row(norm_ffn_g[l]), ffn_in_w[l].astype(BF16), ffn_out_w[l].astype(BF16),
                       tm=512, tf=1408)
    return final_norm(xf, row(final_norm_g), tm=1024).reshape(b, s, d)
```

```python
import functools
import math

import numpy as np
import jax
import jax.numpy as jnp
from jax import lax
from jax.experimental import pallas as pl
from jax.experimental.pallas import tpu as pltpu

F32 = jnp.float32
BF16 = jnp.bfloat16

HEAD_DIM = 64
N_KV_GROUPS = 4
HEADS_PER_GROUP = 4
N_HEADS = N_KV_GROUPS * HEADS_PER_GROUP
GROUP_WIDTH = HEADS_PER_GROUP * HEAD_DIM
CMP_BLOCK = 32
CMP_STRIDE = 16
SEL_BLOCK = 64
SEL_TOP_N = 16
WINDOW = 512
CONV_WIDTH = 31
X_HEADS = 4
REL_BUCKETS = 32
REL_MAX_DIST = 128
EPS = 1e-6
NEG = -1e30
MAX_FLOOR = -1e20
LOG2E = math.log2(math.e)

LANES = 128
SUBLANES = 8
CONV_ROWS = 64
CONV_HALO = 32
N_CMP_PAD = 128
VMEM_LIMIT = 56 * 1024 * 1024


def _sigmoid(x):
    return 0.5 * jnp.tanh(0.5 * x) + 0.5


def _dot(a, b):
    return jnp.dot(a, b, preferred_element_type=F32)


def _dot_nt(a, b):
    return lax.dot_general(a, b, (((1,), (1,)), ((), ())), preferred_element_type=F32)


def _rms_norm(x, g):
    return x * lax.rsqrt(jnp.mean(x * x, axis=-1, keepdims=True) + EPS) * g


def _params(n_axes):
    return pltpu.CompilerParams(dimension_semantics=("arbitrary",) * n_axes,
                                vmem_limit_bytes=VMEM_LIMIT)


def _norm_matmul_kernel(x_ref, g_ref, w_ref, o_ref):
    xn = _rms_norm(x_ref[...], g_ref[...]).astype(BF16)
    o_ref[...] = _dot(xn, w_ref[...]).astype(o_ref.dtype)


def norm_matmul(x, g, w, *, tm):
    m, k = x.shape
    n = w.shape[1]
    return pl.pallas_call(
        _norm_matmul_kernel,
        grid=(m // tm,),
        in_specs=[pl.BlockSpec((tm, k), lambda i: (i, 0)),
                  pl.BlockSpec((1, k), lambda i: (0, 0)),
                  pl.BlockSpec((k, n), lambda i: (0, 0), pipeline_mode=pl.Buffered(1))],
        out_specs=pl.BlockSpec((tm, n), lambda i: (i, 0)),
        out_shape=jax.ShapeDtypeStruct((m, n), BF16),
        compiler_params=_params(1),
        name="norm_matmul",
    )(x, g, w)


def _matmul_kernel(x_ref, w_ref, o_ref):
    o_ref[...] = _dot(x_ref[...].astype(BF16), w_ref[...]).astype(o_ref.dtype)


def matmul(x, w, *, tm):
    m, k = x.shape
    n = w.shape[1]
    return pl.pallas_call(
        _matmul_kernel,
        grid=(m // tm,),
        in_specs=[pl.BlockSpec((tm, k), lambda i: (i, 0)),
                  pl.BlockSpec((k, n), lambda i: (0, 0))],
        out_specs=pl.BlockSpec((tm, n), lambda i: (i, 0)),
        out_shape=jax.ShapeDtypeStruct((m, n), BF16),
        compiler_params=_params(1),
        name="matmul",
    )(x, w)


def _conv_kernel(a_ref, gt_ref, dww_ref, dwb_ref, lng_ref, lnb_ref, pw_ref, o_ref, buf_ref, cbuf_ref, *, t):
    s = pl.program_id(1)

    @pl.when(s == 0)
    def _():
        buf_ref[0:CONV_HALO, :] = jnp.zeros((CONV_HALO, buf_ref.shape[1]), F32)

    @pl.when(s > 0)
    def _():
        buf_ref[0:CONV_HALO, :] = buf_ref[t:t + CONV_HALO, :]

    a = a_ref[0].astype(F32)
    gt = gt_ref[0].astype(F32)
    buf_ref[CONV_HALO:CONV_HALO + t, :] = a * _sigmoid(gt)

    base = CONV_HALO - (CONV_WIDTH - 1)

    def conv_rows(ci, carry):
        r0 = pl.multiple_of(ci * CONV_ROWS, CONV_ROWS)
        for c in range(buf_ref.shape[1] // LANES):
            lanes = slice(c * LANES, (c + 1) * LANES)
            acc = dwb_ref[:, lanes]
            for b in range(SUBLANES):
                n = CONV_ROWS + (SUBLANES if b else 0)
                part = None
                for k in range(CONV_WIDTH):
                    if (base + k) % SUBLANES != b:
                        continue
                    term = dww_ref[k:k + 1, lanes] * buf_ref[pl.ds(r0 + (base + k - b), n), lanes]
                    part = term if part is None else part + term
                acc = acc + part[b:b + CONV_ROWS]
            cbuf_ref[pl.ds(r0, CONV_ROWS), lanes] = acc
        return carry

    lax.fori_loop(0, t // CONV_ROWS, conv_rows, 0)
    v = cbuf_ref[...]

    mu = jnp.mean(v, axis=-1, keepdims=True)
    vc = v - mu
    y = vc * lax.rsqrt(jnp.mean(vc * vc, axis=-1, keepdims=True) + EPS)
    y = y * lng_ref[...] + lnb_ref[...]
    y = y * _sigmoid(y)
    o_ref[0] = _dot(y.astype(BF16), pw_ref[...]).astype(o_ref.dtype)


def conv_module(p, dww, dwb, lng, lnb, pw, *, t):
    b, s, _ = p.shape
    c = pw.shape[0]
    vec = lambda: pl.BlockSpec((1, c), lambda bi, si: (0, 0))
    return pl.pallas_call(
        functools.partial(_conv_kernel, t=t),
        grid=(b, s // t),
        in_specs=[pl.BlockSpec((1, t, c), lambda bi, si: (bi, si, 0)),
                  pl.BlockSpec((1, t, c), lambda bi, si: (bi, si, 1)),
                  pl.BlockSpec((CONV_WIDTH, c), lambda bi, si: (0, 0)),
                  vec(), vec(), vec(),
                  pl.BlockSpec((c, c), lambda bi, si: (0, 0))],
        out_specs=pl.BlockSpec((1, t, c), lambda bi, si: (bi, si, 0)),
        out_shape=jax.ShapeDtypeStruct((b, s, c), BF16),
        scratch_shapes=[pltpu.VMEM((CONV_HALO + t, c), F32), pltpu.VMEM((t, c), F32)],
        compiler_params=_params(2),
        name="conv_module",
    )(p, p, dww, dwb, lng, lnb, pw)


def _compress_kernel(x_ref, pos_ref, w1_ref, w2_ref, o_ref, xf_ref):
    half = CMP_BLOCK // 2
    gw = N_KV_GROUPS * HEAD_DIM
    n_chunk = gw // LANES
    for c in range(2 * n_chunk):
        xf_ref[c] = x_ref[0, :, c * LANES:(c + 1) * LANES].astype(F32)
    for kv in range(2):
        acc_a = jnp.zeros((N_CMP_PAD, gw), F32)
        acc_b = jnp.zeros((N_CMP_PAD, gw), F32)
        for l in range(half):
            rows = jnp.concatenate(
                [xf_ref[kv * n_chunk + c, pl.ds(l, N_CMP_PAD, stride=CMP_STRIDE), :] for c in range(n_chunk)],
                axis=1)
            ra = (rows + pos_ref[kv, l:l + 1, :]).astype(BF16)
            rb = (rows + pos_ref[kv, half + l:half + l + 1, :]).astype(BF16)
            acc_a = acc_a + _dot(ra, w1_ref[kv, l])
            acc_b = acc_b + _dot(rb, w1_ref[kv, half + l])
        z = acc_a + pltpu.roll(acc_b, N_CMP_PAD - 1, 0)
        h = z * _sigmoid(z)
        out = _dot(h.astype(BF16), w2_ref[kv]).astype(o_ref.dtype)
        for g in range(N_KV_GROUPS):
            o_ref[0, g, :, (1 - kv) * HEAD_DIM:(2 - kv) * HEAD_DIM] = out[:, g * HEAD_DIM:(g + 1) * HEAD_DIM]


def compress(p, col_block, pos4, w1bd, w2bd):
    b, s, _ = p.shape
    assert s == N_CMP_PAD * CMP_STRIDE
    gw2 = 2 * N_KV_GROUPS * HEAD_DIM
    return pl.pallas_call(
        _compress_kernel,
        grid=(b,),
        in_specs=[pl.BlockSpec((1, s, gw2), lambda bi: (bi, 0, col_block)),
                  pl.BlockSpec(pos4.shape, lambda bi: (0, 0, 0)),
                  pl.BlockSpec(w1bd.shape, lambda bi: (0, 0, 0, 0)),
                  pl.BlockSpec(w2bd.shape, lambda bi: (0, 0, 0))],
        out_specs=pl.BlockSpec((1, N_KV_GROUPS, N_CMP_PAD, 2 * HEAD_DIM), lambda bi: (bi, 0, 0, 0)),
        out_shape=jax.ShapeDtypeStruct((b, N_KV_GROUPS, N_CMP_PAD, 2 * HEAD_DIM), BF16),
        scratch_shapes=[pltpu.VMEM((gw2 // LANES, s, LANES), F32)],
        compiler_params=_params(1),
        name="nsa_compress",
    )(p, pos4, w1bd, w2bd)


def _nsa_kernel(q_ref, kv_ref, kvc_ref, gl_ref, gexp_ref, biasn_ref, biasc_ref, ovt_ref, kmask_ref,
                o_ref, qsa_ref, kaug_ref, vaug_ref, sbuf_ref, wbuf_ref, sig_ref, part_ref,
                *, tq):
    qi = pl.program_id(2)
    t0 = pl.multiple_of(qi * tq, tq)
    r = HEADS_PER_GROUP
    rows = r * tq
    hd = HEAD_DIM
    n_lt = tq // LANES

    @pl.when(qi == 0)
    def _():
        ones = jnp.ones((kv_ref.shape[1], hd), BF16)
        vaug_ref[...] = jnp.concatenate(
            [kv_ref[0, :, hd:2 * hd], ones, kv_ref[0, :, 3 * hd:4 * hd], ones], axis=1)
        kaug_ref[...] = jnp.concatenate([kv_ref[0, :, 0:hd], kmask_ref[...]], axis=1)

    q = q_ref[0]
    qs = jnp.concatenate([q[:, i * hd:(i + 1) * hd] for i in range(r)], axis=0)

    sig_ref[...] = _sigmoid(_dot(gl_ref[0], gexp_ref[0]))

    def gate(j):
        sig = sig_ref[...]
        return jnp.concatenate(
            [jnp.broadcast_to(sig[:, j * r + i:j * r + i + 1], (tq, LANES)) for i in range(r)],
            axis=0)

    kvc = kvc_ref[0, 0]
    s = _dot_nt(qs, kvc[:, hd:]) + biasc_ref[0, 0]
    m = jnp.maximum(jnp.max(s, axis=-1, keepdims=True), MAX_FLOOR)
    p = jnp.exp2(s - m)
    p = p / jnp.maximum(jnp.sum(p, axis=-1, keepdims=True), 1e-30)
    part_ref[...] = _dot(p.astype(BF16), kvc)

    ps = p[0:tq] + p[tq:2 * tq] + p[2 * tq:3 * tq] + p[3 * tq:4 * tq]
    ps_hi = ps.astype(BF16)
    ps_lo = (ps - ps_hi.astype(F32)).astype(BF16)
    ovt = ovt_ref[...]
    n_sel = ovt.shape[0]
    imp_t = _dot_nt(ovt, ps_hi) + _dot_nt(ovt, ps_lo)
    mi = lax.broadcasted_iota(jnp.int32, (n_sel, tq), 0)
    blk = (t0 + lax.broadcasted_iota(jnp.int32, (n_sel, tq), 1)) >> int(math.log2(SEL_BLOCK))
    forced = (mi == 0) | (mi == blk) | (mi == blk - 1)
    score = jnp.where(forced, jnp.inf, jnp.where(mi <= blk, imp_t, -jnp.inf))
    n_grp = n_sel // SUBLANES
    grp = [score[SUBLANES * a:SUBLANES * (a + 1)] for a in range(n_grp)]
    cnt = [jnp.zeros((SUBLANES, tq), F32) for _ in range(n_grp)]
    for mp in range(n_sel):
        other = score[mp:mp + 1, :]
        for a in range(n_grp):
            if SUBLANES * a > mp:
                ahead = other >= grp[a]
            elif SUBLANES * (a + 1) - 1 <= mp:
                ahead = other > grp[a]
            else:
                later = (lax.broadcasted_iota(jnp.int32, (SUBLANES, tq), 0) + SUBLANES * a) > mp
                ahead = (other > grp[a]) | ((other == grp[a]) & later)
            cnt[a] = cnt[a] + jnp.where(ahead, 1.0, 0.0)
    cnt = jnp.concatenate(cnt, axis=0)
    drop_t = jnp.where(cnt < float(min(SEL_TOP_N, n_sel)), 0.0, 1.0)
    drop_t = jnp.concatenate([jnp.zeros((hd, tq), F32), drop_t,
                              jnp.zeros((LANES - hd - n_sel, tq), F32)], axis=0)
    drop = drop_t.T.astype(BF16)[:, hd:]
    qsa_ref[...] = jnp.concatenate(
        [jnp.concatenate([q[:, i * hd:(i + 1) * hd], drop], axis=1) for i in range(r)], axis=0)

    def lane_max(sc):
        mx = sc[:, 0:LANES]
        for c in range(1, n_lt):
            mx = jnp.maximum(mx, sc[:, c * LANES:(c + 1) * LANES])
        return mx

    def probs(sc, m_rep):
        return jnp.concatenate([jnp.exp2(sc[:, c * LANES:(c + 1) * LANES] - m_rep)
                                for c in range(n_lt)], axis=1).astype(BF16)

    def row_max(mx):
        return jnp.broadcast_to(jnp.max(mx, axis=-1, keepdims=True), (rows, LANES))

    ri = lax.broadcasted_iota(jnp.int32, (rows, tq), 0) & (tq - 1)
    ci = lax.broadcasted_iota(jnp.int32, (rows, tq), 1)

    def attend(n):
        def branch():
            tile = lambda j: slice(j * tq, (j + 1) * tq)
            mx_s = None
            for j in range(n + 1):
                sc = _dot_nt(qsa_ref[...], kaug_ref[tile(j), :])
                if j == n:
                    sc = sc + biasn_ref[0, 0]
                elif j == n - 1:
                    sc = sc + biasn_ref[0, 1]
                sbuf_ref[:, tile(j)] = sc
                mx_s = lane_max(sc) if mx_s is None else jnp.maximum(mx_s, lane_max(sc))
            win_tiles = [j for j in (n - 2, n - 1, n) if j >= 0]
            mx_w = None
            for w, j in enumerate(win_tiles):
                if j == n:
                    add = biasn_ref[0, 0]
                elif j == n - 1:
                    add = biasn_ref[0, 1]
                else:
                    add = jnp.where(ci > ri, 0.0, NEG)
                sc = _dot_nt(qsa_ref[:, 0:hd], kv_ref[0, tile(j), 2 * hd:3 * hd]) + add
                wbuf_ref[:, tile(w)] = sc
                mx_w = lane_max(sc) if mx_w is None else jnp.maximum(mx_w, lane_max(sc))
            m_sel = row_max(mx_s)
            m_win = row_max(mx_w)
            acc_w = None
            for w, j in enumerate(win_tiles):
                d = _dot(probs(wbuf_ref[:, tile(w)], m_win), vaug_ref[tile(j), LANES:2 * LANES])
                acc_w = d if acc_w is None else acc_w + d
            acc_s = None
            for j in range(n + 1):
                d = _dot(probs(sbuf_ref[:, tile(j)], m_sel), vaug_ref[tile(j), 0:LANES])
                acc_s = d if acc_s is None else acc_s + d
            o = (gate(0) * part_ref[...] + gate(2) * acc_w / pltpu.roll(acc_w, hd, 1)
                 + gate(1) * acc_s / pltpu.roll(acc_s, hd, 1))
            o_ref[0] = jnp.concatenate([o[i * tq:(i + 1) * tq, :hd] for i in range(r)],
                                       axis=1).astype(o_ref.dtype)
        return branch

    lax.switch(qi, [attend(n) for n in range(kv_ref.shape[1] // tq)])


def nsa_attention(p, kvc, gexp, biasn, biasc, ovt, expand, *, tq, q_block, kv_block, gl_block):
    b, s, _ = p.shape
    g = N_KV_GROUPS
    nq = s // tq
    rows = HEADS_PER_GROUP * tq
    assert WINDOW == 2 * tq and tq % LANES == 0
    return pl.pallas_call(
        functools.partial(_nsa_kernel, tq=tq),
        grid=(b, g, nq),
        in_specs=[pl.BlockSpec((1, tq, GROUP_WIDTH), lambda bi, gi, qi: (bi, qi, q_block + gi)),
                  pl.BlockSpec((1, s, GROUP_WIDTH), lambda bi, gi, qi: (bi, 0, kv_block + gi)),
                  pl.BlockSpec((1, 1, N_CMP_PAD, 2 * HEAD_DIM), lambda bi, gi, qi: (bi, gi, 0, 0)),
                  pl.BlockSpec((1, tq, LANES), lambda bi, gi, qi: (bi, qi, gl_block)),
                  pl.BlockSpec((1,) + gexp.shape[1:], lambda bi, gi, qi: (gi, 0, 0)),
                  pl.BlockSpec((1, 2, rows, tq), lambda bi, gi, qi: (gi, 0, 0, 0)),
                  pl.BlockSpec((1, 1, rows, N_CMP_PAD), lambda bi, gi, qi: (gi, qi, 0, 0)),
                  pl.BlockSpec(ovt.shape, lambda bi, gi, qi: (0, 0)),
                  pl.BlockSpec(expand.shape, lambda bi, gi, qi: (0, 0))],
        out_specs=pl.BlockSpec((1, tq, GROUP_WIDTH), lambda bi, gi, qi: (bi, qi, gi)),
        out_shape=jax.ShapeDtypeStruct((b, s, g * GROUP_WIDTH), BF16),
        scratch_shapes=[pltpu.VMEM((rows, LANES), BF16),
                        pltpu.VMEM((s, LANES), BF16),
                        pltpu.VMEM((s, 2 * LANES), BF16),
                        pltpu.VMEM((rows, s), F32),
                        pltpu.VMEM((rows, 3 * tq), F32),
                        pltpu.VMEM((tq, LANES), F32),
                        pltpu.VMEM((rows, LANES), F32)],
        compiler_params=_params(3),
        name="nsa_attention",
    )(p, p, kvc, p, gexp, biasn, biasc, ovt, expand)


def _merge_out_kernel(gc_ref, ga_ref, conv_ref, nsa_ref, w_ref, x_ref, o_ref):
    y = (_sigmoid(gc_ref[...].astype(F32)) * conv_ref[...].astype(F32)
         + _sigmoid(ga_ref[...].astype(F32)) * nsa_ref[...].astype(F32))
    o_ref[...] = x_ref[...] + _dot(y.astype(BF16), w_ref[...])


def merge_out(p2, conv, nsa, w, x, *, tm, gc_block, ga_block):
    m, d = x.shape
    tile = lambda blk: pl.BlockSpec((tm, d), lambda i: (i, blk))
    return pl.pallas_call(
        _merge_out_kernel,
        grid=(m // tm,),
        in_specs=[tile(gc_block), tile(ga_block), tile(0), tile(0),
                  pl.BlockSpec((d, d), lambda i: (0, 0)), tile(0)],
        out_specs=tile(0),
        out_shape=jax.ShapeDtypeStruct((m, d), F32),
        compiler_params=_params(1),
        name="merge_out",
    )(p2, p2, conv, nsa, w, x)


def _xattn_kernel(x_ref, g_ref, wq_ref, kv_ref, wo_ref, o_ref):
    x = x_ref[0]
    d = x.shape[-1]
    hd = d // X_HEADS
    h = _rms_norm(x, g_ref[...]).astype(BF16)
    q = (_dot(h, wq_ref[...]) * (1.0 / math.sqrt(hd))).astype(BF16)
    kv = kv_ref[0]
    outs = []
    for i in range(X_HEADS):
        s = _dot_nt(q[:, i * hd:(i + 1) * hd], kv[:, i * hd:(i + 1) * hd])
        p = jnp.exp(s - jnp.max(s, axis=-1, keepdims=True))
        l = jnp.sum(p, axis=-1, keepdims=True)
        outs.append(_dot(p.astype(BF16), kv[:, d + i * hd:d + (i + 1) * hd]) / l)
    o = jnp.concatenate(outs, axis=1).astype(BF16)
    o_ref[0] = x + _dot(o, wo_ref[...])


def cross_attention_block(x, g, wq, kv, wo, *, tm):
    b, s, d = x.shape
    mlen = kv.shape[1]
    return pl.pallas_call(
        _xattn_kernel,
        grid=(b, s // tm),
        in_specs=[pl.BlockSpec((1, tm, d), lambda bi, si: (bi, si, 0)),
                  pl.BlockSpec((1, d), lambda bi, si: (0, 0)),
                  pl.BlockSpec((d, d), lambda bi, si: (0, 0)),
                  pl.BlockSpec((1, mlen, 2 * d), lambda bi, si: (bi, 0, 0)),
                  pl.BlockSpec((d, d), lambda bi, si: (0, 0))],
        out_specs=pl.BlockSpec((1, tm, d), lambda bi, si: (bi, si, 0)),
        out_shape=jax.ShapeDtypeStruct((b, s, d), F32),
        compiler_params=_params(2),
        name="cross_attention",
    )(x, g, wq, kv, wo)


def _ffn_kernel(x_ref, g_ref, wa_ref, wb_ref, wo_ref, o_ref):
    x = x_ref[...]
    xn = _rms_norm(x, g_ref[...]).astype(BF16)
    a = _dot(xn, wa_ref[...])
    bb = _dot(xn, wb_ref[...])
    o_ref[...] = x + _dot((a * _sigmoid(a) * bb).astype(BF16), wo_ref[...])


def ffn_block(x, g, w_in, w_out, *, tm):
    m, d = x.shape
    f = w_out.shape[0]
    once = pl.Buffered(1)
    return pl.pallas_call(
        _ffn_kernel,
        grid=(m // tm,),
        in_specs=[pl.BlockSpec((tm, d), lambda i: (i, 0)),
                  pl.BlockSpec((1, d), lambda i: (0, 0)),
                  pl.BlockSpec((d, f), lambda i: (0, 0), pipeline_mode=once),
                  pl.BlockSpec((d, f), lambda i: (0, 1), pipeline_mode=once),
                  pl.BlockSpec((f, d), lambda i: (0, 0), pipeline_mode=once)],
        out_specs=pl.BlockSpec((tm, d), lambda i: (i, 0)),
        out_shape=jax.ShapeDtypeStruct((m, d), F32),
        compiler_params=_params(1),
        name="ffn_block",
    )(x, g, w_in, w_in, w_out)


def _final_norm_kernel(x_ref, g_ref, o_ref):
    o_ref[...] = _rms_norm(x_ref[...], g_ref[...])


def final_norm(x, g, *, tm):
    m, d = x.shape
    return pl.pallas_call(
        _final_norm_kernel,
        grid=(m // tm,),
        in_specs=[pl.BlockSpec((tm, d), lambda i: (i, 0)), pl.BlockSpec((1, d), lambda i: (0, 0))],
        out_specs=pl.BlockSpec((tm, d), lambda i: (i, 0)),
        out_shape=jax.ShapeDtypeStruct((m, d), F32),
        compiler_params=_params(1),
        name="final_norm",
    )(x, g)


def _t5_bucket(dist):
    n = jnp.maximum(dist, 0)
    max_exact = REL_BUCKETS // 2
    nf = jnp.maximum(n, 1).astype(F32)
    large = max_exact + (jnp.log(nf / max_exact) / math.log(REL_MAX_DIST / max_exact)
                         * (REL_BUCKETS - max_exact)).astype(jnp.int32)
    large = jnp.minimum(large, REL_BUCKETS - 1)
    return jnp.where(n < max_exact, n, large)


def _bias_lookup(rel_bias, dist):
    onehot = jax.nn.one_hot(_t5_bucket(dist), REL_BUCKETS, dtype=F32)
    return jnp.einsum("...b,bh->...h", onehot, rel_bias, precision=lax.Precision.HIGHEST)


def _bias_tables(rel_bias, s, tq):
    g, r = N_KV_GROUPS, HEADS_PER_GROUP
    nq = s // tq
    rel_bias = rel_bias * LOG2E
    far = _bias_lookup(rel_bias, jnp.asarray(s, jnp.int32))
    i = jnp.arange(tq)[:, None]
    j = jnp.arange(tq)[None, :]

    def near(dist):
        bt = (_bias_lookup(rel_bias, dist) - far).transpose(2, 0, 1)
        bt = jnp.where((dist >= 0)[None], bt, NEG)
        return bt.reshape(g, r * tq, tq)

    biasn = jnp.stack([near(i - j), near(i - j + tq)], axis=1)

    t = jnp.arange(s)[:, None]
    c_end = jnp.arange(N_CMP_PAD)[None, :] * CMP_STRIDE + CMP_BLOCK - 1
    bc = _bias_lookup(rel_bias, t - c_end) + jnp.where(t >= c_end, 0.0, NEG)[:, :, None]
    bc = bc.reshape(nq, tq, N_CMP_PAD, g, r).transpose(3, 0, 4, 1, 2)
    biasc = bc.reshape(g, nq, r * tq, N_CMP_PAD)
    return biasn, biasc


def _static_tables(s, tq):
    n_sel = s // SEL_BLOCK
    jj = np.arange(N_CMP_PAD)[None, :] * CMP_STRIDE
    mm0 = np.arange(n_sel)[:, None] * SEL_BLOCK
    n_cmp = (s - CMP_BLOCK) // CMP_STRIDE + 1
    ovt = ((jj < mm0 + SEL_BLOCK) & (jj + CMP_BLOCK > mm0) & (np.arange(N_CMP_PAD)[None, :] < n_cmp))
    assert n_sel <= HEAD_DIM
    expand = np.zeros((s, HEAD_DIM), np.float32)
    expand[np.arange(s), np.arange(s) // SEL_BLOCK] = NEG
    gexp = np.zeros((N_KV_GROUPS, LANES, LANES), np.float32)
    for g in range(N_KV_GROUPS):
        for j in range(3):
            for i in range(HEADS_PER_GROUP):
                gexp[g, j * N_HEADS + g * HEADS_PER_GROUP + i, j * HEADS_PER_GROUP + i] = 1.0
    d = np.arange(LANES + 1, s + 1).astype(np.float32)
    big = 16 + (np.log(d / 16) / math.log(REL_MAX_DIST / 16) * 16).astype(np.int32)
    assert np.all(np.minimum(big, REL_BUCKETS - 1) == REL_BUCKETS - 1)
    return (jnp.asarray(ovt.astype(np.float32), BF16), jnp.asarray(expand, BF16), jnp.asarray(gexp, BF16))


def _in_proj_perm(d):
    g, hd = N_KV_GROUPS, HEAD_DIM
    kvw = g * hd
    o_conv, o_q = 0, 2 * d
    o_kc = o_q + N_HEADS * hd
    o_vc, o_ks, o_vs, o_kw, o_vw = (o_kc + i * kvw for i in range(1, 6))
    o_gate = o_kc + 6 * kvw
    o_gc = o_gate + 3 * N_HEADS
    o_ga = o_gc + d
    segs = [(o_conv, 2 * d + N_HEADS * hd)]
    for gi in range(g):
        segs += [(base + gi * hd, hd) for base in (o_ks, o_vs, o_kw, o_vw)]
    segs += [(o_gc, 2 * d), (o_kc, 2 * kvw), (o_gate, 3 * N_HEADS)]
    n_used = sum(n for _, n in segs)
    return segs, (-n_used) % LANES


def _permute_in_proj(w_in):
    d = w_in.shape[1]
    segs, n_pad = _in_proj_perm(d)
    q_scale = jnp.ones((w_in.shape[2],), F32).at[2 * d:2 * d + N_HEADS * HEAD_DIM].set(LOG2E / math.sqrt(HEAD_DIM))
    w = (w_in * q_scale).astype(BF16)
    parts = [w[:, :, a:a + n] for a, n in segs]
    parts.append(jnp.zeros(w.shape[:2] + (n_pad,), BF16))
    return jnp.concatenate(parts, axis=2)


def kernel(x, mem, norm_mix_g, w_in, conv_dw_w, conv_dw_b, conv_ln_g, conv_ln_b, conv_pw_w, cmp_pos,
           cmp_w1, cmp_w2, w_out, norm_x_g, xq_w, xkv_w, xo_w, norm_ffn_g, ffn_in_w, ffn_out_w,
           rel_bias, final_norm_g):
    b, s, d = x.shape
    depth = w_in.shape[0]
    mlen = mem.shape[1]
    m = b * s
    tq = 256

    w_in_p = _permute_in_proj(w_in)
    n_p = w_in_p.shape[2]
    q_block = (2 * d) // GROUP_WIDTH
    kv_block = q_block + N_KV_GROUPS
    gc_block = (3 * d + N_KV_GROUPS * GROUP_WIDTH) // d
    ga_block = gc_block + 1
    cmp_block = (6 * d) // (2 * N_KV_GROUPS * HEAD_DIM)
    gl_block = (6 * d + 2 * N_KV_GROUPS * HEAD_DIM) // LANES

    ovt, expand, gexp = _static_tables(s, tq)
    biasn, biasc = _bias_tables(rel_bias, s, tq)

    eye_g = jnp.eye(N_KV_GROUPS, dtype=F32)
    row = lambda v: v.reshape(1, -1)

    xf = x.reshape(m, d)
    for l in range(depth):
        p = norm_matmul(xf, row(norm_mix_g[l]), w_in_p[l], tm=512)
        p3 = p.reshape(b, s, n_p)
        conv = conv_module(p3, conv_dw_w[l], row(conv_dw_b[l]), row(conv_ln_g[l]), row(conv_ln_b[l]),
                           conv_pw_w[l].astype(BF16), t=256)
        w1bd = jnp.einsum("gh,kldc->klgdhc", eye_g, cmp_w1[l]).reshape(
            2, CMP_BLOCK, N_KV_GROUPS * HEAD_DIM, N_KV_GROUPS * HEAD_DIM).astype(BF16)
        w2bd = jnp.einsum("gh,kdc->kgdhc", eye_g, cmp_w2[l]).reshape(
            2, N_KV_GROUPS * HEAD_DIM, N_KV_GROUPS * HEAD_DIM).astype(BF16)
        pos4 = jnp.tile(cmp_pos[l], (1, 1, N_KV_GROUPS))
        kvc = compress(p3, cmp_block, pos4, w1bd, w2bd)
        nsa = nsa_attention(p3, kvc, gexp, biasn, biasc, ovt, expand, tq=tq,
                            q_block=q_block, kv_block=kv_block, gl_block=gl_block)
        xf = merge_out(p, conv.reshape(m, d), nsa.reshape(m, d), w_out[l].astype(BF16), xf,
                       tm=512, gc_block=gc_block, ga_block=ga_block)
        kvx = matmul(mem.reshape(b * mlen, d), xkv_w[l].astype(BF16), tm=mlen)
        xf = cross_attention_block(xf.reshape(b, s, d), row(norm_x_g[l]), xq_w[l].astype(BF16),
                                   kvx.reshape(b, mlen, 2 * d), xo_w[l].astype(BF16), tm=512).reshape(m, d)
        xf = ffn_block(xf, row(norm_ffn_g[l]), ffn_in_w[l].astype(BF16), ffn_out_w[l].astype(BF16), tm=512)
    return final_norm(xf, row(final_norm_g), tm=1024).reshape(b, s, d)
```

```python
import functools
import math

import numpy as np
import jax
import jax.numpy as jnp
from jax import lax
from jax.experimental import pallas as pl
from jax.experimental.pallas import tpu as pltpu

F32 = jnp.float32
BF16 = jnp.bfloat16

HEAD_DIM = 64
N_KV_GROUPS = 4
HEADS_PER_GROUP = 4
N_HEADS = N_KV_GROUPS * HEADS_PER_GROUP
GROUP_WIDTH = HEADS_PER_GROUP * HEAD_DIM
CMP_BLOCK = 32
CMP_STRIDE = 16
SEL_BLOCK = 64
SEL_TOP_N = 16
WINDOW = 512
CONV_WIDTH = 31
X_HEADS = 4
REL_BUCKETS = 32
REL_MAX_DIST = 128
EPS = 1e-6
NEG = -1e30
MAX_FLOOR = -1e20
LOG2E = math.log2(math.e)

LANES = 128
SUBLANES = 8
CONV_ROWS = 64
CONV_HALO = 32
N_CMP_PAD = 128
VMEM_LIMIT = 56 * 1024 * 1024


def _sigmoid(x):
    return 0.5 * jnp.tanh(0.5 * x) + 0.5


def _dot(a, b):
    return jnp.dot(a, b, preferred_element_type=F32)


def _dot_nt(a, b):
    return lax.dot_general(a, b, (((1,), (1,)), ((), ())), preferred_element_type=F32)


def _rms_norm(x, g):
    return x * lax.rsqrt(jnp.mean(x * x, axis=-1, keepdims=True) + EPS) * g


def _params(n_axes):
    return pltpu.CompilerParams(dimension_semantics=("arbitrary",) * n_axes,
                                vmem_limit_bytes=VMEM_LIMIT)


def _norm_matmul_kernel(x_ref, g_ref, w_ref, o_ref):
    xn = _rms_norm(x_ref[...], g_ref[...]).astype(BF16)
    o_ref[...] = _dot(xn, w_ref[...]).astype(o_ref.dtype)


def norm_matmul(x, g, w, *, tm):
    m, k = x.shape
    n = w.shape[1]
    return pl.pallas_call(
        _norm_matmul_kernel,
        grid=(m // tm,),
        in_specs=[pl.BlockSpec((tm, k), lambda i: (i, 0)),
                  pl.BlockSpec((1, k), lambda i: (0, 0)),
                  pl.BlockSpec((k, n), lambda i: (0, 0), pipeline_mode=pl.Buffered(1))],
        out_specs=pl.BlockSpec((tm, n), lambda i: (i, 0)),
        out_shape=jax.ShapeDtypeStruct((m, n), BF16),
        compiler_params=_params(1),
        name="norm_matmul",
    )(x, g, w)


def _matmul_kernel(x_ref, w_ref, o_ref):
    o_ref[...] = _dot(x_ref[...].astype(BF16), w_ref[...]).astype(o_ref.dtype)


def matmul(x, w, *, tm):
    m, k = x.shape
    n = w.shape[1]
    return pl.pallas_call(
        _matmul_kernel,
        grid=(m // tm,),
        in_specs=[pl.BlockSpec((tm, k), lambda i: (i, 0)),
                  pl.BlockSpec((k, n), lambda i: (0, 0))],
        out_specs=pl.BlockSpec((tm, n), lambda i: (i, 0)),
        out_shape=jax.ShapeDtypeStruct((m, n), BF16),
        compiler_params=_params(1),
        name="matmul",
    )(x, w)


def _conv_kernel(a_ref, gt_ref, dww_ref, dwb_ref, lng_ref, lnb_ref, pw_ref, o_ref, buf_ref, cbuf_ref, *, t):
    s = pl.program_id(1)

    @pl.when(s == 0)
    def _():
        buf_ref[0:CONV_HALO, :] = jnp.zeros((CONV_HALO, buf_ref.shape[1]), F32)

    @pl.when(s > 0)
    def _():
        buf_ref[0:CONV_HALO, :] = buf_ref[t:t + CONV_HALO, :]

    a = a_ref[0].astype(F32)
    gt = gt_ref[0].astype(F32)
    buf_ref[CONV_HALO:CONV_HALO + t, :] = a * _sigmoid(gt)

    base = CONV_HALO - (CONV_WIDTH - 1)

    def conv_rows(ci, carry):
        r0 = pl.multiple_of(ci * CONV_ROWS, CONV_ROWS)
        for c in range(buf_ref.shape[1] // LANES):
            lanes = slice(c * LANES, (c + 1) * LANES)
            acc = dwb_ref[:, lanes]
            for b in range(SUBLANES):
                n = CONV_ROWS + (SUBLANES if b else 0)
                part = None
                for k in range(CONV_WIDTH):
                    if (base + k) % SUBLANES != b:
                        continue
                    term = dww_ref[k:k + 1, lanes] * buf_ref[pl.ds(r0 + (base + k - b), n), lanes]
                    part = term if part is None else part + term
                acc = acc + part[b:b + CONV_ROWS]
            cbuf_ref[pl.ds(r0, CONV_ROWS), lanes] = acc
        return carry

    lax.fori_loop(0, t // CONV_ROWS, conv_rows, 0)
    v = cbuf_ref[...]

    mu = jnp.mean(v, axis=-1, keepdims=True)
    vc = v - mu
    y = vc * lax.rsqrt(jnp.mean(vc * vc, axis=-1, keepdims=True) + EPS)
    y = y * lng_ref[...] + lnb_ref[...]
    y = y * _sigmoid(y)
    o_ref[0] = _dot(y.astype(BF16), pw_ref[...]).astype(o_ref.dtype)


def conv_module(p, dww, dwb, lng, lnb, pw, *, t):
    b, s, _ = p.shape
    c = pw.shape[0]
    vec = lambda: pl.BlockSpec((1, c), lambda bi, si: (0, 0))
    return pl.pallas_call(
        functools.partial(_conv_kernel, t=t),
        grid=(b, s // t),
        in_specs=[pl.BlockSpec((1, t, c), lambda bi, si: (bi, si, 0)),
                  pl.BlockSpec((1, t, c), lambda bi, si: (bi, si, 1)),
                  pl.BlockSpec((CONV_WIDTH, c), lambda bi, si: (0, 0)),
                  vec(), vec(), vec(),
                  pl.BlockSpec((c, c), lambda bi, si: (0, 0))],
        out_specs=pl.BlockSpec((1, t, c), lambda bi, si: (bi, si, 0)),
        out_shape=jax.ShapeDtypeStruct((b, s, c), BF16),
        scratch_shapes=[pltpu.VMEM((CONV_HALO + t, c), F32), pltpu.VMEM((t, c), F32)],
        compiler_params=_params(2),
        name="conv_module",
    )(p, p, dww, dwb, lng, lnb, pw)


def _compress_kernel(x_ref, pos_ref, w1_ref, w2_ref, o_ref, xf_ref):
    half = CMP_BLOCK // 2
    gw = N_KV_GROUPS * HEAD_DIM
    n_chunk = gw // LANES
    for c in range(2 * n_chunk):
        xf_ref[c] = x_ref[0, :, c * LANES:(c + 1) * LANES].astype(F32)
    for kv in range(2):
        acc_a = jnp.zeros((N_CMP_PAD, gw), F32)
        acc_b = jnp.zeros((N_CMP_PAD, gw), F32)
        for l in range(half):
            rows = jnp.concatenate(
                [xf_ref[kv * n_chunk + c, pl.ds(l, N_CMP_PAD, stride=CMP_STRIDE), :] for c in range(n_chunk)],
                axis=1)
            ra = (rows + pos_ref[kv, l:l + 1, :]).astype(BF16)
            rb = (rows + pos_ref[kv, half + l:half + l + 1, :]).astype(BF16)
            acc_a = acc_a + _dot(ra, w1_ref[kv, l])
            acc_b = acc_b + _dot(rb, w1_ref[kv, half + l])
        z = acc_a + pltpu.roll(acc_b, N_CMP_PAD - 1, 0)
        h = z * _sigmoid(z)
        out = _dot(h.astype(BF16), w2_ref[kv]).astype(o_ref.dtype)
        for g in range(N_KV_GROUPS):
            o_ref[0, g, :, (1 - kv) * HEAD_DIM:(2 - kv) * HEAD_DIM] = out[:, g * HEAD_DIM:(g + 1) * HEAD_DIM]


def compress(p, col_block, pos4, w1bd, w2bd):
    b, s, _ = p.shape
    assert s == N_CMP_PAD * CMP_STRIDE
    gw2 = 2 * N_KV_GROUPS * HEAD_DIM
    return pl.pallas_call(
        _compress_kernel,
        grid=(b,),
        in_specs=[pl.BlockSpec((1, s, gw2), lambda bi: (bi, 0, col_block)),
                  pl.BlockSpec(pos4.shape, lambda bi: (0, 0, 0)),
                  pl.BlockSpec(w1bd.shape, lambda bi: (0, 0, 0, 0)),
                  pl.BlockSpec(w2bd.shape, lambda bi: (0, 0, 0))],
        out_specs=pl.BlockSpec((1, N_KV_GROUPS, N_CMP_PAD, 2 * HEAD_DIM), lambda bi: (bi, 0, 0, 0)),
        out_shape=jax.ShapeDtypeStruct((b, N_KV_GROUPS, N_CMP_PAD, 2 * HEAD_DIM), BF16),
        scratch_shapes=[pltpu.VMEM((gw2 // LANES, s, LANES), F32)],
        compiler_params=_params(1),
        name="nsa_compress",
    )(p, pos4, w1bd, w2bd)


def _nsa_kernel(q_ref, kv_ref, kvc_ref, gl_ref, gexp_ref, biasn_ref, biasc_ref, ovt_ref, kmask_ref,
                o_ref, qsa_ref, kaug_ref, vaug_ref, sbuf_ref, wbuf_ref, sig_ref, part_ref,
                *, tq):
    qi = pl.program_id(2)
    t0 = pl.multiple_of(qi * tq, tq)
    r = HEADS_PER_GROUP
    rows = r * tq
    hd = HEAD_DIM
    n_lt = tq // LANES

    @pl.when(qi == 0)
    def _():
        ones = jnp.ones((kv_ref.shape[1], hd), BF16)
        vaug_ref[...] = jnp.concatenate(
            [kv_ref[0, :, hd:2 * hd], ones, kv_ref[0, :, 3 * hd:4 * hd], ones], axis=1)
        kaug_ref[...] = jnp.concatenate([kv_ref[0, :, 0:hd], kmask_ref[...]], axis=1)

    q = q_ref[0]
    qs = jnp.concatenate([q[:, i * hd:(i + 1) * hd] for i in range(r)], axis=0)

    sig_ref[...] = _sigmoid(_dot(gl_ref[0], gexp_ref[0]))

    def gate(j):
        sig = sig_ref[...]
        return jnp.concatenate(
            [jnp.broadcast_to(sig[:, j * r + i:j * r + i + 1], (tq, LANES)) for i in range(r)],
            axis=0)

    kvc = kvc_ref[0, 0]
    s = _dot_nt(qs, kvc[:, hd:]) + biasc_ref[0, :, 0].reshape(rows, N_CMP_PAD)
    m = jnp.maximum(jnp.max(s, axis=-1, keepdims=True), MAX_FLOOR)
    p = jnp.exp2(s - m)
    p = p / jnp.maximum(jnp.sum(p, axis=-1, keepdims=True), 1e-30)
    part_ref[...] = _dot(p.astype(BF16), kvc)

    ps = p[0:tq] + p[tq:2 * tq] + p[2 * tq:3 * tq] + p[3 * tq:4 * tq]
    ps_hi = ps.astype(BF16)
    ps_lo = (ps - ps_hi.astype(F32)).astype(BF16)
    ovt = ovt_ref[...]
    n_sel = ovt.shape[0]
    imp_t = _dot_nt(ovt, ps_hi) + _dot_nt(ovt, ps_lo)
    mi = lax.broadcasted_iota(jnp.int32, (n_sel, tq), 0)
    blk = (t0 + lax.broadcasted_iota(jnp.int32, (n_sel, tq), 1)) >> int(math.log2(SEL_BLOCK))
    forced = (mi == 0) | (mi == blk) | (mi == blk - 1)
    score = jnp.where(forced, jnp.inf, jnp.where(mi <= blk, imp_t, -jnp.inf))
    n_grp = n_sel // SUBLANES
    grp = [score[SUBLANES * a:SUBLANES * (a + 1)] for a in range(n_grp)]
    cnt = [jnp.zeros((SUBLANES, tq), F32) for _ in range(n_grp)]
    for mp in range(n_sel):
        other = score[mp:mp + 1, :]
        for a in range(n_grp):
            if SUBLANES * a > mp:
                ahead = other >= grp[a]
            elif SUBLANES * (a + 1) - 1 <= mp:
                ahead = other > grp[a]
            else:
                later = (lax.broadcasted_iota(jnp.int32, (SUBLANES, tq), 0) + SUBLANES * a) > mp
                ahead = (other > grp[a]) | ((other == grp[a]) & later)
            cnt[a] = cnt[a] + jnp.where(ahead, 1.0, 0.0)
    cnt = jnp.concatenate(cnt, axis=0)
    drop_t = jnp.where(cnt < float(min(SEL_TOP_N, n_sel)), 0.0, 1.0)
    drop_t = jnp.concatenate([jnp.zeros((hd, tq), F32), drop_t,
                              jnp.zeros((LANES - hd - n_sel, tq), F32)], axis=0)
    drop = drop_t.T.astype(BF16)[:, hd:]
    qsa_ref[...] = jnp.concatenate(
        [jnp.concatenate([q[:, i * hd:(i + 1) * hd], drop], axis=1) for i in range(r)], axis=0)

    def lane_max(sc):
        mx = sc[:, 0:LANES]
        for c in range(1, n_lt):
            mx = jnp.maximum(mx, sc[:, c * LANES:(c + 1) * LANES])
        return mx

    def probs(sc, m_rep):
        return jnp.concatenate([jnp.exp2(sc[:, c * LANES:(c + 1) * LANES] - m_rep)
                                for c in range(n_lt)], axis=1).astype(BF16)

    def row_max(mx):
        return jnp.broadcast_to(jnp.max(mx, axis=-1, keepdims=True), (rows, LANES))

    ri = lax.broadcasted_iota(jnp.int32, (rows, tq), 0) & (tq - 1)
    ci = lax.broadcasted_iota(jnp.int32, (rows, tq), 1)

    def attend(n):
        def branch():
            tile = lambda j: slice(j * tq, (j + 1) * tq)
            mx_s = None
            for j in range(n + 1):
                sc = _dot_nt(qsa_ref[...], kaug_ref[tile(j), :])
                if j == n:
                    sc = sc + biasn_ref[0, 0]
                elif j == n - 1:
                    sc = sc + biasn_ref[0, 1]
                sbuf_ref[:, tile(j)] = sc
                mx_s = lane_max(sc) if mx_s is None else jnp.maximum(mx_s, lane_max(sc))
            win_tiles = [j for j in (n - 2, n - 1, n) if j >= 0]
            mx_w = None
            for w, j in enumerate(win_tiles):
                if j == n:
                    add = biasn_ref[0, 0]
                elif j == n - 1:
                    add = biasn_ref[0, 1]
                else:
                    add = jnp.where(ci > ri, 0.0, NEG)
                sc = _dot_nt(qsa_ref[:, 0:hd], kv_ref[0, tile(j), 2 * hd:3 * hd]) + add
                wbuf_ref[:, tile(w)] = sc
                mx_w = lane_max(sc) if mx_w is None else jnp.maximum(mx_w, lane_max(sc))
            m_sel = row_max(mx_s)
            m_win = row_max(mx_w)
            acc_w = None
            for w, j in enumerate(win_tiles):
                d = _dot(probs(wbuf_ref[:, tile(w)], m_win), vaug_ref[tile(j), LANES:2 * LANES])
                acc_w = d if acc_w is None else acc_w + d
            acc_s = None
            for j in range(n + 1):
                d = _dot(probs(sbuf_ref[:, tile(j)], m_sel), vaug_ref[tile(j), 0:LANES])
                acc_s = d if acc_s is None else acc_s + d
            o = (gate(0) * part_ref[...] + gate(2) * acc_w / pltpu.roll(acc_w, hd, 1)
                 + gate(1) * acc_s / pltpu.roll(acc_s, hd, 1))
            o_ref[0] = jnp.concatenate([o[i * tq:(i + 1) * tq, :hd] for i in range(r)],
                                       axis=1).astype(o_ref.dtype)
        return branch

    lax.switch(qi, [attend(n) for n in range(kv_ref.shape[1] // tq)])


def nsa_attention(p, kvc, gexp, biasn, biasc, ovt, expand, *, tq, q_block, kv_block, gl_block):
    b, s, _ = p.shape
    g = N_KV_GROUPS
    nq = s // tq
    rows = HEADS_PER_GROUP * tq
    assert WINDOW == 2 * tq and tq % LANES == 0
    return pl.pallas_call(
        functools.partial(_nsa_kernel, tq=tq),
        grid=(b, g, nq),
        in_specs=[pl.BlockSpec((1, tq, GROUP_WIDTH), lambda bi, gi, qi: (bi, qi, q_block + gi)),
                  pl.BlockSpec((1, s, GROUP_WIDTH), lambda bi, gi, qi: (bi, 0, kv_block + gi)),
                  pl.BlockSpec((1, 1, N_CMP_PAD, 2 * HEAD_DIM), lambda bi, gi, qi: (bi, gi, 0, 0)),
                  pl.BlockSpec((1, tq, LANES), lambda bi, gi, qi: (bi, qi, gl_block)),
                  pl.BlockSpec((1,) + gexp.shape[1:], lambda bi, gi, qi: (gi, 0, 0)),
                  pl.BlockSpec((1, 2, rows, tq), lambda bi, gi, qi: (gi, 0, 0, 0)),
                  pl.BlockSpec((1, HEADS_PER_GROUP, 1, tq, N_CMP_PAD), lambda bi, gi, qi: (gi, 0, qi, 0, 0)),
                  pl.BlockSpec(ovt.shape, lambda bi, gi, qi: (0, 0)),
                  pl.BlockSpec(expand.shape, lambda bi, gi, qi: (0, 0))],
        out_specs=pl.BlockSpec((1, tq, GROUP_WIDTH), lambda bi, gi, qi: (bi, qi, gi)),
        out_shape=jax.ShapeDtypeStruct((b, s, g * GROUP_WIDTH), BF16),
        scratch_shapes=[pltpu.VMEM((rows, LANES), BF16),
                        pltpu.VMEM((s, LANES), BF16),
                        pltpu.VMEM((s, 2 * LANES), BF16),
                        pltpu.VMEM((rows, s), F32),
                        pltpu.VMEM((rows, 3 * tq), F32),
                        pltpu.VMEM((tq, LANES), F32),
                        pltpu.VMEM((rows, LANES), F32)],
        compiler_params=_params(3),
        name="nsa_attention",
    )(p, p, kvc, p, gexp, biasn, biasc, ovt, expand)


def _merge_out_kernel(gc_ref, ga_ref, conv_ref, nsa_ref, w_ref, x_ref, o_ref):
    y = (_sigmoid(gc_ref[...].astype(F32)) * conv_ref[...].astype(F32)
         + _sigmoid(ga_ref[...].astype(F32)) * nsa_ref[...].astype(F32))
    o_ref[...] = x_ref[...] + _dot(y.astype(BF16), w_ref[...])


def merge_out(p2, conv, nsa, w, x, *, tm, gc_block, ga_block):
    m, d = x.shape
    tile = lambda blk: pl.BlockSpec((tm, d), lambda i: (i, blk))
    return pl.pallas_call(
        _merge_out_kernel,
        grid=(m // tm,),
        in_specs=[tile(gc_block), tile(ga_block), tile(0), tile(0),
                  pl.BlockSpec((d, d), lambda i: (0, 0)), tile(0)],
        out_specs=tile(0),
        out_shape=jax.ShapeDtypeStruct((m, d), F32),
        compiler_params=_params(1),
        name="merge_out",
    )(p2, p2, conv, nsa, w, x)


def _xattn_kernel(x_ref, g_ref, wq_ref, kv_ref, wo_ref, o_ref):
    x = x_ref[0]
    d = x.shape[-1]
    hd = d // X_HEADS
    h = _rms_norm(x, g_ref[...]).astype(BF16)
    q = (_dot(h, wq_ref[...]) * (1.0 / math.sqrt(hd))).astype(BF16)
    kv = kv_ref[0]
    outs = []
    for i in range(X_HEADS):
        s = _dot_nt(q[:, i * hd:(i + 1) * hd], kv[:, i * hd:(i + 1) * hd])
        p = jnp.exp(s - jnp.max(s, axis=-1, keepdims=True))
        l = jnp.sum(p, axis=-1, keepdims=True)
        outs.append(_dot(p.astype(BF16), kv[:, d + i * hd:d + (i + 1) * hd]) / l)
    o = jnp.concatenate(outs, axis=1).astype(BF16)
    o_ref[0] = x + _dot(o, wo_ref[...])


def cross_attention_block(x, g, wq, kv, wo, *, tm):
    b, s, d = x.shape
    mlen = kv.shape[1]
    return pl.pallas_call(
        _xattn_kernel,
        grid=(b, s // tm),
        in_specs=[pl.BlockSpec((1, tm, d), lambda bi, si: (bi, si, 0)),
                  pl.BlockSpec((1, d), lambda bi, si: (0, 0)),
                  pl.BlockSpec((d, d), lambda bi, si: (0, 0)),
                  pl.BlockSpec((1, mlen, 2 * d), lambda bi, si: (bi, 0, 0)),
                  pl.BlockSpec((d, d), lambda bi, si: (0, 0))],
        out_specs=pl.BlockSpec((1, tm, d), lambda bi, si: (bi, si, 0)),
        out_shape=jax.ShapeDtypeStruct((b, s, d), F32),
        compiler_params=_params(2),
        name="cross_attention",
    )(x, g, wq, kv, wo)


def _ffn_kernel(x_ref, g_ref, wa_ref, wb_ref, wo_ref, gout_ref, o_ref, *, norm_out):
    x = x_ref[...]
    xn = _rms_norm(x, g_ref[...]).astype(BF16)
    a = _dot(xn, wa_ref[...])
    bb = _dot(xn, wb_ref[...])
    y = x + _dot((a * _sigmoid(a) * bb).astype(BF16), wo_ref[...])
    o_ref[...] = _rms_norm(y, gout_ref[...]) if norm_out else y


def ffn_block(x, g, w_in, w_out, g_out, *, tm, norm_out):
    m, d = x.shape
    f = w_out.shape[0]
    once = pl.Buffered(1)
    return pl.pallas_call(
        functools.partial(_ffn_kernel, norm_out=norm_out),
        grid=(m // tm,),
        in_specs=[pl.BlockSpec((tm, d), lambda i: (i, 0)),
                  pl.BlockSpec((1, d), lambda i: (0, 0)),
                  pl.BlockSpec((d, f), lambda i: (0, 0), pipeline_mode=once),
                  pl.BlockSpec((d, f), lambda i: (0, 1), pipeline_mode=once),
                  pl.BlockSpec((f, d), lambda i: (0, 0), pipeline_mode=once),
                  pl.BlockSpec((1, d), lambda i: (0, 0))],
        out_specs=pl.BlockSpec((tm, d), lambda i: (i, 0)),
        out_shape=jax.ShapeDtypeStruct((m, d), F32),
        compiler_params=_params(1),
        name="ffn_block",
    )(x, g, w_in, w_in, w_out, g_out)


def _t5_bucket(dist):
    n = jnp.maximum(dist, 0)
    max_exact = REL_BUCKETS // 2
    nf = jnp.maximum(n, 1).astype(F32)
    large = max_exact + (jnp.log(nf / max_exact) / math.log(REL_MAX_DIST / max_exact)
                         * (REL_BUCKETS - max_exact)).astype(jnp.int32)
    large = jnp.minimum(large, REL_BUCKETS - 1)
    return jnp.where(n < max_exact, n, large)


def _bias_lookup(rel_bias, dist):
    onehot = jax.nn.one_hot(_t5_bucket(dist), REL_BUCKETS, dtype=F32)
    return jnp.einsum("bh,...b->h...", rel_bias, onehot, precision=lax.Precision.HIGHEST)


def _bias_tables(rel_bias, s, tq):
    g, r = N_KV_GROUPS, HEADS_PER_GROUP
    nq = s // tq
    rel_bias = rel_bias * LOG2E
    far = _bias_lookup(rel_bias, jnp.asarray(s, jnp.int32))
    i = jnp.arange(tq)[:, None]
    j = jnp.arange(tq)[None, :]

    def near(dist):
        bt = _bias_lookup(rel_bias, dist) - far[:, None, None]
        bt = jnp.where((dist >= 0)[None], bt, NEG)
        return bt.reshape(g, r * tq, tq)

    biasn = jnp.stack([near(i - j), near(i - j + tq)], axis=1)

    t = jnp.arange(s)[:, None]
    c_end = jnp.arange(N_CMP_PAD)[None, :] * CMP_STRIDE + CMP_BLOCK - 1
    bc = _bias_lookup(rel_bias, t - c_end) + jnp.where(t >= c_end, 0.0, NEG)[None]
    biasc = bc.reshape(g, r, nq, tq, N_CMP_PAD)
    return biasn, biasc


def _static_tables(s, tq):
    n_sel = s // SEL_BLOCK
    jj = np.arange(N_CMP_PAD)[None, :] * CMP_STRIDE
    mm0 = np.arange(n_sel)[:, None] * SEL_BLOCK
    n_cmp = (s - CMP_BLOCK) // CMP_STRIDE + 1
    ovt = ((jj < mm0 + SEL_BLOCK) & (jj + CMP_BLOCK > mm0) & (np.arange(N_CMP_PAD)[None, :] < n_cmp))
    assert n_sel <= HEAD_DIM
    expand = np.zeros((s, HEAD_DIM), np.float32)
    expand[np.arange(s), np.arange(s) // SEL_BLOCK] = NEG
    gexp = np.zeros((N_KV_GROUPS, LANES, LANES), np.float32)
    for g in range(N_KV_GROUPS):
        for j in range(3):
            for i in range(HEADS_PER_GROUP):
                gexp[g, j * N_HEADS + g * HEADS_PER_GROUP + i, j * HEADS_PER_GROUP + i] = 1.0
    d = np.arange(LANES + 1, s + 1).astype(np.float32)
    big = 16 + (np.log(d / 16) / math.log(REL_MAX_DIST / 16) * 16).astype(np.int32)
    assert np.all(np.minimum(big, REL_BUCKETS - 1) == REL_BUCKETS - 1)
    return (jnp.asarray(ovt.astype(np.float32), BF16), jnp.asarray(expand, BF16), jnp.asarray(gexp, BF16))


def _in_proj_perm(d):
    g, hd = N_KV_GROUPS, HEAD_DIM
    kvw = g * hd
    o_conv, o_q = 0, 2 * d
    o_kc = o_q + N_HEADS * hd
    o_vc, o_ks, o_vs, o_kw, o_vw = (o_kc + i * kvw for i in range(1, 6))
    o_gate = o_kc + 6 * kvw
    o_gc = o_gate + 3 * N_HEADS
    o_ga = o_gc + d
    segs = [(o_conv, 2 * d, None), (o_q, N_HEADS * hd, LOG2E / math.sqrt(hd))]
    for gi in range(g):
        segs += [(base + gi * hd, hd, None) for base in (o_ks, o_vs, o_kw, o_vw)]
    segs += [(o_gc, 2 * d, None), (o_kc, 2 * kvw, None), (o_gate, 3 * N_HEADS, None)]
    n_used = sum(n for _, n, _ in segs)
    return segs, (-n_used) % LANES


def _permute_in_proj(w_in):
    segs, n_pad = _in_proj_perm(w_in.shape[1])
    w = w_in.astype(BF16)
    parts = [w[:, :, a:a + n] if sc is None else (w_in[:, :, a:a + n] * sc).astype(BF16)
             for a, n, sc in segs]
    parts.append(jnp.zeros(w.shape[:2] + (n_pad,), BF16))
    return jnp.concatenate(parts, axis=2)


def kernel(x, mem, norm_mix_g, w_in, conv_dw_w, conv_dw_b, conv_ln_g, conv_ln_b, conv_pw_w, cmp_pos,
           cmp_w1, cmp_w2, w_out, norm_x_g, xq_w, xkv_w, xo_w, norm_ffn_g, ffn_in_w, ffn_out_w,
           rel_bias, final_norm_g):
    b, s, d = x.shape
    depth = w_in.shape[0]
    mlen = mem.shape[1]
    m = b * s
    tq = 256

    w_in_p = _permute_in_proj(w_in)
    n_p = w_in_p.shape[2]
    q_block = (2 * d) // GROUP_WIDTH
    kv_block = q_block + N_KV_GROUPS
    gc_block = (3 * d + N_KV_GROUPS * GROUP_WIDTH) // d
    ga_block = gc_block + 1
    cmp_block = (6 * d) // (2 * N_KV_GROUPS * HEAD_DIM)
    gl_block = (6 * d + 2 * N_KV_GROUPS * HEAD_DIM) // LANES

    ovt, expand, gexp = _static_tables(s, tq)
    biasn, biasc = _bias_tables(rel_bias, s, tq)

    eye_g = jnp.eye(N_KV_GROUPS, dtype=F32)
    row = lambda v: v.reshape(1, -1)

    xf = x.reshape(m, d)
    for l in range(depth):
        p = norm_matmul(xf, row(norm_mix_g[l]), w_in_p[l], tm=512)
        p3 = p.reshape(b, s, n_p)
        conv = conv_module(p3, conv_dw_w[l], row(conv_dw_b[l]), row(conv_ln_g[l]), row(conv_ln_b[l]),
                           conv_pw_w[l].astype(BF16), t=512)
        w1bd = jnp.einsum("gh,kldc->klgdhc", eye_g, cmp_w1[l]).reshape(
            2, CMP_BLOCK, N_KV_GROUPS * HEAD_DIM, N_KV_GROUPS * HEAD_DIM).astype(BF16)
        w2bd = jnp.einsum("gh,kdc->kgdhc", eye_g, cmp_w2[l]).reshape(
            2, N_KV_GROUPS * HEAD_DIM, N_KV_GROUPS * HEAD_DIM).astype(BF16)
        pos4 = jnp.tile(cmp_pos[l], (1, 1, N_KV_GROUPS))
        kvc = compress(p3, cmp_block, pos4, w1bd, w2bd)
        nsa = nsa_attention(p3, kvc, gexp, biasn, biasc, ovt, expand, tq=tq,
                            q_block=q_block, kv_block=kv_block, gl_block=gl_block)
        xf = merge_out(p, conv.reshape(m, d), nsa.reshape(m, d), w_out[l].astype(BF16), xf,
                       tm=512, gc_block=gc_block, ga_block=ga_block)
        kvx = matmul(mem.reshape(b * mlen, d), xkv_w[l].astype(BF16), tm=mlen)
        xf = cross_attention_block(xf.reshape(b, s, d), row(norm_x_g[l]), xq_w[l].astype(BF16),
                                   kvx.reshape(b, mlen, 2 * d), xo_w[l].astype(BF16), tm=512).reshape(m, d)
        xf = ffn_block(xf, row(norm_ffn_g[l]), ffn_in_w[l].astype(BF16), ffn_out_w[l].astype(BF16),
                       row(final_norm_g), tm=512, norm_out=(l == depth - 1))
    return xf.reshape(b, s, d)
```

```python
import functools
import math

import numpy as np
import jax
import jax.numpy as jnp
from jax import lax
from jax.experimental import pallas as pl
from jax.experimental.pallas import tpu as pltpu

F32 = jnp.float32
BF16 = jnp.bfloat16

HEAD_DIM = 64
N_KV_GROUPS = 4
HEADS_PER_GROUP = 4
N_HEADS = N_KV_GROUPS * HEADS_PER_GROUP
GROUP_WIDTH = HEADS_PER_GROUP * HEAD_DIM
CMP_BLOCK = 32
CMP_STRIDE = 16
SEL_BLOCK = 64
SEL_TOP_N = 16
WINDOW = 512
CONV_WIDTH = 31
X_HEADS = 4
REL_BUCKETS = 32
REL_MAX_DIST = 128
EPS = 1e-6
NEG = -1e30
MAX_FLOOR = -1e20
LOG2E = math.log2(math.e)

LANES = 128
SUBLANES = 8
CONV_ROWS = 64
CONV_HALO = 32
N_CMP_PAD = 128
VMEM_LIMIT = 56 * 1024 * 1024


def _sigmoid(x):
    return 0.5 * jnp.tanh(0.5 * x) + 0.5


def _dot(a, b):
    return jnp.dot(a, b, preferred_element_type=F32)


def _dot_nt(a, b):
    return lax.dot_general(a, b, (((1,), (1,)), ((), ())), preferred_element_type=F32)


def _rms_norm(x, g):
    return x * lax.rsqrt(jnp.mean(x * x, axis=-1, keepdims=True) + EPS) * g


def _params(n_axes):
    return pltpu.CompilerParams(dimension_semantics=("arbitrary",) * n_axes,
                                vmem_limit_bytes=VMEM_LIMIT)


def _norm_matmul_kernel(x_ref, g_ref, wt_ref, o_ref):
    xn = _rms_norm(x_ref[...], g_ref[...]).astype(BF16)
    o_ref[...] = _dot_nt(xn, wt_ref[...]).astype(o_ref.dtype)


def norm_matmul(x, g, wt, layer, *, tm):
    m, k = x.shape
    n = wt.shape[1]
    return pl.pallas_call(
        _norm_matmul_kernel,
        grid=(m // tm,),
        in_specs=[pl.BlockSpec((tm, k), lambda i: (i, 0)),
                  pl.BlockSpec((1, k), lambda i: (0, 0)),
                  pl.BlockSpec((None, n, k), lambda i: (layer, 0, 0), pipeline_mode=pl.Buffered(1))],
        out_specs=pl.BlockSpec((tm, n), lambda i: (i, 0)),
        out_shape=jax.ShapeDtypeStruct((m, n), BF16),
        compiler_params=_params(1),
        name="norm_matmul",
    )(x, g, wt)


def _matmul_kernel(x_ref, w_ref, o_ref):
    o_ref[...] = _dot(x_ref[...].astype(BF16), w_ref[...]).astype(o_ref.dtype)


def matmul(x, w, *, tm):
    m, k = x.shape
    n = w.shape[1]
    return pl.pallas_call(
        _matmul_kernel,
        grid=(m // tm,),
        in_specs=[pl.BlockSpec((tm, k), lambda i: (i, 0)),
                  pl.BlockSpec((k, n), lambda i: (0, 0))],
        out_specs=pl.BlockSpec((tm, n), lambda i: (i, 0)),
        out_shape=jax.ShapeDtypeStruct((m, n), BF16),
        compiler_params=_params(1),
        name="matmul",
    )(x, w)


def _conv_kernel(a_ref, gt_ref, dww_ref, dwb_ref, lng_ref, lnb_ref, pw_ref, o_ref, buf_ref, cbuf_ref, *, t):
    s = pl.program_id(1)

    @pl.when(s == 0)
    def _():
        buf_ref[0:CONV_HALO, :] = jnp.zeros((CONV_HALO, buf_ref.shape[1]), F32)

    @pl.when(s > 0)
    def _():
        buf_ref[0:CONV_HALO, :] = buf_ref[t:t + CONV_HALO, :]

    a = a_ref[0].astype(F32)
    gt = gt_ref[0].astype(F32)
    buf_ref[CONV_HALO:CONV_HALO + t, :] = a * _sigmoid(gt)

    base = CONV_HALO - (CONV_WIDTH - 1)

    def conv_rows(ci, carry):
        r0 = pl.multiple_of(ci * CONV_ROWS, CONV_ROWS)
        for c in range(buf_ref.shape[1] // LANES):
            lanes = slice(c * LANES, (c + 1) * LANES)
            acc = dwb_ref[:, lanes]
            for b in range(SUBLANES):
                n = CONV_ROWS + (SUBLANES if b else 0)
                part = None
                for k in range(CONV_WIDTH):
                    if (base + k) % SUBLANES != b:
                        continue
                    term = dww_ref[k:k + 1, lanes] * buf_ref[pl.ds(r0 + (base + k - b), n), lanes]
                    part = term if part is None else part + term
                acc = acc + part[b:b + CONV_ROWS]
            cbuf_ref[pl.ds(r0, CONV_ROWS), lanes] = acc
        return carry

    lax.fori_loop(0, t // CONV_ROWS, conv_rows, 0)
    v = cbuf_ref[...]

    mu = jnp.mean(v, axis=-1, keepdims=True)
    vc = v - mu
    y = vc * lax.rsqrt(jnp.mean(vc * vc, axis=-1, keepdims=True) + EPS)
    y = y * lng_ref[...] + lnb_ref[...]
    y = y * _sigmoid(y)
    o_ref[0] = _dot(y.astype(BF16), pw_ref[...]).astype(o_ref.dtype)


def conv_module(p, dww, dwb, lng, lnb, pw, *, t):
    b, s, _ = p.shape
    c = pw.shape[0]
    vec = lambda: pl.BlockSpec((1, c), lambda bi, si: (0, 0))
    return pl.pallas_call(
        functools.partial(_conv_kernel, t=t),
        grid=(b, s // t),
        in_specs=[pl.BlockSpec((1, t, c), lambda bi, si: (bi, si, 0)),
                  pl.BlockSpec((1, t, c), lambda bi, si: (bi, si, 1)),
                  pl.BlockSpec((CONV_WIDTH, c), lambda bi, si: (0, 0)),
                  vec(), vec(), vec(),
                  pl.BlockSpec((c, c), lambda bi, si: (0, 0))],
        out_specs=pl.BlockSpec((1, t, c), lambda bi, si: (bi, si, 0)),
        out_shape=jax.ShapeDtypeStruct((b, s, c), BF16),
        scratch_shapes=[pltpu.VMEM((CONV_HALO + t, c), F32), pltpu.VMEM((t, c), F32)],
        compiler_params=_params(2),
        name="conv_module",
    )(p, p, dww, dwb, lng, lnb, pw)


def _compress_kernel(x_ref, pos_ref, w1s_ref, w2s_ref, o_ref, xf_ref, w1_ref, w2_ref):
    half = CMP_BLOCK // 2
    gw = N_KV_GROUPS * HEAD_DIM
    n_chunk = gw // LANES

    @pl.when(pl.program_id(0) == 0)
    def _():
        w1_ref[...] = jnp.zeros(w1_ref.shape, BF16)
        w2_ref[...] = jnp.zeros(w2_ref.shape, BF16)
        for g in range(N_KV_GROUPS):
            blk = slice(g * HEAD_DIM, (g + 1) * HEAD_DIM)
            w1_ref[:, :, blk, blk] = w1s_ref[...]
            w2_ref[:, blk, blk] = w2s_ref[...]

    for c in range(2 * n_chunk):
        xf_ref[c] = x_ref[0, :, c * LANES:(c + 1) * LANES].astype(F32)
    for kv in range(2):
        acc_a = jnp.zeros((N_CMP_PAD, gw), F32)
        acc_b = jnp.zeros((N_CMP_PAD, gw), F32)
        for l in range(half):
            rows = jnp.concatenate(
                [xf_ref[kv * n_chunk + c, pl.ds(l, N_CMP_PAD, stride=CMP_STRIDE), :] for c in range(n_chunk)],
                axis=1)
            ra = (rows + pos_ref[kv, l:l + 1, :]).astype(BF16)
            rb = (rows + pos_ref[kv, half + l:half + l + 1, :]).astype(BF16)
            acc_a = acc_a + _dot(ra, w1_ref[kv, l])
            acc_b = acc_b + _dot(rb, w1_ref[kv, half + l])
        z = acc_a + pltpu.roll(acc_b, N_CMP_PAD - 1, 0)
        h = z * _sigmoid(z)
        out = _dot(h.astype(BF16), w2_ref[kv]).astype(o_ref.dtype)
        for g in range(N_KV_GROUPS):
            o_ref[0, g, :, (1 - kv) * HEAD_DIM:(2 - kv) * HEAD_DIM] = out[:, g * HEAD_DIM:(g + 1) * HEAD_DIM]


def compress(p, col_block, pos4, w1, w2):
    b, s, _ = p.shape
    assert s == N_CMP_PAD * CMP_STRIDE
    gw = N_KV_GROUPS * HEAD_DIM
    return pl.pallas_call(
        _compress_kernel,
        grid=(b,),
        in_specs=[pl.BlockSpec((1, s, 2 * gw), lambda bi: (bi, 0, col_block)),
                  pl.BlockSpec(pos4.shape, lambda bi: (0, 0, 0)),
                  pl.BlockSpec(w1.shape, lambda bi: (0, 0, 0, 0)),
                  pl.BlockSpec(w2.shape, lambda bi: (0, 0, 0))],
        out_specs=pl.BlockSpec((1, N_KV_GROUPS, N_CMP_PAD, 2 * HEAD_DIM), lambda bi: (bi, 0, 0, 0)),
        out_shape=jax.ShapeDtypeStruct((b, N_KV_GROUPS, N_CMP_PAD, 2 * HEAD_DIM), BF16),
        scratch_shapes=[pltpu.VMEM((2 * gw // LANES, s, LANES), F32),
                        pltpu.VMEM((2, CMP_BLOCK, gw, gw), BF16),
                        pltpu.VMEM((2, gw, gw), BF16)],
        compiler_params=_params(1),
        name="nsa_compress",
    )(p, pos4, w1, w2)


def _nsa_kernel(q_ref, kv_ref, kvc_ref, gl_ref, gexp_ref, biasn_ref, biasc_ref, ovt_ref, kmask_ref,
                o_ref, qsa_ref, kaug_ref, vaug_ref, sbuf_ref, wbuf_ref, sig_ref, part_ref,
                *, tq):
    qi = pl.program_id(2)
    t0 = pl.multiple_of(qi * tq, tq)
    r = HEADS_PER_GROUP
    rows = r * tq
    hd = HEAD_DIM
    n_lt = tq // LANES

    @pl.when(qi == 0)
    def _():
        ones = jnp.ones((kv_ref.shape[1], hd), BF16)
        vaug_ref[...] = jnp.concatenate(
            [kv_ref[0, :, hd:2 * hd], ones, kv_ref[0, :, 3 * hd:4 * hd], ones], axis=1)
        kaug_ref[...] = jnp.concatenate([kv_ref[0, :, 0:hd], kmask_ref[...]], axis=1)

    q = q_ref[0]
    qs = jnp.concatenate([q[:, i * hd:(i + 1) * hd] for i in range(r)], axis=0)

    sig_ref[...] = _sigmoid(_dot(gl_ref[0], gexp_ref[0]))

    def gate(j):
        sig = sig_ref[...]
        return jnp.concatenate(
            [jnp.broadcast_to(sig[:, j * r + i:j * r + i + 1], (tq, LANES)) for i in range(r)],
            axis=0)

    kvc = kvc_ref[0, 0]
    s = _dot_nt(qs, kvc[:, hd:]) + biasc_ref[0, :, 0].reshape(rows, N_CMP_PAD)
    m = jnp.maximum(jnp.max(s, axis=-1, keepdims=True), MAX_FLOOR)
    p = jnp.exp2(s - m)
    p = p / jnp.maximum(jnp.sum(p, axis=-1, keepdims=True), 1e-30)
    part_ref[...] = _dot(p.astype(BF16), kvc)

    ps = p[0:tq] + p[tq:2 * tq] + p[2 * tq:3 * tq] + p[3 * tq:4 * tq]
    ps_hi = ps.astype(BF16)
    ps_lo = (ps - ps_hi.astype(F32)).astype(BF16)
    ovt = ovt_ref[...]
    n_sel = ovt.shape[0]
    imp_t = _dot_nt(ovt, ps_hi) + _dot_nt(ovt, ps_lo)
    mi = lax.broadcasted_iota(jnp.int32, (n_sel, tq), 0)
    blk = (t0 + lax.broadcasted_iota(jnp.int32, (n_sel, tq), 1)) >> int(math.log2(SEL_BLOCK))
    forced = (mi == 0) | (mi == blk) | (mi == blk - 1)
    score = jnp.where(forced, jnp.inf, jnp.where(mi <= blk, imp_t, -jnp.inf))
    n_grp = n_sel // SUBLANES
    grp = [score[SUBLANES * a:SUBLANES * (a + 1)] for a in range(n_grp)]
    cnt = [jnp.zeros((SUBLANES, tq), F32) for _ in range(n_grp)]
    for mp in range(n_sel):
        other = score[mp:mp + 1, :]
        for a in range(n_grp):
            if SUBLANES * a > mp:
                ahead = other >= grp[a]
            elif SUBLANES * (a + 1) - 1 <= mp:
                ahead = other > grp[a]
            else:
                later = (lax.broadcasted_iota(jnp.int32, (SUBLANES, tq), 0) + SUBLANES * a) > mp
                ahead = (other > grp[a]) | ((other == grp[a]) & later)
            cnt[a] = cnt[a] + jnp.where(ahead, 1.0, 0.0)
    cnt = jnp.concatenate(cnt, axis=0)
    drop_t = jnp.where(cnt < float(min(SEL_TOP_N, n_sel)), 0.0, 1.0)
    drop_t = jnp.concatenate([jnp.zeros((hd, tq), F32), drop_t,
                              jnp.zeros((LANES - hd - n_sel, tq), F32)], axis=0)
    drop = drop_t.T.astype(BF16)[:, hd:]
    qsa_ref[...] = jnp.concatenate(
        [jnp.concatenate([q[:, i * hd:(i + 1) * hd], drop], axis=1) for i in range(r)], axis=0)

    def lane_max(sc):
        mx = sc[:, 0:LANES]
        for c in range(1, n_lt):
            mx = jnp.maximum(mx, sc[:, c * LANES:(c + 1) * LANES])
        return mx

    def probs(sc, m_rep):
        return jnp.concatenate([jnp.exp2(sc[:, c * LANES:(c + 1) * LANES] - m_rep)
                                for c in range(n_lt)], axis=1).astype(BF16)

    def row_max(mx):
        return jnp.broadcast_to(jnp.max(mx, axis=-1, keepdims=True), (rows, LANES))

    ri = lax.broadcasted_iota(jnp.int32, (rows, tq), 0) & (tq - 1)
    ci = lax.broadcasted_iota(jnp.int32, (rows, tq), 1)

    def attend(n):
        def branch():
            tile = lambda j: slice(j * tq, (j + 1) * tq)
            mx_s = None
            for j in range(n + 1):
                sc = _dot_nt(qsa_ref[...], kaug_ref[tile(j), :])
                if j == n:
                    sc = sc + biasn_ref[0, 0]
                elif j == n - 1:
                    sc = sc + biasn_ref[0, 1]
                sbuf_ref[:, tile(j)] = sc
                mx_s = lane_max(sc) if mx_s is None else jnp.maximum(mx_s, lane_max(sc))
            win_tiles = [j for j in (n - 2, n - 1, n) if j >= 0]
            mx_w = None
            for w, j in enumerate(win_tiles):
                if j == n:
                    add = biasn_ref[0, 0]
                elif j == n - 1:
                    add = biasn_ref[0, 1]
                else:
                    add = jnp.where(ci > ri, 0.0, NEG)
                sc = _dot_nt(qsa_ref[:, 0:hd], kv_ref[0, tile(j), 2 * hd:3 * hd]) + add
                wbuf_ref[:, tile(w)] = sc
                mx_w = lane_max(sc) if mx_w is None else jnp.maximum(mx_w, lane_max(sc))
            m_sel = row_max(mx_s)
            m_win = row_max(mx_w)
            acc_w = None
            for w, j in enumerate(win_tiles):
                d = _dot(probs(wbuf_ref[:, tile(w)], m_win), vaug_ref[tile(j), LANES:2 * LANES])
                acc_w = d if acc_w is None else acc_w + d
            acc_s = None
            for j in range(n + 1):
                d = _dot(probs(sbuf_ref[:, tile(j)], m_sel), vaug_ref[tile(j), 0:LANES])
                acc_s = d if acc_s is None else acc_s + d
            o = (gate(0) * part_ref[...] + gate(2) * acc_w / pltpu.roll(acc_w, hd, 1)
                 + gate(1) * acc_s / pltpu.roll(acc_s, hd, 1))
            o_ref[0] = jnp.concatenate([o[i * tq:(i + 1) * tq, :hd] for i in range(r)],
                                       axis=1).astype(o_ref.dtype)
        return branch

    lax.switch(qi, [attend(n) for n in range(kv_ref.shape[1] // tq)])


def nsa_attention(p, kvc, gexp, biasn, biasc, ovt, expand, *, tq, q_block, kv_block, gl_block):
    b, s, _ = p.shape
    g = N_KV_GROUPS
    nq = s // tq
    rows = HEADS_PER_GROUP * tq
    assert WINDOW == 2 * tq and tq % LANES == 0
    return pl.pallas_call(
        functools.partial(_nsa_kernel, tq=tq),
        grid=(b, g, nq),
        in_specs=[pl.BlockSpec((1, tq, GROUP_WIDTH), lambda bi, gi, qi: (bi, qi, q_block + gi)),
                  pl.BlockSpec((1, s, GROUP_WIDTH), lambda bi, gi, qi: (bi, 0, kv_block + gi)),
                  pl.BlockSpec((1, 1, N_CMP_PAD, 2 * HEAD_DIM), lambda bi, gi, qi: (bi, gi, 0, 0)),
                  pl.BlockSpec((1, tq, LANES), lambda bi, gi, qi: (bi, qi, gl_block)),
                  pl.BlockSpec((1,) + gexp.shape[1:], lambda bi, gi, qi: (gi, 0, 0)),
                  pl.BlockSpec((1, 2, rows, tq), lambda bi, gi, qi: (gi, 0, 0, 0)),
                  pl.BlockSpec((1, HEADS_PER_GROUP, 1, tq, N_CMP_PAD), lambda bi, gi, qi: (gi, 0, qi, 0, 0)),
                  pl.BlockSpec(ovt.shape, lambda bi, gi, qi: (0, 0)),
                  pl.BlockSpec(expand.shape, lambda bi, gi, qi: (0, 0))],
        out_specs=pl.BlockSpec((1, tq, GROUP_WIDTH), lambda bi, gi, qi: (bi, qi, gi)),
        out_shape=jax.ShapeDtypeStruct((b, s, g * GROUP_WIDTH), BF16),
        scratch_shapes=[pltpu.VMEM((rows, LANES), BF16),
                        pltpu.VMEM((s, LANES), BF16),
                        pltpu.VMEM((s, 2 * LANES), BF16),
                        pltpu.VMEM((rows, s), F32),
                        pltpu.VMEM((rows, 3 * tq), F32),
                        pltpu.VMEM((tq, LANES), F32),
                        pltpu.VMEM((rows, LANES), F32)],
        compiler_params=_params(3),
        name="nsa_attention",
    )(p, p, kvc, p, gexp, biasn, biasc, ovt, expand)


def _merge_out_kernel(gc_ref, ga_ref, conv_ref, nsa_ref, w_ref, x_ref, o_ref):
    y = (_sigmoid(gc_ref[...].astype(F32)) * conv_ref[...].astype(F32)
         + _sigmoid(ga_ref[...].astype(F32)) * nsa_ref[...].astype(F32))
    o_ref[...] = x_ref[...] + _dot(y.astype(BF16), w_ref[...])


def merge_out(p2, conv, nsa, w, x, *, tm, gc_block, ga_block):
    m, d = x.shape
    tile = lambda blk: pl.BlockSpec((tm, d), lambda i: (i, blk))
    return pl.pallas_call(
        _merge_out_kernel,
        grid=(m // tm,),
        in_specs=[tile(gc_block), tile(ga_block), tile(0), tile(0),
                  pl.BlockSpec((d, d), lambda i: (0, 0)), tile(0)],
        out_specs=tile(0),
        out_shape=jax.ShapeDtypeStruct((m, d), F32),
        compiler_params=_params(1),
        name="merge_out",
    )(p2, p2, conv, nsa, w, x)


def _xattn_kernel(x_ref, g_ref, wq_ref, kv_ref, wo_ref, o_ref):
    x = x_ref[0]
    d = x.shape[-1]
    hd = d // X_HEADS
    h = _rms_norm(x, g_ref[...]).astype(BF16)
    q = (_dot(h, wq_ref[...]) * (1.0 / math.sqrt(hd))).astype(BF16)
    kv = kv_ref[0]
    outs = []
    for i in range(X_HEADS):
        s = _dot_nt(q[:, i * hd:(i + 1) * hd], kv[:, i * hd:(i + 1) * hd])
        p = jnp.exp(s - jnp.max(s, axis=-1, keepdims=True))
        l = jnp.sum(p, axis=-1, keepdims=True)
        outs.append(_dot(p.astype(BF16), kv[:, d + i * hd:d + (i + 1) * hd]) / l)
    o = jnp.concatenate(outs, axis=1).astype(BF16)
    o_ref[0] = x + _dot(o, wo_ref[...])


def cross_attention_block(x, g, wq, kv, wo, *, tm):
    b, s, d = x.shape
    mlen = kv.shape[1]
    return pl.pallas_call(
        _xattn_kernel,
        grid=(b, s // tm),
        in_specs=[pl.BlockSpec((1, tm, d), lambda bi, si: (bi, si, 0)),
                  pl.BlockSpec((1, d), lambda bi, si: (0, 0)),
                  pl.BlockSpec((d, d), lambda bi, si: (0, 0)),
                  pl.BlockSpec((1, mlen, 2 * d), lambda bi, si: (bi, 0, 0)),
                  pl.BlockSpec((d, d), lambda bi, si: (0, 0))],
        out_specs=pl.BlockSpec((1, tm, d), lambda bi, si: (bi, si, 0)),
        out_shape=jax.ShapeDtypeStruct((b, s, d), F32),
        compiler_params=_params(2),
        name="cross_attention",
    )(x, g, wq, kv, wo)


def _ffn_kernel(x_ref, g_ref, wa_ref, wb_ref, wo_ref, gout_ref, o_ref, *, norm_out):
    x = x_ref[...]
    xn = _rms_norm(x, g_ref[...]).astype(BF16)
    a = _dot(xn, wa_ref[...])
    bb = _dot(xn, wb_ref[...])
    y = x + _dot((a * _sigmoid(a) * bb).astype(BF16), wo_ref[...])
    o_ref[...] = _rms_norm(y, gout_ref[...]) if norm_out else y


def ffn_block(x, g, w_in, w_out, g_out, *, tm, norm_out):
    m, d = x.shape
    f = w_out.shape[0]
    once = pl.Buffered(1)
    return pl.pallas_call(
        functools.partial(_ffn_kernel, norm_out=norm_out),
        grid=(m // tm,),
        in_specs=[pl.BlockSpec((tm, d), lambda i: (i, 0)),
                  pl.BlockSpec((1, d), lambda i: (0, 0)),
                  pl.BlockSpec((d, f), lambda i: (0, 0), pipeline_mode=once),
                  pl.BlockSpec((d, f), lambda i: (0, 1), pipeline_mode=once),
                  pl.BlockSpec((f, d), lambda i: (0, 0), pipeline_mode=once),
                  pl.BlockSpec((1, d), lambda i: (0, 0))],
        out_specs=pl.BlockSpec((tm, d), lambda i: (i, 0)),
        out_shape=jax.ShapeDtypeStruct((m, d), F32),
        compiler_params=_params(1),
        name="ffn_block",
    )(x, g, w_in, w_in, w_out, g_out)


def _t5_bucket(dist):
    n = jnp.maximum(dist, 0)
    max_exact = REL_BUCKETS // 2
    nf = jnp.maximum(n, 1).astype(F32)
    large = max_exact + (jnp.log(nf / max_exact) / math.log(REL_MAX_DIST / max_exact)
                         * (REL_BUCKETS - max_exact)).astype(jnp.int32)
    large = jnp.minimum(large, REL_BUCKETS - 1)
    return jnp.where(n < max_exact, n, large)


def _bias_lookup(rel_bias, dist):
    onehot = jax.nn.one_hot(_t5_bucket(dist), REL_BUCKETS, dtype=F32)
    return jnp.einsum("bh,...b->h...", rel_bias, onehot, precision=lax.Precision.HIGHEST)


def _bias_tables(rel_bias, s, tq):
    g, r = N_KV_GROUPS, HEADS_PER_GROUP
    nq = s // tq
    rel_bias = rel_bias * LOG2E
    far = _bias_lookup(rel_bias, jnp.asarray(s, jnp.int32))
    i = jnp.arange(tq)[:, None]
    j = jnp.arange(tq)[None, :]

    def near(dist):
        bt = _bias_lookup(rel_bias, dist) - far[:, None, None]
        bt = jnp.where((dist >= 0)[None], bt, NEG)
        return bt.reshape(g, r * tq, tq)

    biasn = jnp.stack([near(i - j), near(i - j + tq)], axis=1)

    t = jnp.arange(s)[:, None]
    c_end = jnp.arange(N_CMP_PAD)[None, :] * CMP_STRIDE + CMP_BLOCK - 1
    bc = _bias_lookup(rel_bias, t - c_end) + jnp.where(t >= c_end, 0.0, NEG)[None]
    biasc = bc.reshape(g, r, nq, tq, N_CMP_PAD)
    return biasn, biasc


def _static_tables(s, tq):
    n_sel = s // SEL_BLOCK
    jj = np.arange(N_CMP_PAD)[None, :] * CMP_STRIDE
    mm0 = np.arange(n_sel)[:, None] * SEL_BLOCK
    n_cmp = (s - CMP_BLOCK) // CMP_STRIDE + 1
    ovt = ((jj < mm0 + SEL_BLOCK) & (jj + CMP_BLOCK > mm0) & (np.arange(N_CMP_PAD)[None, :] < n_cmp))
    assert n_sel <= HEAD_DIM
    expand = np.zeros((s, HEAD_DIM), np.float32)
    expand[np.arange(s), np.arange(s) // SEL_BLOCK] = NEG
    gexp = np.zeros((N_KV_GROUPS, LANES, LANES), np.float32)
    for g in range(N_KV_GROUPS):
        for j in range(3):
            for i in range(HEADS_PER_GROUP):
                gexp[g, j * N_HEADS + g * HEADS_PER_GROUP + i, j * HEADS_PER_GROUP + i] = 1.0
    d = np.arange(LANES + 1, s + 1).astype(np.float32)
    big = 16 + (np.log(d / 16) / math.log(REL_MAX_DIST / 16) * 16).astype(np.int32)
    assert np.all(np.minimum(big, REL_BUCKETS - 1) == REL_BUCKETS - 1)
    return (jnp.asarray(ovt.astype(np.float32), BF16), jnp.asarray(expand, BF16), jnp.asarray(gexp, BF16))


def _in_proj_perm(d):
    g, hd = N_KV_GROUPS, HEAD_DIM
    kvw = g * hd
    o_conv, o_q = 0, 2 * d
    o_kc = o_q + N_HEADS * hd
    o_vc, o_ks, o_vs, o_kw, o_vw = (o_kc + i * kvw for i in range(1, 6))
    o_gate = o_kc + 6 * kvw
    o_gc = o_gate + 3 * N_HEADS
    o_ga = o_gc + d
    segs = [(o_conv, 2 * d, None), (o_q, N_HEADS * hd, LOG2E / math.sqrt(hd))]
    for gi in range(g):
        segs += [(base + gi * hd, hd, None) for base in (o_ks, o_vs, o_kw, o_vw)]
    segs += [(o_gc, 2 * d, None), (o_kc, 2 * kvw, None), (o_gate, 3 * N_HEADS, None)]
    n_used = sum(n for _, n, _ in segs)
    return segs, (-n_used) % LANES


def _permute_in_proj(w_in):
    segs, n_pad = _in_proj_perm(w_in.shape[1])
    wt = jnp.swapaxes(w_in, 1, 2)
    parts = [(wt[:, a:a + n] if sc is None else wt[:, a:a + n] * sc).astype(BF16) for a, n, sc in segs]
    parts.append(jnp.zeros((wt.shape[0], n_pad, wt.shape[2]), BF16))
    return jnp.concatenate(parts, axis=1)


def kernel(x, mem, norm_mix_g, w_in, conv_dw_w, conv_dw_b, conv_ln_g, conv_ln_b, conv_pw_w, cmp_pos,
           cmp_w1, cmp_w2, w_out, norm_x_g, xq_w, xkv_w, xo_w, norm_ffn_g, ffn_in_w, ffn_out_w,
           rel_bias, final_norm_g):
    b, s, d = x.shape
    depth = w_in.shape[0]
    mlen = mem.shape[1]
    m = b * s
    tq = 256

    w_in_pt = _permute_in_proj(w_in)
    n_p = w_in_pt.shape[1]
    q_block = (2 * d) // GROUP_WIDTH
    kv_block = q_block + N_KV_GROUPS
    gc_block = (3 * d + N_KV_GROUPS * GROUP_WIDTH) // d
    ga_block = gc_block + 1
    cmp_block = (6 * d) // (2 * N_KV_GROUPS * HEAD_DIM)
    gl_block = (6 * d + 2 * N_KV_GROUPS * HEAD_DIM) // LANES

    ovt, expand, gexp = _static_tables(s, tq)
    biasn, biasc = _bias_tables(rel_bias, s, tq)

    row = lambda v: v.reshape(1, -1)

    xf = x.reshape(m, d)
    for l in range(depth):
        p = norm_matmul(xf, row(norm_mix_g[l]), w_in_pt, l, tm=512)
        p3 = p.reshape(b, s, n_p)
        conv = conv_module(p3, conv_dw_w[l], row(conv_dw_b[l]), row(conv_ln_g[l]), row(conv_ln_b[l]),
                           conv_pw_w[l].astype(BF16), t=512)
        pos4 = jnp.tile(cmp_pos[l], (1, 1, N_KV_GROUPS))
        kvc = compress(p3, cmp_block, pos4, cmp_w1[l].astype(BF16), cmp_w2[l].astype(BF16))
        nsa = nsa_attention(p3, kvc, gexp, biasn, biasc, ovt, expand, tq=tq,
                            q_block=q_block, kv_block=kv_block, gl_block=gl_block)
        xf = merge_out(p, conv.reshape(m, d), nsa.reshape(m, d), w_out[l].astype(BF16), xf,
                       tm=512, gc_block=gc_block, ga_block=ga_block)
        kvx = matmul(mem.reshape(b * mlen, d), xkv_w[l].astype(BF16), tm=mlen)
        xf = cross_attention_block(xf.reshape(b, s, d), row(norm_x_g[l]), xq_w[l].astype(BF16),
                                   kvx.reshape(b, mlen, 2 * d), xo_w[l].astype(BF16), tm=512).reshape(m, d)
        xf = ffn_block(xf, row(norm_ffn_g[l]), ffn_in_w[l].astype(BF16), ffn_out_w[l].astype(BF16),
                       row(final_norm_g), tm=512, norm_out=(l == depth - 1))
    return xf.reshape(b, s, d)
```

```python
import functools
import math

import numpy as np
import jax
import jax.numpy as jnp
from jax import lax
from jax.experimental import pallas as pl
from jax.experimental.pallas import tpu as pltpu

F32 = jnp.float32
BF16 = jnp.bfloat16

HEAD_DIM = 64
N_KV_GROUPS = 4
HEADS_PER_GROUP = 4
N_HEADS = N_KV_GROUPS * HEADS_PER_GROUP
GROUP_WIDTH = HEADS_PER_GROUP * HEAD_DIM
CMP_BLOCK = 32
CMP_STRIDE = 16
SEL_BLOCK = 64
SEL_TOP_N = 16
WINDOW = 512
CONV_WIDTH = 31
X_HEADS = 4
REL_BUCKETS = 32
REL_MAX_DIST = 128
EPS = 1e-6
NEG = -1e30
MAX_FLOOR = -1e20
LOG2E = math.log2(math.e)

LANES = 128
SUBLANES = 8
CONV_ROWS = 64
CONV_HALO = 32
N_CMP_PAD = 128
VMEM_LIMIT = 56 * 1024 * 1024


def _sigmoid(x):
    return 0.5 * jnp.tanh(0.5 * x) + 0.5


def _dot(a, b):
    return jnp.dot(a, b, preferred_element_type=F32)


def _dot_nt(a, b):
    return lax.dot_general(a, b, (((1,), (1,)), ((), ())), preferred_element_type=F32)


def _rms_norm(x, g):
    return x * lax.rsqrt(jnp.mean(x * x, axis=-1, keepdims=True) + EPS) * g


def _params(n_axes):
    return pltpu.CompilerParams(dimension_semantics=("arbitrary",) * n_axes,
                                vmem_limit_bytes=VMEM_LIMIT)


def _norm_matmul_kernel(x_ref, g_ref, wt_ref, o_ref):
    xn = _rms_norm(x_ref[...], g_ref[...]).astype(BF16)
    o_ref[...] = _dot_nt(xn, wt_ref[...]).astype(o_ref.dtype)


def norm_matmul(x, g, wt, layer, *, tm):
    m, k = x.shape
    n = wt.shape[1]
    return pl.pallas_call(
        _norm_matmul_kernel,
        grid=(m // tm,),
        in_specs=[pl.BlockSpec((tm, k), lambda i: (i, 0)),
                  pl.BlockSpec((1, k), lambda i: (0, 0)),
                  pl.BlockSpec((None, n, k), lambda i: (layer, 0, 0), pipeline_mode=pl.Buffered(1))],
        out_specs=pl.BlockSpec((tm, n), lambda i: (i, 0)),
        out_shape=jax.ShapeDtypeStruct((m, n), BF16),
        compiler_params=_params(1),
        name="norm_matmul",
    )(x, g, wt)


def _matmul_kernel(x_ref, w_ref, o_ref):
    o_ref[...] = _dot(x_ref[...].astype(BF16), w_ref[...]).astype(o_ref.dtype)


def matmul(x, w, *, tm):
    m, k = x.shape
    n = w.shape[1]
    return pl.pallas_call(
        _matmul_kernel,
        grid=(m // tm,),
        in_specs=[pl.BlockSpec((tm, k), lambda i: (i, 0)),
                  pl.BlockSpec((k, n), lambda i: (0, 0))],
        out_specs=pl.BlockSpec((tm, n), lambda i: (i, 0)),
        out_shape=jax.ShapeDtypeStruct((m, n), BF16),
        compiler_params=_params(1),
        name="matmul",
    )(x, w)


def _conv_kernel(a_ref, gt_ref, dww_ref, dwb_ref, lng_ref, lnb_ref, pw_ref, o_ref, buf_ref, cbuf_ref, *, t):
    s = pl.program_id(1)

    @pl.when(s == 0)
    def _():
        buf_ref[0:CONV_HALO, :] = jnp.zeros((CONV_HALO, buf_ref.shape[1]), F32)

    @pl.when(s > 0)
    def _():
        buf_ref[0:CONV_HALO, :] = buf_ref[t:t + CONV_HALO, :]

    a = a_ref[0].astype(F32)
    gt = gt_ref[0].astype(F32)
    buf_ref[CONV_HALO:CONV_HALO + t, :] = a * _sigmoid(gt)

    base = CONV_HALO - (CONV_WIDTH - 1)

    def conv_rows(ci, carry):
        r0 = pl.multiple_of(ci * CONV_ROWS, CONV_ROWS)
        for c in range(buf_ref.shape[1] // LANES):
            lanes = slice(c * LANES, (c + 1) * LANES)
            acc = dwb_ref[:, lanes]
            for b in range(SUBLANES):
                n = CONV_ROWS + (SUBLANES if b else 0)
                part = None
                for k in range(CONV_WIDTH):
                    if (base + k) % SUBLANES != b:
                        continue
                    term = dww_ref[k:k + 1, lanes] * buf_ref[pl.ds(r0 + (base + k - b), n), lanes]
                    part = term if part is None else part + term
                acc = acc + part[b:b + CONV_ROWS]
            cbuf_ref[pl.ds(r0, CONV_ROWS), lanes] = acc
        return carry

    lax.fori_loop(0, t // CONV_ROWS, conv_rows, 0)
    v = cbuf_ref[...]

    mu = jnp.mean(v, axis=-1, keepdims=True)
    vc = v - mu
    y = vc * lax.rsqrt(jnp.mean(vc * vc, axis=-1, keepdims=True) + EPS)
    y = y * lng_ref[...] + lnb_ref[...]
    y = y * _sigmoid(y)
    o_ref[0] = _dot(y.astype(BF16), pw_ref[...]).astype(o_ref.dtype)


def conv_module(p, dww, dwb, lng, lnb, pw, *, t):
    b, s, _ = p.shape
    c = pw.shape[0]
    vec = lambda: pl.BlockSpec((1, c), lambda bi, si: (0, 0))
    return pl.pallas_call(
        functools.partial(_conv_kernel, t=t),
        grid=(b, s // t),
        in_specs=[pl.BlockSpec((1, t, c), lambda bi, si: (bi, si, 0)),
                  pl.BlockSpec((1, t, c), lambda bi, si: (bi, si, 1)),
                  pl.BlockSpec((CONV_WIDTH, c), lambda bi, si: (0, 0)),
                  vec(), vec(), vec(),
                  pl.BlockSpec((c, c), lambda bi, si: (0, 0))],
        out_specs=pl.BlockSpec((1, t, c), lambda bi, si: (bi, si, 0)),
        out_shape=jax.ShapeDtypeStruct((b, s, c), BF16),
        scratch_shapes=[pltpu.VMEM((CONV_HALO + t, c), F32), pltpu.VMEM((t, c), F32)],
        compiler_params=_params(2),
        name="conv_module",
    )(p, p, dww, dwb, lng, lnb, pw)


def _compress_kernel(x_ref, pos_ref, w1s_ref, w2s_ref, o_ref, xf_ref, w1_ref, w2_ref):
    half = CMP_BLOCK // 2
    gw = N_KV_GROUPS * HEAD_DIM
    n_chunk = gw // LANES

    @pl.when(pl.program_id(0) == 0)
    def _():
        w1_ref[...] = jnp.zeros(w1_ref.shape, BF16)
        w2_ref[...] = jnp.zeros(w2_ref.shape, BF16)
        for g in range(N_KV_GROUPS):
            blk = slice(g * HEAD_DIM, (g + 1) * HEAD_DIM)
            w1_ref[:, :, blk, blk] = w1s_ref[...]
            w2_ref[:, blk, blk] = w2s_ref[...]

    for c in range(2 * n_chunk):
        xf_ref[c] = x_ref[0, :, c * LANES:(c + 1) * LANES].astype(F32)
    for kv in range(2):
        acc_a = jnp.zeros((N_CMP_PAD, gw), F32)
        acc_b = jnp.zeros((N_CMP_PAD, gw), F32)
        for l in range(half):
            rows = jnp.concatenate(
                [xf_ref[kv * n_chunk + c, pl.ds(l, N_CMP_PAD, stride=CMP_STRIDE), :] for c in range(n_chunk)],
                axis=1)
            ra = (rows + pos_ref[kv, l:l + 1, :]).astype(BF16)
            rb = (rows + pos_ref[kv, half + l:half + l + 1, :]).astype(BF16)
            acc_a = acc_a + _dot(ra, w1_ref[kv, l])
            acc_b = acc_b + _dot(rb, w1_ref[kv, half + l])
        z = acc_a + pltpu.roll(acc_b, N_CMP_PAD - 1, 0)
        h = z * _sigmoid(z)
        out = _dot(h.astype(BF16), w2_ref[kv]).astype(o_ref.dtype)
        for g in range(N_KV_GROUPS):
            o_ref[0, g, :, (1 - kv) * HEAD_DIM:(2 - kv) * HEAD_DIM] = out[:, g * HEAD_DIM:(g + 1) * HEAD_DIM]


def compress(p, col_block, pos4, w1, w2):
    b, s, _ = p.shape
    assert s == N_CMP_PAD * CMP_STRIDE
    gw = N_KV_GROUPS * HEAD_DIM
    return pl.pallas_call(
        _compress_kernel,
        grid=(b,),
        in_specs=[pl.BlockSpec((1, s, 2 * gw), lambda bi: (bi, 0, col_block)),
                  pl.BlockSpec(pos4.shape, lambda bi: (0, 0, 0)),
                  pl.BlockSpec(w1.shape, lambda bi: (0, 0, 0, 0)),
                  pl.BlockSpec(w2.shape, lambda bi: (0, 0, 0))],
        out_specs=pl.BlockSpec((1, N_KV_GROUPS, N_CMP_PAD, 2 * HEAD_DIM), lambda bi: (bi, 0, 0, 0)),
        out_shape=jax.ShapeDtypeStruct((b, N_KV_GROUPS, N_CMP_PAD, 2 * HEAD_DIM), BF16),
        scratch_shapes=[pltpu.VMEM((2 * gw // LANES, s, LANES), F32),
                        pltpu.VMEM((2, CMP_BLOCK, gw, gw), BF16),
                        pltpu.VMEM((2, gw, gw), BF16)],
        compiler_params=_params(1),
        name="nsa_compress",
    )(p, pos4, w1, w2)


def _nsa_kernel(q_ref, kv_ref, kvc_ref, gl_ref, gexp_ref, biasn_ref, biasc_ref, ovt_ref, kmask_ref,
                qn_ref, gln_ref, biascn_ref, o_ref, qsa_ref, kaug_ref, vaug_ref, sbuf_ref, wbuf_ref, sig_ref, part_ref,
                *, tq):
    qi = pl.program_id(2)
    t0 = pl.multiple_of(qi * tq, tq)
    r = HEADS_PER_GROUP
    rows = r * tq
    hd = HEAD_DIM
    n_lt = tq // LANES

    def prepare(q, gl, bc, t_base, slot):
        qs = jnp.concatenate([q[:, i * hd:(i + 1) * hd] for i in range(r)], axis=0)

        sig_ref[slot] = _sigmoid(_dot(gl, gexp_ref[0]))

        kvc = kvc_ref[0, 0]
        s = _dot_nt(qs, kvc[:, hd:]) + bc.reshape(rows, N_CMP_PAD)
        m = jnp.maximum(jnp.max(s, axis=-1, keepdims=True), MAX_FLOOR)
        p = jnp.exp2(s - m)
        p = p / jnp.maximum(jnp.sum(p, axis=-1, keepdims=True), 1e-30)
        part_ref[slot] = _dot(p.astype(BF16), kvc)

        ps = p[0:tq] + p[tq:2 * tq] + p[2 * tq:3 * tq] + p[3 * tq:4 * tq]
        ps_hi = ps.astype(BF16)
        ps_lo = (ps - ps_hi.astype(F32)).astype(BF16)
        ovt = ovt_ref[...]
        n_sel = ovt.shape[0]
        imp_t = _dot_nt(ovt, ps_hi) + _dot_nt(ovt, ps_lo)
        mi = lax.broadcasted_iota(jnp.int32, (n_sel, tq), 0)
        blk = (t_base + lax.broadcasted_iota(jnp.int32, (n_sel, tq), 1)) >> int(math.log2(SEL_BLOCK))
        forced = (mi == 0) | (mi == blk) | (mi == blk - 1)
        score = jnp.where(forced, jnp.inf, jnp.where(mi <= blk, imp_t, -jnp.inf))
        n_grp = n_sel // SUBLANES
        grp = [score[SUBLANES * a:SUBLANES * (a + 1)] for a in range(n_grp)]
        cnt = [jnp.zeros((SUBLANES, tq), F32) for _ in range(n_grp)]
        for mp in range(n_sel):
            other = score[mp:mp + 1, :]
            for a in range(n_grp):
                if SUBLANES * a > mp:
                    ahead = other >= grp[a]
                elif SUBLANES * (a + 1) - 1 <= mp:
                    ahead = other > grp[a]
                else:
                    later = (lax.broadcasted_iota(jnp.int32, (SUBLANES, tq), 0) + SUBLANES * a) > mp
                    ahead = (other > grp[a]) | ((other == grp[a]) & later)
                cnt[a] = cnt[a] + jnp.where(ahead, 1.0, 0.0)
        cnt = jnp.concatenate(cnt, axis=0)
        drop_t = jnp.where(cnt < float(min(SEL_TOP_N, n_sel)), 0.0, 1.0)
        drop_t = jnp.concatenate([jnp.zeros((hd, tq), F32), drop_t,
                                  jnp.zeros((LANES - hd - n_sel, tq), F32)], axis=0)
        drop = drop_t.T.astype(BF16)[:, hd:]
        qsa_ref[slot] = jnp.concatenate(
            [jnp.concatenate([q[:, i * hd:(i + 1) * hd], drop], axis=1) for i in range(r)], axis=0)

    def gate(j, slot):
        sig = sig_ref[slot]
        return jnp.concatenate(
            [jnp.broadcast_to(sig[:, j * r + i:j * r + i + 1], (tq, LANES)) for i in range(r)],
            axis=0)

    @pl.when(qi == 0)
    def _():
        ones = jnp.ones((kv_ref.shape[1], hd), BF16)
        vaug_ref[...] = jnp.concatenate(
            [kv_ref[0, :, hd:2 * hd], ones, kv_ref[0, :, 3 * hd:4 * hd], ones], axis=1)
        kaug_ref[...] = jnp.concatenate([kv_ref[0, :, 0:hd], kmask_ref[...]], axis=1)
        prepare(q_ref[0], gl_ref[0], biasc_ref[0, :, 0], 0, 0)


    def lane_max(sc):
        mx = sc[:, 0:LANES]
        for c in range(1, n_lt):
            mx = jnp.maximum(mx, sc[:, c * LANES:(c + 1) * LANES])
        return mx

    def probs(sc, m_rep):
        return jnp.concatenate([jnp.exp2(sc[:, c * LANES:(c + 1) * LANES] - m_rep)
                                for c in range(n_lt)], axis=1).astype(BF16)

    def row_max(mx):
        return jnp.broadcast_to(jnp.max(mx, axis=-1, keepdims=True), (rows, LANES))

    ri = lax.broadcasted_iota(jnp.int32, (rows, tq), 0) & (tq - 1)
    ci = lax.broadcasted_iota(jnp.int32, (rows, tq), 1)

    n_tiles = kv_ref.shape[1] // tq

    def attend(n):
        def branch():
            tile = lambda j: slice(j * tq, (j + 1) * tq)
            slot = n % 2
            if n + 1 < n_tiles:
                prepare(qn_ref[0], gln_ref[0], biascn_ref[0, :, 0], (n + 1) * tq, 1 - slot)
            mx_s = None
            for j in range(n + 1):
                sc = _dot_nt(qsa_ref[slot], kaug_ref[tile(j), :])
                if j == n:
                    sc = sc + biasn_ref[0, 0]
                elif j == n - 1:
                    sc = sc + biasn_ref[0, 1]
                sbuf_ref[:, tile(j)] = sc
                mx_s = lane_max(sc) if mx_s is None else jnp.maximum(mx_s, lane_max(sc))
            win_tiles = [j for j in (n - 2, n - 1, n) if j >= 0]
            mx_w = None
            for w, j in enumerate(win_tiles):
                if j == n:
                    add = biasn_ref[0, 0]
                elif j == n - 1:
                    add = biasn_ref[0, 1]
                else:
                    add = jnp.where(ci > ri, 0.0, NEG)
                sc = _dot_nt(qsa_ref[slot, :, 0:hd], kv_ref[0, tile(j), 2 * hd:3 * hd]) + add
                wbuf_ref[:, tile(w)] = sc
                mx_w = lane_max(sc) if mx_w is None else jnp.maximum(mx_w, lane_max(sc))
            m_sel = row_max(mx_s)
            m_win = row_max(mx_w)
            acc_w = None
            for w, j in enumerate(win_tiles):
                d = _dot(probs(wbuf_ref[:, tile(w)], m_win), vaug_ref[tile(j), LANES:2 * LANES])
                acc_w = d if acc_w is None else acc_w + d
            acc_s = None
            for j in range(n + 1):
                d = _dot(probs(sbuf_ref[:, tile(j)], m_sel), vaug_ref[tile(j), 0:LANES])
                acc_s = d if acc_s is None else acc_s + d
            o = (gate(0, slot) * part_ref[slot] + gate(2, slot) * acc_w / pltpu.roll(acc_w, hd, 1)
                 + gate(1, slot) * acc_s / pltpu.roll(acc_s, hd, 1))
            o_ref[0] = jnp.concatenate([o[i * tq:(i + 1) * tq, :hd] for i in range(r)],
                                       axis=1).astype(o_ref.dtype)
        return branch

    lax.switch(qi, [attend(n) for n in range(n_tiles)])


def nsa_attention(p, kvc, gexp, biasn, biasc, ovt, expand, *, tq, q_block, kv_block, gl_block):
    b, s, _ = p.shape
    g = N_KV_GROUPS
    nq = s // tq
    rows = HEADS_PER_GROUP * tq
    assert WINDOW == 2 * tq and tq % LANES == 0
    return pl.pallas_call(
        functools.partial(_nsa_kernel, tq=tq),
        grid=(b, g, nq),
        in_specs=[pl.BlockSpec((1, tq, GROUP_WIDTH), lambda bi, gi, qi: (bi, qi, q_block + gi)),
                  pl.BlockSpec((1, s, GROUP_WIDTH), lambda bi, gi, qi: (bi, 0, kv_block + gi)),
                  pl.BlockSpec((1, 1, N_CMP_PAD, 2 * HEAD_DIM), lambda bi, gi, qi: (bi, gi, 0, 0)),
                  pl.BlockSpec((1, tq, LANES), lambda bi, gi, qi: (bi, qi, gl_block)),
                  pl.BlockSpec((1,) + gexp.shape[1:], lambda bi, gi, qi: (gi, 0, 0)),
                  pl.BlockSpec((1, 2, rows, tq), lambda bi, gi, qi: (gi, 0, 0, 0)),
                  pl.BlockSpec((1, HEADS_PER_GROUP, 1, tq, N_CMP_PAD), lambda bi, gi, qi: (gi, 0, qi, 0, 0)),
                  pl.BlockSpec(ovt.shape, lambda bi, gi, qi: (0, 0)),
                  pl.BlockSpec(expand.shape, lambda bi, gi, qi: (0, 0)),
                  pl.BlockSpec((1, tq, GROUP_WIDTH),
                               lambda bi, gi, qi: (bi, jnp.minimum(qi + 1, nq - 1), q_block + gi)),
                  pl.BlockSpec((1, tq, LANES), lambda bi, gi, qi: (bi, jnp.minimum(qi + 1, nq - 1), gl_block)),
                  pl.BlockSpec((1, HEADS_PER_GROUP, 1, tq, N_CMP_PAD),
                               lambda bi, gi, qi: (gi, 0, jnp.minimum(qi + 1, nq - 1), 0, 0))],
        out_specs=pl.BlockSpec((1, tq, GROUP_WIDTH), lambda bi, gi, qi: (bi, qi, gi)),
        out_shape=jax.ShapeDtypeStruct((b, s, g * GROUP_WIDTH), BF16),
        scratch_shapes=[pltpu.VMEM((2, rows, LANES), BF16),
                        pltpu.VMEM((s, LANES), BF16),
                        pltpu.VMEM((s, 2 * LANES), BF16),
                        pltpu.VMEM((rows, s), F32),
                        pltpu.VMEM((rows, 3 * tq), F32),
                        pltpu.VMEM((2, tq, LANES), F32),
                        pltpu.VMEM((2, rows, LANES), F32)],
        compiler_params=_params(3),
        name="nsa_attention",
    )(p, p, kvc, p, gexp, biasn, biasc, ovt, expand, p, p, biasc)


def _merge_out_kernel(gc_ref, ga_ref, conv_ref, nsa_ref, w_ref, x_ref, o_ref):
    y = (_sigmoid(gc_ref[...].astype(F32)) * conv_ref[...].astype(F32)
         + _sigmoid(ga_ref[...].astype(F32)) * nsa_ref[...].astype(F32))
    o_ref[...] = x_ref[...] + _dot(y.astype(BF16), w_ref[...])


def merge_out(p2, conv, nsa, w, x, *, tm, gc_block, ga_block):
    m, d = x.shape
    tile = lambda blk: pl.BlockSpec((tm, d), lambda i: (i, blk))
    return pl.pallas_call(
        _merge_out_kernel,
        grid=(m // tm,),
        in_specs=[tile(gc_block), tile(ga_block), tile(0), tile(0),
                  pl.BlockSpec((d, d), lambda i: (0, 0)), tile(0)],
        out_specs=tile(0),
        out_shape=jax.ShapeDtypeStruct((m, d), F32),
        compiler_params=_params(1),
        name="merge_out",
    )(p2, p2, conv, nsa, w, x)


def _xattn_kernel(x_ref, g_ref, wq_ref, kv_ref, wo_ref, o_ref):
    x = x_ref[0]
    d = x.shape[-1]
    hd = d // X_HEADS
    h = _rms_norm(x, g_ref[...]).astype(BF16)
    q = (_dot(h, wq_ref[...]) * (1.0 / math.sqrt(hd))).astype(BF16)
    kv = kv_ref[0]
    outs = []
    for i in range(X_HEADS):
        s = _dot_nt(q[:, i * hd:(i + 1) * hd], kv[:, i * hd:(i + 1) * hd])
        p = jnp.exp(s - jnp.max(s, axis=-1, keepdims=True))
        l = jnp.sum(p, axis=-1, keepdims=True)
        outs.append(_dot(p.astype(BF16), kv[:, d + i * hd:d + (i + 1) * hd]) / l)
    o = jnp.concatenate(outs, axis=1).astype(BF16)
    o_ref[0] = x + _dot(o, wo_ref[...])


def cross_attention_block(x, g, wq, kv, wo, *, tm):
    b, s, d = x.shape
    mlen = kv.shape[1]
    return pl.pallas_call(
        _xattn_kernel,
        grid=(b, s // tm),
        in_specs=[pl.BlockSpec((1, tm, d), lambda bi, si: (bi, si, 0)),
                  pl.BlockSpec((1, d), lambda bi, si: (0, 0)),
                  pl.BlockSpec((d, d), lambda bi, si: (0, 0)),
                  pl.BlockSpec((1, mlen, 2 * d), lambda bi, si: (bi, 0, 0)),
                  pl.BlockSpec((d, d), lambda bi, si: (0, 0))],
        out_specs=pl.BlockSpec((1, tm, d), lambda bi, si: (bi, si, 0)),
        out_shape=jax.ShapeDtypeStruct((b, s, d), F32),
        compiler_params=_params(2),
        name="cross_attention",
    )(x, g, wq, kv, wo)


def _ffn_kernel(x_ref, g_ref, wa_ref, wb_ref, wo_ref, gout_ref, o_ref, *, norm_out):
    x = x_ref[...]
    xn = _rms_norm(x, g_ref[...]).astype(BF16)
    a = _dot(xn, wa_ref[...])
    bb = _dot(xn, wb_ref[...])
    y = x + _dot((a * _sigmoid(a) * bb).astype(BF16), wo_ref[...])
    o_ref[...] = _rms_norm(y, gout_ref[...]) if norm_out else y


def ffn_block(x, g, w_in, w_out, g_out, *, tm, norm_out):
    m, d = x.shape
    f = w_out.shape[0]
    once = pl.Buffered(1)
    return pl.pallas_call(
        functools.partial(_ffn_kernel, norm_out=norm_out),
        grid=(m // tm,),
        in_specs=[pl.BlockSpec((tm, d), lambda i: (i, 0)),
                  pl.BlockSpec((1, d), lambda i: (0, 0)),
                  pl.BlockSpec((d, f), lambda i: (0, 0), pipeline_mode=once),
                  pl.BlockSpec((d, f), lambda i: (0, 1), pipeline_mode=once),
                  pl.BlockSpec((f, d), lambda i: (0, 0), pipeline_mode=once),
                  pl.BlockSpec((1, d), lambda i: (0, 0))],
        out_specs=pl.BlockSpec((tm, d), lambda i: (i, 0)),
        out_shape=jax.ShapeDtypeStruct((m, d), F32),
        compiler_params=_params(1),
        name="ffn_block",
    )(x, g, w_in, w_in, w_out, g_out)


def _t5_bucket(dist):
    n = jnp.maximum(dist, 0)
    max_exact = REL_BUCKETS // 2
    nf = jnp.maximum(n, 1).astype(F32)
    large = max_exact + (jnp.log(nf / max_exact) / math.log(REL_MAX_DIST / max_exact)
                         * (REL_BUCKETS - max_exact)).astype(jnp.int32)
    large = jnp.minimum(large, REL_BUCKETS - 1)
    return jnp.where(n < max_exact, n, large)


def _bias_lookup(rel_bias, dist):
    onehot = jax.nn.one_hot(_t5_bucket(dist), REL_BUCKETS, dtype=F32)
    return jnp.einsum("bh,...b->h...", rel_bias, onehot, precision=lax.Precision.HIGHEST)


def _bias_tables(rel_bias, s, tq):
    g, r = N_KV_GROUPS, HEADS_PER_GROUP
    nq = s // tq
    rel_bias = rel_bias * LOG2E
    far = _bias_lookup(rel_bias, jnp.asarray(s, jnp.int32))
    i = jnp.arange(tq)[:, None]
    j = jnp.arange(tq)[None, :]

    def near(dist):
        bt = _bias_lookup(rel_bias, dist) - far[:, None, None]
        bt = jnp.where((dist >= 0)[None], bt, NEG)
        return bt.reshape(g, r * tq, tq)

    biasn = jnp.stack([near(i - j), near(i - j + tq)], axis=1)

    t = jnp.arange(s)[:, None]
    c_end = jnp.arange(N_CMP_PAD)[None, :] * CMP_STRIDE + CMP_BLOCK - 1
    bc = _bias_lookup(rel_bias, t - c_end) + jnp.where(t >= c_end, 0.0, NEG)[None]
    biasc = bc.reshape(g, r, nq, tq, N_CMP_PAD)
    return biasn, biasc


def _static_tables(s, tq):
    n_sel = s // SEL_BLOCK
    jj = np.arange(N_CMP_PAD)[None, :] * CMP_STRIDE
    mm0 = np.arange(n_sel)[:, None] * SEL_BLOCK
    n_cmp = (s - CMP_BLOCK) // CMP_STRIDE + 1
    ovt = ((jj < mm0 + SEL_BLOCK) & (jj + CMP_BLOCK > mm0) & (np.arange(N_CMP_PAD)[None, :] < n_cmp))
    assert n_sel <= HEAD_DIM
    expand = np.zeros((s, HEAD_DIM), np.float32)
    expand[np.arange(s), np.arange(s) // SEL_BLOCK] = NEG
    gexp = np.zeros((N_KV_GROUPS, LANES, LANES), np.float32)
    for g in range(N_KV_GROUPS):
        for j in range(3):
            for i in range(HEADS_PER_GROUP):
                gexp[g, j * N_HEADS + g * HEADS_PER_GROUP + i, j * HEADS_PER_GROUP + i] = 1.0
    d = np.arange(LANES + 1, s + 1).astype(np.float32)
    big = 16 + (np.log(d / 16) / math.log(REL_MAX_DIST / 16) * 16).astype(np.int32)
    assert np.all(np.minimum(big, REL_BUCKETS - 1) == REL_BUCKETS - 1)
    return (jnp.asarray(ovt.astype(np.float32), BF16), jnp.asarray(expand, BF16), jnp.asarray(gexp, BF16))


def _in_proj_perm(d):
    g, hd = N_KV_GROUPS, HEAD_DIM
    kvw = g * hd
    o_conv, o_q = 0, 2 * d
    o_kc = o_q + N_HEADS * hd
    o_vc, o_ks, o_vs, o_kw, o_vw = (o_kc + i * kvw for i in range(1, 6))
    o_gate = o_kc + 6 * kvw
    o_gc = o_gate + 3 * N_HEADS
    o_ga = o_gc + d
    segs = [(o_conv, 2 * d, None), (o_q, N_HEADS * hd, LOG2E / math.sqrt(hd))]
    for gi in range(g):
        segs += [(base + gi * hd, hd, None) for base in (o_ks, o_vs, o_kw, o_vw)]
    segs += [(o_gc, 2 * d, None), (o_kc, 2 * kvw, None), (o_gate, 3 * N_HEADS, None)]
    n_used = sum(n for _, n, _ in segs)
    return segs, (-n_used) % LANES


def _permute_in_proj(w_in):
    segs, n_pad = _in_proj_perm(w_in.shape[1])
    wt = jnp.swapaxes(w_in, 1, 2)
    parts = [(wt[:, a:a + n] if sc is None else wt[:, a:a + n] * sc).astype(BF16) for a, n, sc in segs]
    parts.append(jnp.zeros((wt.shape[0], n_pad, wt.shape[2]), BF16))
    return jnp.concatenate(parts, axis=1)


def kernel(x, mem, norm_mix_g, w_in, conv_dw_w, conv_dw_b, conv_ln_g, conv_ln_b, conv_pw_w, cmp_pos,
           cmp_w1, cmp_w2, w_out, norm_x_g, xq_w, xkv_w, xo_w, norm_ffn_g, ffn_in_w, ffn_out_w,
           rel_bias, final_norm_g):
    b, s, d = x.shape
    depth = w_in.shape[0]
    mlen = mem.shape[1]
    m = b * s
    tq = 256

    w_in_pt = _permute_in_proj(w_in)
    n_p = w_in_pt.shape[1]
    q_block = (2 * d) // GROUP_WIDTH
    kv_block = q_block + N_KV_GROUPS
    gc_block = (3 * d + N_KV_GROUPS * GROUP_WIDTH) // d
    ga_block = gc_block + 1
    cmp_block = (6 * d) // (2 * N_KV_GROUPS * HEAD_DIM)
    gl_block = (6 * d + 2 * N_KV_GROUPS * HEAD_DIM) // LANES

    ovt, expand, gexp = _static_tables(s, tq)
    biasn, biasc = _bias_tables(rel_bias, s, tq)

    row = lambda v: v.reshape(1, -1)

    xf = x.reshape(m, d)
    for l in range(depth):
        p = norm_matmul(xf, row(norm_mix_g[l]), w_in_pt, l, tm=512)
        p3 = p.reshape(b, s, n_p)
        conv = conv_module(p3, conv_dw_w[l], row(conv_dw_b[l]), row(conv_ln_g[l]), row(conv_ln_b[l]),
                           conv_pw_w[l].astype(BF16), t=512)
        pos4 = jnp.tile(cmp_pos[l], (1, 1, N_KV_GROUPS))
        kvc = compress(p3, cmp_block, pos4, cmp_w1[l].astype(BF16), cmp_w2[l].astype(BF16))
        nsa = nsa_attention(p3, kvc, gexp, biasn, biasc, ovt, expand, tq=tq,
                            q_block=q_block, kv_block=kv_block, gl_block=gl_block)
        xf = merge_out(p, conv.reshape(m, d), nsa.reshape(m, d), w_out[l].astype(BF16), xf,
                       tm=512, gc_block=gc_block, ga_block=ga_block)
        kvx = matmul(mem.reshape(b * mlen, d), xkv_w[l].astype(BF16), tm=mlen)
        xf = cross_attention_block(xf.reshape(b, s, d), row(norm_x_g[l]), xq_w[l].astype(BF16),
                                   kvx.reshape(b, mlen, 2 * d), xo_w[l].astype(BF16), tm=512).reshape(m, d)
        xf = ffn_block(xf, row(norm_ffn_g[l]), ffn_in_w[l].astype(BF16), ffn_out_w[l].astype(BF16),
                       row(final_norm_g), tm=512, norm_out=(l == depth - 1))
    return xf.reshape(b, s, d)
```

```python
import functools
import math

import numpy as np
import jax
import jax.numpy as jnp
from jax import lax
from jax.experimental import pallas as pl
from jax.experimental.pallas import tpu as pltpu

F32 = jnp.float32
BF16 = jnp.bfloat16

HEAD_DIM = 64
N_KV_GROUPS = 4
HEADS_PER_GROUP = 4
N_HEADS = N_KV_GROUPS * HEADS_PER_GROUP
GROUP_WIDTH = HEADS_PER_GROUP * HEAD_DIM
CMP_BLOCK = 32
CMP_STRIDE = 16
SEL_BLOCK = 64
SEL_TOP_N = 16
WINDOW = 512
CONV_WIDTH = 31
X_HEADS = 4
REL_BUCKETS = 32
REL_MAX_DIST = 128
EPS = 1e-6
NEG = -1e30
MAX_FLOOR = -1e20
LOG2E = math.log2(math.e)

LANES = 128
SUBLANES = 8
CONV_ROWS = 64
CONV_HALO = 32
N_CMP_PAD = 128
VMEM_LIMIT = 56 * 1024 * 1024

ROW_TILE = 512
NSA_Q_TILE = WINDOW // 2


def _sigmoid(x):
    return 0.5 * jnp.tanh(0.5 * x) + 0.5


def _dot(a, b):
    return jnp.dot(a, b, preferred_element_type=F32)


def _dot_nt(a, b):
    return lax.dot_general(a, b, (((1,), (1,)), ((), ())), preferred_element_type=F32)


def _rms_norm(x, g):
    return x * lax.rsqrt(jnp.mean(x * x, axis=-1, keepdims=True) + EPS) * g


def _params(n_axes):
    return pltpu.CompilerParams(dimension_semantics=("arbitrary",) * n_axes,
                                vmem_limit_bytes=VMEM_LIMIT)


def _norm_matmul_kernel(x_ref, g_ref, wt_ref, o_ref):
    xn = _rms_norm(x_ref[...], g_ref[...]).astype(BF16)
    o_ref[...] = _dot_nt(xn, wt_ref[...]).astype(o_ref.dtype)


def norm_matmul(x, g, wt, layer, *, tm):
    m, k = x.shape
    n = wt.shape[1]
    return pl.pallas_call(
        _norm_matmul_kernel,
        grid=(m // tm,),
        in_specs=[pl.BlockSpec((tm, k), lambda i: (i, 0)),
                  pl.BlockSpec((1, k), lambda i: (0, 0)),
                  pl.BlockSpec((None, n, k), lambda i: (layer, 0, 0), pipeline_mode=pl.Buffered(1))],
        out_specs=pl.BlockSpec((tm, n), lambda i: (i, 0)),
        out_shape=jax.ShapeDtypeStruct((m, n), BF16),
        compiler_params=_params(1),
        name="norm_matmul",
    )(x, g, wt)


def _matmul_kernel(x_ref, w_ref, o_ref):
    o_ref[...] = _dot(x_ref[...].astype(BF16), w_ref[...]).astype(o_ref.dtype)


def matmul(x, w, *, tm):
    m, k = x.shape
    n = w.shape[1]
    return pl.pallas_call(
        _matmul_kernel,
        grid=(m // tm,),
        in_specs=[pl.BlockSpec((tm, k), lambda i: (i, 0)),
                  pl.BlockSpec((k, n), lambda i: (0, 0))],
        out_specs=pl.BlockSpec((tm, n), lambda i: (i, 0)),
        out_shape=jax.ShapeDtypeStruct((m, n), BF16),
        compiler_params=_params(1),
        name="matmul",
    )(x, w)


def _conv_kernel(a_ref, gt_ref, dww_ref, dwb_ref, lng_ref, lnb_ref, pw_ref, o_ref, buf_ref, cbuf_ref, *, t):
    s = pl.program_id(1)

    @pl.when(s == 0)
    def _():
        buf_ref[0:CONV_HALO, :] = jnp.zeros((CONV_HALO, buf_ref.shape[1]), F32)

    @pl.when(s > 0)
    def _():
        buf_ref[0:CONV_HALO, :] = buf_ref[t:t + CONV_HALO, :]

    a = a_ref[0].astype(F32)
    gt = gt_ref[0].astype(F32)
    buf_ref[CONV_HALO:CONV_HALO + t, :] = a * _sigmoid(gt)

    base = CONV_HALO - (CONV_WIDTH - 1)

    def conv_rows(ci, carry):
        r0 = pl.multiple_of(ci * CONV_ROWS, CONV_ROWS)
        for c in range(buf_ref.shape[1] // LANES):
            lanes = slice(c * LANES, (c + 1) * LANES)
            acc = dwb_ref[:, lanes]
            for b in range(SUBLANES):
                n = CONV_ROWS + (SUBLANES if b else 0)
                part = None
                for k in range(CONV_WIDTH):
                    if (base + k) % SUBLANES != b:
                        continue
                    term = dww_ref[k:k + 1, lanes] * buf_ref[pl.ds(r0 + (base + k - b), n), lanes]
                    part = term if part is None else part + term
                acc = acc + part[b:b + CONV_ROWS]
            cbuf_ref[pl.ds(r0, CONV_ROWS), lanes] = acc
        return carry

    lax.fori_loop(0, t // CONV_ROWS, conv_rows, 0)
    v = cbuf_ref[...]

    mu = jnp.mean(v, axis=-1, keepdims=True)
    vc = v - mu
    y = vc * lax.rsqrt(jnp.mean(vc * vc, axis=-1, keepdims=True) + EPS)
    y = y * lng_ref[...] + lnb_ref[...]
    y = y * _sigmoid(y)
    o_ref[0] = _dot(y.astype(BF16), pw_ref[...]).astype(o_ref.dtype)


def conv_module(p, dww, dwb, lng, lnb, pw, *, t):
    b, s, _ = p.shape
    c = pw.shape[0]
    vec = lambda: pl.BlockSpec((1, c), lambda bi, si: (0, 0))
    return pl.pallas_call(
        functools.partial(_conv_kernel, t=t),
        grid=(b, s // t),
        in_specs=[pl.BlockSpec((1, t, c), lambda bi, si: (bi, si, 0)),
                  pl.BlockSpec((1, t, c), lambda bi, si: (bi, si, 1)),
                  pl.BlockSpec((CONV_WIDTH, c), lambda bi, si: (0, 0)),
                  vec(), vec(), vec(),
                  pl.BlockSpec((c, c), lambda bi, si: (0, 0))],
        out_specs=pl.BlockSpec((1, t, c), lambda bi, si: (bi, si, 0)),
        out_shape=jax.ShapeDtypeStruct((b, s, c), BF16),
        scratch_shapes=[pltpu.VMEM((CONV_HALO + t, c), F32), pltpu.VMEM((t, c), F32)],
        compiler_params=_params(2),
        name="conv_module",
    )(p, p, dww, dwb, lng, lnb, pw)


def _compress_kernel(x_ref, pos_ref, w1s_ref, w2s_ref, o_ref, xf_ref, w1_ref, w2_ref):
    half = CMP_BLOCK // 2
    gw = N_KV_GROUPS * HEAD_DIM
    n_chunk = gw // LANES

    @pl.when(pl.program_id(0) == 0)
    def _():
        w1_ref[...] = jnp.zeros(w1_ref.shape, BF16)
        w2_ref[...] = jnp.zeros(w2_ref.shape, BF16)
        for g in range(N_KV_GROUPS):
            blk = slice(g * HEAD_DIM, (g + 1) * HEAD_DIM)
            w1_ref[:, :, blk, blk] = w1s_ref[...]
            w2_ref[:, blk, blk] = w2s_ref[...]

    for c in range(2 * n_chunk):
        xf_ref[c] = x_ref[0, :, c * LANES:(c + 1) * LANES].astype(F32)
    for kv in range(2):
        acc_a = jnp.zeros((N_CMP_PAD, gw), F32)
        acc_b = jnp.zeros((N_CMP_PAD, gw), F32)
        for l in range(half):
            rows = jnp.concatenate(
                [xf_ref[kv * n_chunk + c, pl.ds(l, N_CMP_PAD, stride=CMP_STRIDE), :] for c in range(n_chunk)],
                axis=1)
            ra = (rows + pos_ref[kv, l:l + 1, :]).astype(BF16)
            rb = (rows + pos_ref[kv, half + l:half + l + 1, :]).astype(BF16)
            acc_a = acc_a + _dot(ra, w1_ref[kv, l])
            acc_b = acc_b + _dot(rb, w1_ref[kv, half + l])
        z = acc_a + pltpu.roll(acc_b, N_CMP_PAD - 1, 0)
        h = z * _sigmoid(z)
        out = _dot(h.astype(BF16), w2_ref[kv]).astype(o_ref.dtype)
        for g in range(N_KV_GROUPS):
            o_ref[0, g, :, (1 - kv) * HEAD_DIM:(2 - kv) * HEAD_DIM] = out[:, g * HEAD_DIM:(g + 1) * HEAD_DIM]


def compress(p, col_block, pos4, w1, w2):
    b, s, _ = p.shape
    assert s == N_CMP_PAD * CMP_STRIDE
    gw = N_KV_GROUPS * HEAD_DIM
    return pl.pallas_call(
        _compress_kernel,
        grid=(b,),
        in_specs=[pl.BlockSpec((1, s, 2 * gw), lambda bi: (bi, 0, col_block)),
                  pl.BlockSpec(pos4.shape, lambda bi: (0, 0, 0)),
                  pl.BlockSpec(w1.shape, lambda bi: (0, 0, 0, 0)),
                  pl.BlockSpec(w2.shape, lambda bi: (0, 0, 0))],
        out_specs=pl.BlockSpec((1, N_KV_GROUPS, N_CMP_PAD, 2 * HEAD_DIM), lambda bi: (bi, 0, 0, 0)),
        out_shape=jax.ShapeDtypeStruct((b, N_KV_GROUPS, N_CMP_PAD, 2 * HEAD_DIM), BF16),
        scratch_shapes=[pltpu.VMEM((2 * gw // LANES, s, LANES), F32),
                        pltpu.VMEM((2, CMP_BLOCK, gw, gw), BF16),
                        pltpu.VMEM((2, gw, gw), BF16)],
        compiler_params=_params(1),
        name="nsa_compress",
    )(p, pos4, w1, w2)


def _nsa_kernel(q_ref, kv_ref, kvc_ref, gl_ref, gexp_ref, biasn_ref, biasc_ref, ovt_ref, kmask_ref,
                qn_ref, gln_ref, biascn_ref, o_ref, qsa_ref, kaug_ref, vaug_ref, sbuf_ref, wbuf_ref, sig_ref, part_ref,
                *, tq):
    qi = pl.program_id(2)
    r = HEADS_PER_GROUP
    rows = r * tq
    hd = HEAD_DIM
    n_lt = tq // LANES

    def prepare(q, gl, bc, t_base, slot):
        qs = jnp.concatenate([q[:, i * hd:(i + 1) * hd] for i in range(r)], axis=0)

        sig_ref[slot] = _sigmoid(_dot(gl, gexp_ref[0]))

        kvc = kvc_ref[0, 0]
        s = _dot_nt(qs, kvc[:, hd:]) + bc.reshape(rows, N_CMP_PAD)
        m = jnp.maximum(jnp.max(s, axis=-1, keepdims=True), MAX_FLOOR)
        p = jnp.exp2(s - m)
        p = p / jnp.maximum(jnp.sum(p, axis=-1, keepdims=True), 1e-30)
        part_ref[slot] = _dot(p.astype(BF16), kvc)

        ps = p[0:tq] + p[tq:2 * tq] + p[2 * tq:3 * tq] + p[3 * tq:4 * tq]
        ps_hi = ps.astype(BF16)
        ps_lo = (ps - ps_hi.astype(F32)).astype(BF16)
        ovt = ovt_ref[...]
        n_sel = ovt.shape[0]
        imp_t = _dot_nt(ovt, ps_hi) + _dot_nt(ovt, ps_lo)
        mi = lax.broadcasted_iota(jnp.int32, (n_sel, tq), 0)
        blk = (t_base + lax.broadcasted_iota(jnp.int32, (n_sel, tq), 1)) >> int(math.log2(SEL_BLOCK))
        forced = (mi == 0) | (mi == blk) | (mi == blk - 1)
        score = jnp.where(forced, jnp.inf, jnp.where(mi <= blk, imp_t, -jnp.inf))
        n_grp = n_sel // SUBLANES
        grp = [score[SUBLANES * a:SUBLANES * (a + 1)] for a in range(n_grp)]
        cnt = [jnp.zeros((SUBLANES, tq), F32) for _ in range(n_grp)]
        for mp in range(n_sel):
            other = score[mp:mp + 1, :]
            for a in range(n_grp):
                if SUBLANES * a > mp:
                    ahead = other >= grp[a]
                elif SUBLANES * (a + 1) - 1 <= mp:
                    ahead = other > grp[a]
                else:
                    later = (lax.broadcasted_iota(jnp.int32, (SUBLANES, tq), 0) + SUBLANES * a) > mp
                    ahead = (other > grp[a]) | ((other == grp[a]) & later)
                cnt[a] = cnt[a] + jnp.where(ahead, 1.0, 0.0)
        cnt = jnp.concatenate(cnt, axis=0)
        drop_t = jnp.where(cnt < float(min(SEL_TOP_N, n_sel)), 0.0, 1.0)
        drop_t = jnp.concatenate([jnp.zeros((hd, tq), F32), drop_t,
                                  jnp.zeros((LANES - hd - n_sel, tq), F32)], axis=0)
        drop = drop_t.T.astype(BF16)[:, hd:]
        qsa_ref[slot] = jnp.concatenate(
            [jnp.concatenate([q[:, i * hd:(i + 1) * hd], drop], axis=1) for i in range(r)], axis=0)

    def gate(j, slot):
        sig = sig_ref[slot]
        return jnp.concatenate(
            [jnp.broadcast_to(sig[:, j * r + i:j * r + i + 1], (tq, LANES)) for i in range(r)],
            axis=0)

    @pl.when(qi == 0)
    def _():
        ones = jnp.ones((kv_ref.shape[1], hd), BF16)
        vaug_ref[...] = jnp.concatenate(
            [kv_ref[0, :, hd:2 * hd], ones, kv_ref[0, :, 3 * hd:4 * hd], ones], axis=1)
        kaug_ref[...] = jnp.concatenate([kv_ref[0, :, 0:hd], kmask_ref[...]], axis=1)
        prepare(q_ref[0], gl_ref[0], biasc_ref[0, :, 0], 0, 0)


    def lane_max(sc):
        mx = sc[:, 0:LANES]
        for c in range(1, n_lt):
            mx = jnp.maximum(mx, sc[:, c * LANES:(c + 1) * LANES])
        return mx

    def probs(sc, m_rep):
        return jnp.concatenate([jnp.exp2(sc[:, c * LANES:(c + 1) * LANES] - m_rep)
                                for c in range(n_lt)], axis=1).astype(BF16)

    def row_max(mx):
        return jnp.broadcast_to(jnp.max(mx, axis=-1, keepdims=True), (rows, LANES))

    ri = lax.broadcasted_iota(jnp.int32, (rows, tq), 0) & (tq - 1)
    ci = lax.broadcasted_iota(jnp.int32, (rows, tq), 1)

    n_tiles = kv_ref.shape[1] // tq

    def attend(n):
        def branch():
            tile = lambda j: slice(j * tq, (j + 1) * tq)
            slot = n % 2
            if n + 1 < n_tiles:
                prepare(qn_ref[0], gln_ref[0], biascn_ref[0, :, 0], (n + 1) * tq, 1 - slot)
            mx_s = None
            for j in range(n + 1):
                sc = _dot_nt(qsa_ref[slot], kaug_ref[tile(j), :])
                if j == n:
                    sc = sc + biasn_ref[0, 0]
                elif j == n - 1:
                    sc = sc + biasn_ref[0, 1]
                sbuf_ref[:, tile(j)] = sc
                mx_s = lane_max(sc) if mx_s is None else jnp.maximum(mx_s, lane_max(sc))
            win_tiles = [j for j in (n - 2, n - 1, n) if j >= 0]
            mx_w = None
            for w, j in enumerate(win_tiles):
                if j == n:
                    add = biasn_ref[0, 0]
                elif j == n - 1:
                    add = biasn_ref[0, 1]
                else:
                    add = jnp.where(ci > ri, 0.0, NEG)
                sc = _dot_nt(qsa_ref[slot, :, 0:hd], kv_ref[0, tile(j), 2 * hd:3 * hd]) + add
                wbuf_ref[:, tile(w)] = sc
                mx_w = lane_max(sc) if mx_w is None else jnp.maximum(mx_w, lane_max(sc))
            m_sel = row_max(mx_s)
            m_win = row_max(mx_w)
            acc_w = None
            for w, j in enumerate(win_tiles):
                d = _dot(probs(wbuf_ref[:, tile(w)], m_win), vaug_ref[tile(j), LANES:2 * LANES])
                acc_w = d if acc_w is None else acc_w + d
            acc_s = None
            for j in range(n + 1):
                d = _dot(probs(sbuf_ref[:, tile(j)], m_sel), vaug_ref[tile(j), 0:LANES])
                acc_s = d if acc_s is None else acc_s + d
            o = (gate(0, slot) * part_ref[slot] + gate(2, slot) * acc_w / pltpu.roll(acc_w, hd, 1)
                 + gate(1, slot) * acc_s / pltpu.roll(acc_s, hd, 1))
            o_ref[0] = jnp.concatenate([o[i * tq:(i + 1) * tq, :hd] for i in range(r)],
                                       axis=1).astype(o_ref.dtype)
        return branch

    lax.switch(qi, [attend(n) for n in range(n_tiles)])


def nsa_attention(p, kvc, gexp, biasn, biasc, ovt, expand, *, tq, q_block, kv_block, gl_block):
    b, s, _ = p.shape
    g = N_KV_GROUPS
    nq = s // tq
    rows = HEADS_PER_GROUP * tq
    assert WINDOW == 2 * tq and tq % LANES == 0
    return pl.pallas_call(
        functools.partial(_nsa_kernel, tq=tq),
        grid=(b, g, nq),
        in_specs=[pl.BlockSpec((1, tq, GROUP_WIDTH), lambda bi, gi, qi: (bi, qi, q_block + gi)),
                  pl.BlockSpec((1, s, GROUP_WIDTH), lambda bi, gi, qi: (bi, 0, kv_block + gi)),
                  pl.BlockSpec((1, 1, N_CMP_PAD, 2 * HEAD_DIM), lambda bi, gi, qi: (bi, gi, 0, 0)),
                  pl.BlockSpec((1, tq, LANES), lambda bi, gi, qi: (bi, qi, gl_block)),
                  pl.BlockSpec((1,) + gexp.shape[1:], lambda bi, gi, qi: (gi, 0, 0)),
                  pl.BlockSpec((1, 2, rows, tq), lambda bi, gi, qi: (gi, 0, 0, 0)),
                  pl.BlockSpec((1, HEADS_PER_GROUP, 1, tq, N_CMP_PAD), lambda bi, gi, qi: (gi, 0, qi, 0, 0)),
                  pl.BlockSpec(ovt.shape, lambda bi, gi, qi: (0, 0)),
                  pl.BlockSpec(expand.shape, lambda bi, gi, qi: (0, 0)),
                  pl.BlockSpec((1, tq, GROUP_WIDTH),
                               lambda bi, gi, qi: (bi, jnp.minimum(qi + 1, nq - 1), q_block + gi)),
                  pl.BlockSpec((1, tq, LANES), lambda bi, gi, qi: (bi, jnp.minimum(qi + 1, nq - 1), gl_block)),
                  pl.BlockSpec((1, HEADS_PER_GROUP, 1, tq, N_CMP_PAD),
                               lambda bi, gi, qi: (gi, 0, jnp.minimum(qi + 1, nq - 1), 0, 0))],
        out_specs=pl.BlockSpec((1, tq, GROUP_WIDTH), lambda bi, gi, qi: (bi, qi, gi)),
        out_shape=jax.ShapeDtypeStruct((b, s, g * GROUP_WIDTH), BF16),
        scratch_shapes=[pltpu.VMEM((2, rows, LANES), BF16),
                        pltpu.VMEM((s, LANES), BF16),
                        pltpu.VMEM((s, 2 * LANES), BF16),
                        pltpu.VMEM((rows, s), F32),
                        pltpu.VMEM((rows, 3 * tq), F32),
                        pltpu.VMEM((2, tq, LANES), F32),
                        pltpu.VMEM((2, rows, LANES), F32)],
        compiler_params=_params(3),
        name="nsa_attention",
    )(p, p, kvc, p, gexp, biasn, biasc, ovt, expand, p, p, biasc)


def _merge_out_kernel(gc_ref, ga_ref, conv_ref, nsa_ref, w_ref, x_ref, o_ref):
    y = _sigmoid(gc_ref[...]) * conv_ref[...] + _sigmoid(ga_ref[...]) * nsa_ref[...]
    o_ref[...] = x_ref[...] + _dot(y, w_ref[...])


def merge_out(p2, conv, nsa, w, x, *, tm, gc_block, ga_block):
    m, d = x.shape
    tile = lambda blk: pl.BlockSpec((tm, d), lambda i: (i, blk))
    return pl.pallas_call(
        _merge_out_kernel,
        grid=(m // tm,),
        in_specs=[tile(gc_block), tile(ga_block), tile(0), tile(0),
                  pl.BlockSpec((d, d), lambda i: (0, 0)), tile(0)],
        out_specs=tile(0),
        out_shape=jax.ShapeDtypeStruct((m, d), F32),
        compiler_params=_params(1),
        name="merge_out",
    )(p2, p2, conv, nsa, w, x)


def _xattn_kernel(x_ref, g_ref, wq_ref, kv_ref, wo_ref, o_ref):
    x = x_ref[0]
    d = x.shape[-1]
    hd = d // X_HEADS
    h = _rms_norm(x, g_ref[...]).astype(BF16)
    q = (_dot(h, wq_ref[...]) * (1.0 / math.sqrt(hd))).astype(BF16)
    kv = kv_ref[0]
    outs = []
    for i in range(X_HEADS):
        s = _dot_nt(q[:, i * hd:(i + 1) * hd], kv[:, i * hd:(i + 1) * hd])
        p = jnp.exp(s - jnp.max(s, axis=-1, keepdims=True))
        l = jnp.sum(p, axis=-1, keepdims=True)
        outs.append(_dot(p.astype(BF16), kv[:, d + i * hd:d + (i + 1) * hd]) / l)
    o = jnp.concatenate(outs, axis=1).astype(BF16)
    o_ref[0] = x + _dot(o, wo_ref[...])


def cross_attention_block(x, g, wq, kv, wo, *, tm):
    b, s, d = x.shape
    mlen = kv.shape[1]
    return pl.pallas_call(
        _xattn_kernel,
        grid=(b, s // tm),
        in_specs=[pl.BlockSpec((1, tm, d), lambda bi, si: (bi, si, 0)),
                  pl.BlockSpec((1, d), lambda bi, si: (0, 0)),
                  pl.BlockSpec((d, d), lambda bi, si: (0, 0)),
                  pl.BlockSpec((1, mlen, 2 * d), lambda bi, si: (bi, 0, 0)),
                  pl.BlockSpec((d, d), lambda bi, si: (0, 0))],
        out_specs=pl.BlockSpec((1, tm, d), lambda bi, si: (bi, si, 0)),
        out_shape=jax.ShapeDtypeStruct((b, s, d), F32),
        compiler_params=_params(2),
        name="cross_attention",
    )(x, g, wq, kv, wo)


def _ffn_kernel(x_ref, g_ref, wa_ref, wb_ref, wo_ref, gout_ref, o_ref, *, norm_out):
    x = x_ref[...]
    xn = _rms_norm(x, g_ref[...]).astype(BF16)
    a = _dot(xn, wa_ref[...])
    bb = _dot(xn, wb_ref[...])
    y = x + _dot((a * _sigmoid(a) * bb).astype(BF16), wo_ref[...])
    o_ref[...] = _rms_norm(y, gout_ref[...]) if norm_out else y


def ffn_block(x, g, w_in, w_out, g_out, *, tm, norm_out):
    m, d = x.shape
    f = w_out.shape[0]
    once = pl.Buffered(1)
    return pl.pallas_call(
        functools.partial(_ffn_kernel, norm_out=norm_out),
        grid=(m // tm,),
        in_specs=[pl.BlockSpec((tm, d), lambda i: (i, 0)),
                  pl.BlockSpec((1, d), lambda i: (0, 0)),
                  pl.BlockSpec((d, f), lambda i: (0, 0), pipeline_mode=once),
                  pl.BlockSpec((d, f), lambda i: (0, 1), pipeline_mode=once),
                  pl.BlockSpec((f, d), lambda i: (0, 0), pipeline_mode=once),
                  pl.BlockSpec((1, d), lambda i: (0, 0))],
        out_specs=pl.BlockSpec((tm, d), lambda i: (i, 0)),
        out_shape=jax.ShapeDtypeStruct((m, d), F32),
        compiler_params=_params(1),
        name="ffn_block",
    )(x, g, w_in, w_in, w_out, g_out)


def _t5_bucket(dist):
    n = jnp.maximum(dist, 0)
    max_exact = REL_BUCKETS // 2
    nf = jnp.maximum(n, 1).astype(F32)
    large = max_exact + (jnp.log(nf / max_exact) / math.log(REL_MAX_DIST / max_exact)
                         * (REL_BUCKETS - max_exact)).astype(jnp.int32)
    large = jnp.minimum(large, REL_BUCKETS - 1)
    return jnp.where(n < max_exact, n, large)


def _bias_lookup(rel_bias, dist):
    onehot = jax.nn.one_hot(_t5_bucket(dist), REL_BUCKETS, dtype=F32)
    return jnp.einsum("bh,...b->h...", rel_bias, onehot, precision=lax.Precision.HIGHEST)


def _bias_tables(rel_bias, s, tq):
    g, r = N_KV_GROUPS, HEADS_PER_GROUP
    nq = s // tq
    rel_bias = rel_bias * LOG2E
    far = _bias_lookup(rel_bias, jnp.asarray(s, jnp.int32))
    i = jnp.arange(tq)[:, None]
    j = jnp.arange(tq)[None, :]

    def near(dist):
        bt = _bias_lookup(rel_bias, dist) - far[:, None, None]
        bt = jnp.where((dist >= 0)[None], bt, NEG)
        return bt.reshape(g, r * tq, tq)

    biasn = jnp.stack([near(i - j), near(i - j + tq)], axis=1)

    t = jnp.arange(s)[:, None]
    c_end = jnp.arange(N_CMP_PAD)[None, :] * CMP_STRIDE + CMP_BLOCK - 1
    bc = _bias_lookup(rel_bias, t - c_end) + jnp.where(t >= c_end, 0.0, NEG)[None]
    biasc = bc.reshape(g, r, nq, tq, N_CMP_PAD)
    return biasn, biasc


def _static_tables(s, tq):
    n_sel = s // SEL_BLOCK
    jj = np.arange(N_CMP_PAD)[None, :] * CMP_STRIDE
    mm0 = np.arange(n_sel)[:, None] * SEL_BLOCK
    n_cmp = (s - CMP_BLOCK) // CMP_STRIDE + 1
    ovt = ((jj < mm0 + SEL_BLOCK) & (jj + CMP_BLOCK > mm0) & (np.arange(N_CMP_PAD)[None, :] < n_cmp))
    assert n_sel <= HEAD_DIM
    expand = np.zeros((s, HEAD_DIM), np.float32)
    expand[np.arange(s), np.arange(s) // SEL_BLOCK] = NEG
    gexp = np.zeros((N_KV_GROUPS, LANES, LANES), np.float32)
    for g in range(N_KV_GROUPS):
        for j in range(3):
            for i in range(HEADS_PER_GROUP):
                gexp[g, j * N_HEADS + g * HEADS_PER_GROUP + i, j * HEADS_PER_GROUP + i] = 1.0
    d = np.arange(LANES + 1, s + 1).astype(np.float32)
    big = 16 + (np.log(d / 16) / math.log(REL_MAX_DIST / 16) * 16).astype(np.int32)
    assert np.all(np.minimum(big, REL_BUCKETS - 1) == REL_BUCKETS - 1)
    return (jnp.asarray(ovt.astype(np.float32), BF16), jnp.asarray(expand, BF16), jnp.asarray(gexp, BF16))


def _in_proj_perm(d):
    g, hd = N_KV_GROUPS, HEAD_DIM
    kvw = g * hd
    o_conv, o_q = 0, 2 * d
    o_kc = o_q + N_HEADS * hd
    o_vc, o_ks, o_vs, o_kw, o_vw = (o_kc + i * kvw for i in range(1, 6))
    o_gate = o_kc + 6 * kvw
    o_gc = o_gate + 3 * N_HEADS
    o_ga = o_gc + d
    segs = [(o_conv, 2 * d, None), (o_q, N_HEADS * hd, LOG2E / math.sqrt(hd))]
    for gi in range(g):
        segs += [(base + gi * hd, hd, None) for base in (o_ks, o_vs, o_kw, o_vw)]
    segs += [(o_gc, 2 * d, None), (o_kc, 2 * kvw, None), (o_gate, 3 * N_HEADS, None)]
    n_used = sum(n for _, n, _ in segs)
    return segs, (-n_used) % LANES


def _permute_in_proj(w_in):
    segs, n_pad = _in_proj_perm(w_in.shape[1])
    wt = jnp.swapaxes(w_in, 1, 2)
    parts = [(wt[:, a:a + n] if sc is None else wt[:, a:a + n] * sc).astype(BF16) for a, n, sc in segs]
    parts.append(jnp.zeros((wt.shape[0], n_pad, wt.shape[2]), BF16))
    return jnp.concatenate(parts, axis=1)


def kernel(x, mem, norm_mix_g, w_in, conv_dw_w, conv_dw_b, conv_ln_g, conv_ln_b, conv_pw_w, cmp_pos,
           cmp_w1, cmp_w2, w_out, norm_x_g, xq_w, xkv_w, xo_w, norm_ffn_g, ffn_in_w, ffn_out_w,
           rel_bias, final_norm_g):
    b, s, d = x.shape
    depth = w_in.shape[0]
    mlen = mem.shape[1]
    m = b * s
    tq = NSA_Q_TILE

    w_in_pt = _permute_in_proj(w_in)
    n_p = w_in_pt.shape[1]
    q_block = (2 * d) // GROUP_WIDTH
    kv_block = q_block + N_KV_GROUPS
    gc_block = (3 * d + N_KV_GROUPS * GROUP_WIDTH) // d
    ga_block = gc_block + 1
    cmp_block = (6 * d) // (2 * N_KV_GROUPS * HEAD_DIM)
    gl_block = (6 * d + 2 * N_KV_GROUPS * HEAD_DIM) // LANES

    ovt, expand, gexp = _static_tables(s, tq)
    biasn, biasc = _bias_tables(rel_bias, s, tq)

    row = lambda v: v.reshape(1, -1)

    xf = x.reshape(m, d)
    for l in range(depth):
        p = norm_matmul(xf, row(norm_mix_g[l]), w_in_pt, l, tm=ROW_TILE)
        p3 = p.reshape(b, s, n_p)
        conv = conv_module(p3, conv_dw_w[l], row(conv_dw_b[l]), row(conv_ln_g[l]), row(conv_ln_b[l]),
                           conv_pw_w[l].astype(BF16), t=ROW_TILE)
        pos4 = jnp.tile(cmp_pos[l], (1, 1, N_KV_GROUPS))
        kvc = compress(p3, cmp_block, pos4, cmp_w1[l].astype(BF16), cmp_w2[l].astype(BF16))
        nsa = nsa_attention(p3, kvc, gexp, biasn, biasc, ovt, expand, tq=tq,
                            q_block=q_block, kv_block=kv_block, gl_block=gl_block)
        xf = merge_out(p, conv.reshape(m, d), nsa.reshape(m, d), w_out[l].astype(BF16), xf,
                       tm=ROW_TILE, gc_block=gc_block, ga_block=ga_block)
        kvx = matmul(mem.reshape(b * mlen, d), xkv_w[l].astype(BF16), tm=mlen)
        xf = cross_attention_block(xf.reshape(b, s, d), row(norm_x_g[l]), xq_w[l].astype(BF16),
                                   kvx.reshape(b, mlen, 2 * d), xo_w[l].astype(BF16),
                                   tm=ROW_TILE).reshape(m, d)
        xf = ffn_block(xf, row(norm_ffn_g[l]), ffn_in_w[l].astype(BF16), ffn_out_w[l].astype(BF16),
                       row(final_norm_g), tm=ROW_TILE, norm_out=(l == depth - 1))
    return xf.reshape(b, s, d)
```

```python
import functools
import math

import numpy as np
import jax
import jax.numpy as jnp
from jax import lax
from jax.experimental import pallas as pl
from jax.experimental.pallas import tpu as pltpu

F32 = jnp.float32
BF16 = jnp.bfloat16

HEAD_DIM = 64
N_KV_GROUPS = 4
HEADS_PER_GROUP = 4
N_HEADS = N_KV_GROUPS * HEADS_PER_GROUP
GROUP_WIDTH = HEADS_PER_GROUP * HEAD_DIM
CMP_BLOCK = 32
CMP_STRIDE = 16
SEL_BLOCK = 64
SEL_TOP_N = 16
WINDOW = 512
CONV_WIDTH = 31
X_HEADS = 4
REL_BUCKETS = 32
REL_MAX_DIST = 128
EPS = 1e-6
NEG = -1e30
MAX_FLOOR = -1e20
LOG2E = math.log2(math.e)

LANES = 128
SUBLANES = 8
CONV_ROWS = 64
CONV_HALO = 32
N_CMP_PAD = 128
VMEM_LIMIT = 56 * 1024 * 1024

ROW_TILE = 512
NSA_Q_TILE = WINDOW // 2


def _sigmoid(x):
    return 0.5 * jnp.tanh(0.5 * x) + 0.5


def _dot(a, b):
    return jnp.dot(a, b, preferred_element_type=F32)


def _dot_nt(a, b):
    return lax.dot_general(a, b, (((1,), (1,)), ((), ())), preferred_element_type=F32)


def _rms_norm(x, g):
    return x * lax.rsqrt(jnp.mean(x * x, axis=-1, keepdims=True) + EPS) * g


def _params(n_axes):
    return pltpu.CompilerParams(dimension_semantics=("arbitrary",) * n_axes,
                                vmem_limit_bytes=VMEM_LIMIT)


def _norm_matmul_kernel(x_ref, g_ref, wt_ref, o_ref):
    xn = _rms_norm(x_ref[...], g_ref[...]).astype(BF16)
    o_ref[...] = _dot_nt(xn, wt_ref[...]).astype(o_ref.dtype)


def norm_matmul(x, g, wt, layer, *, tm):
    m, k = x.shape
    n = wt.shape[1]
    return pl.pallas_call(
        _norm_matmul_kernel,
        grid=(m // tm,),
        in_specs=[pl.BlockSpec((tm, k), lambda i: (i, 0)),
                  pl.BlockSpec((1, k), lambda i: (0, 0)),
                  pl.BlockSpec((None, n, k), lambda i: (layer, 0, 0), pipeline_mode=pl.Buffered(1))],
        out_specs=pl.BlockSpec((tm, n), lambda i: (i, 0)),
        out_shape=jax.ShapeDtypeStruct((m, n), BF16),
        compiler_params=_params(1),
        name="norm_matmul",
    )(x, g, wt)


def _matmul_kernel(x_ref, w_ref, o_ref):
    o_ref[...] = _dot(x_ref[...].astype(BF16), w_ref[...]).astype(o_ref.dtype)


def matmul(x, w, *, tm):
    m, k = x.shape
    n = w.shape[1]
    return pl.pallas_call(
        _matmul_kernel,
        grid=(m // tm,),
        in_specs=[pl.BlockSpec((tm, k), lambda i: (i, 0)),
                  pl.BlockSpec((k, n), lambda i: (0, 0))],
        out_specs=pl.BlockSpec((tm, n), lambda i: (i, 0)),
        out_shape=jax.ShapeDtypeStruct((m, n), BF16),
        compiler_params=_params(1),
        name="matmul",
    )(x, w)


def _conv_kernel(a_ref, gt_ref, dww_ref, dwb_ref, lng_ref, lnb_ref, pw_ref, o_ref, buf_ref, cbuf_ref, *, t):
    s = pl.program_id(1)

    @pl.when(s == 0)
    def _():
        buf_ref[0:CONV_HALO, :] = jnp.zeros((CONV_HALO, buf_ref.shape[1]), F32)

    @pl.when(s > 0)
    def _():
        buf_ref[0:CONV_HALO, :] = buf_ref[t:t + CONV_HALO, :]

    a = a_ref[0].astype(F32)
    gt = gt_ref[0].astype(F32)
    buf_ref[CONV_HALO:CONV_HALO + t, :] = a * _sigmoid(gt)

    base = CONV_HALO - (CONV_WIDTH - 1)

    def conv_rows(ci, carry):
        r0 = pl.multiple_of(ci * CONV_ROWS, CONV_ROWS)
        for c in range(buf_ref.shape[1] // LANES):
            lanes = slice(c * LANES, (c + 1) * LANES)
            acc = dwb_ref[:, lanes]
            for b in range(SUBLANES):
                n = CONV_ROWS + (SUBLANES if b else 0)
                part = None
                for k in range(CONV_WIDTH):
                    if (base + k) % SUBLANES != b:
                        continue
                    term = dww_ref[k:k + 1, lanes] * buf_ref[pl.ds(r0 + (base + k - b), n), lanes]
                    part = term if part is None else part + term
                acc = acc + part[b:b + CONV_ROWS]
            cbuf_ref[pl.ds(r0, CONV_ROWS), lanes] = acc
        return carry

    lax.fori_loop(0, t // CONV_ROWS, conv_rows, 0)
    v = cbuf_ref[...]

    mu = jnp.mean(v, axis=-1, keepdims=True)
    vc = v - mu
    y = vc * lax.rsqrt(jnp.mean(vc * vc, axis=-1, keepdims=True) + EPS)
    y = y * lng_ref[...] + lnb_ref[...]
    y = y * _sigmoid(y)
    o_ref[0] = _dot(y.astype(BF16), pw_ref[...]).astype(o_ref.dtype)


def conv_module(p, dww, dwb, lng, lnb, pw, *, t):
    b, s, _ = p.shape
    c = pw.shape[0]
    vec = lambda: pl.BlockSpec((1, c), lambda bi, si: (0, 0))
    return pl.pallas_call(
        functools.partial(_conv_kernel, t=t),
        grid=(b, s // t),
        in_specs=[pl.BlockSpec((1, t, c), lambda bi, si: (bi, si, 0)),
                  pl.BlockSpec((1, t, c), lambda bi, si: (bi, si, 1)),
                  pl.BlockSpec((CONV_WIDTH, c), lambda bi, si: (0, 0)),
                  vec(), vec(), vec(),
                  pl.BlockSpec((c, c), lambda bi, si: (0, 0))],
        out_specs=pl.BlockSpec((1, t, c), lambda bi, si: (bi, si, 0)),
        out_shape=jax.ShapeDtypeStruct((b, s, c), BF16),
        scratch_shapes=[pltpu.VMEM((CONV_HALO + t, c), F32), pltpu.VMEM((t, c), F32)],
        compiler_params=_params(2),
        name="conv_module",
    )(p, p, dww, dwb, lng, lnb, pw)


def _compress_kernel(x_ref, pos_ref, w1s_ref, w2s_ref, o_ref, xf_ref, w1_ref, w2_ref):
    half = CMP_BLOCK // 2
    gw = N_KV_GROUPS * HEAD_DIM
    n_chunk = gw // LANES

    @pl.when(pl.program_id(0) == 0)
    def _():
        w1_ref[...] = jnp.zeros(w1_ref.shape, BF16)
        w2_ref[...] = jnp.zeros(w2_ref.shape, BF16)
        for g in range(N_KV_GROUPS):
            blk = slice(g * HEAD_DIM, (g + 1) * HEAD_DIM)
            w1_ref[:, :, blk, blk] = w1s_ref[...]
            w2_ref[:, blk, blk] = w2s_ref[...]

    for c in range(2 * n_chunk):
        xf_ref[c] = x_ref[0, :, c * LANES:(c + 1) * LANES].astype(F32)
    for kv in range(2):
        acc_a = jnp.zeros((N_CMP_PAD, gw), F32)
        acc_b = jnp.zeros((N_CMP_PAD, gw), F32)
        for l in range(half):
            rows = jnp.concatenate(
                [xf_ref[kv * n_chunk + c, pl.ds(l, N_CMP_PAD, stride=CMP_STRIDE), :] for c in range(n_chunk)],
                axis=1)
            ra = (rows + pos_ref[kv, l:l + 1, :]).astype(BF16)
            rb = (rows + pos_ref[kv, half + l:half + l + 1, :]).astype(BF16)
            acc_a = acc_a + _dot(ra, w1_ref[kv, l])
            acc_b = acc_b + _dot(rb, w1_ref[kv, half + l])
        z = acc_a + pltpu.roll(acc_b, N_CMP_PAD - 1, 0)
        h = z * _sigmoid(z)
        out = _dot(h.astype(BF16), w2_ref[kv]).astype(o_ref.dtype)
        for g in range(N_KV_GROUPS):
            o_ref[0, g, :, (1 - kv) * HEAD_DIM:(2 - kv) * HEAD_DIM] = out[:, g * HEAD_DIM:(g + 1) * HEAD_DIM]


def compress(p, col_block, pos4, w1, w2):
    b, s, _ = p.shape
    assert s == N_CMP_PAD * CMP_STRIDE
    gw = N_KV_GROUPS * HEAD_DIM
    return pl.pallas_call(
        _compress_kernel,
        grid=(b,),
        in_specs=[pl.BlockSpec((1, s, 2 * gw), lambda bi: (bi, 0, col_block)),
                  pl.BlockSpec(pos4.shape, lambda bi: (0, 0, 0)),
                  pl.BlockSpec(w1.shape, lambda bi: (0, 0, 0, 0)),
                  pl.BlockSpec(w2.shape, lambda bi: (0, 0, 0))],
        out_specs=pl.BlockSpec((1, N_KV_GROUPS, N_CMP_PAD, 2 * HEAD_DIM), lambda bi: (bi, 0, 0, 0)),
        out_shape=jax.ShapeDtypeStruct((b, N_KV_GROUPS, N_CMP_PAD, 2 * HEAD_DIM), BF16),
        scratch_shapes=[pltpu.VMEM((2 * gw // LANES, s, LANES), F32),
                        pltpu.VMEM((2, CMP_BLOCK, gw, gw), BF16),
                        pltpu.VMEM((2, gw, gw), BF16)],
        compiler_params=_params(1),
        name="nsa_compress",
    )(p, pos4, w1, w2)


def _nsa_kernel(q_ref, kv_ref, kvc_ref, gl_ref, gexp_ref, biasn_ref, biasc_ref, ovt_ref, kmask_ref,
                qn_ref, gln_ref, biascn_ref, o_ref, qsa_ref, kaug_ref, vaug_ref, sbuf_ref, wbuf_ref, sig_ref, part_ref,
                *, tq):
    qi = pl.program_id(2)
    r = HEADS_PER_GROUP
    rows = r * tq
    hd = HEAD_DIM
    n_lt = tq // LANES

    def prepare(q, gl, bc, t_base, slot):
        qs = jnp.concatenate([q[:, i * hd:(i + 1) * hd] for i in range(r)], axis=0)

        sig_ref[slot] = _sigmoid(_dot(gl, gexp_ref[0]))

        kvc = kvc_ref[0, 0]
        s = _dot_nt(qs, kvc[:, hd:]) + bc.reshape(rows, N_CMP_PAD)
        m = jnp.maximum(jnp.max(s, axis=-1, keepdims=True), MAX_FLOOR)
        p = jnp.exp2(s - m)
        p = p / jnp.maximum(jnp.sum(p, axis=-1, keepdims=True), 1e-30)
        part_ref[slot] = _dot(p.astype(BF16), kvc)

        ps = p[0:tq] + p[tq:2 * tq] + p[2 * tq:3 * tq] + p[3 * tq:4 * tq]
        ps_hi = ps.astype(BF16)
        ps_lo = (ps - ps_hi.astype(F32)).astype(BF16)
        ovt = ovt_ref[...]
        n_sel = ovt.shape[0]
        imp_t = _dot_nt(ovt, ps_hi) + _dot_nt(ovt, ps_lo)
        mi = lax.broadcasted_iota(jnp.int32, (n_sel, tq), 0)
        blk = (t_base + lax.broadcasted_iota(jnp.int32, (n_sel, tq), 1)) >> int(math.log2(SEL_BLOCK))
        forced = (mi == 0) | (mi == blk) | (mi == blk - 1)
        score = jnp.where(forced, jnp.inf, jnp.where(mi <= blk, imp_t, -jnp.inf))
        n_grp = n_sel // SUBLANES
        grp = [score[SUBLANES * a:SUBLANES * (a + 1)] for a in range(n_grp)]
        cnt = [jnp.zeros((SUBLANES, tq), F32) for _ in range(n_grp)]
        for mp in range(n_sel):
            other = score[mp:mp + 1, :]
            for a in range(n_grp):
                if SUBLANES * a > mp:
                    ahead = other >= grp[a]
                elif SUBLANES * (a + 1) - 1 <= mp:
                    ahead = other > grp[a]
                else:
                    later = (lax.broadcasted_iota(jnp.int32, (SUBLANES, tq), 0) + SUBLANES * a) > mp
                    ahead = (other > grp[a]) | ((other == grp[a]) & later)
                cnt[a] = cnt[a] + jnp.where(ahead, 1.0, 0.0)
        cnt = jnp.concatenate(cnt, axis=0)
        drop_t = jnp.where(cnt < float(min(SEL_TOP_N, n_sel)), 0.0, 1.0)
        drop_t = jnp.concatenate([jnp.zeros((hd, tq), F32), drop_t,
                                  jnp.zeros((LANES - hd - n_sel, tq), F32)], axis=0)
        drop = drop_t.T.astype(BF16)[:, hd:]
        qsa_ref[slot] = jnp.concatenate(
            [jnp.concatenate([q[:, i * hd:(i + 1) * hd], drop], axis=1) for i in range(r)], axis=0)

    def gate(j, slot):
        sig = sig_ref[slot]
        return jnp.concatenate(
            [jnp.broadcast_to(sig[:, j * r + i:j * r + i + 1], (tq, LANES)) for i in range(r)],
            axis=0)

    @pl.when(qi == 0)
    def _():
        ones = jnp.ones((kv_ref.shape[1], hd), BF16)
        vaug_ref[...] = jnp.concatenate(
            [kv_ref[0, :, hd:2 * hd], ones, kv_ref[0, :, 3 * hd:4 * hd], ones], axis=1)
        kaug_ref[...] = jnp.concatenate([kv_ref[0, :, 0:hd], kmask_ref[...]], axis=1)
        prepare(q_ref[0], gl_ref[0], biasc_ref[0, :, 0], 0, 0)


    def lane_max(sc):
        mx = sc[:, 0:LANES]
        for c in range(1, n_lt):
            mx = jnp.maximum(mx, sc[:, c * LANES:(c + 1) * LANES])
        return mx

    def probs(sc, m_rep):
        return jnp.concatenate([jnp.exp2(sc[:, c * LANES:(c + 1) * LANES] - m_rep)
                                for c in range(n_lt)], axis=1).astype(BF16)

    def row_max(mx):
        return jnp.broadcast_to(jnp.max(mx, axis=-1, keepdims=True), (rows, LANES))

    ri = lax.broadcasted_iota(jnp.int32, (rows, tq), 0) & (tq - 1)
    ci = lax.broadcasted_iota(jnp.int32, (rows, tq), 1)

    n_tiles = kv_ref.shape[1] // tq

    def attend(n):
        def branch():
            tile = lambda j: slice(j * tq, (j + 1) * tq)
            slot = n % 2
            if n + 1 < n_tiles:
                prepare(qn_ref[0], gln_ref[0], biascn_ref[0, :, 0], (n + 1) * tq, 1 - slot)
            mx_s = None
            for j in range(n + 1):
                sc = _dot_nt(qsa_ref[slot], kaug_ref[tile(j), :])
                if j == n:
                    sc = sc + biasn_ref[0, 0]
                elif j == n - 1:
                    sc = sc + biasn_ref[0, 1]
                sbuf_ref[:, tile(j)] = sc
                mx_s = lane_max(sc) if mx_s is None else jnp.maximum(mx_s, lane_max(sc))
            win_tiles = [j for j in (n - 2, n - 1, n) if j >= 0]
            mx_w = None
            for w, j in enumerate(win_tiles):
                if j == n:
                    add = biasn_ref[0, 0]
                elif j == n - 1:
                    add = biasn_ref[0, 1]
                else:
                    add = jnp.where(ci > ri, 0.0, NEG)
                sc = _dot_nt(qsa_ref[slot, :, 0:hd], kv_ref[0, tile(j), 2 * hd:3 * hd]) + add
                wbuf_ref[:, tile(w)] = sc
                mx_w = lane_max(sc) if mx_w is None else jnp.maximum(mx_w, lane_max(sc))
            m_sel = row_max(mx_s)
            m_win = row_max(mx_w)
            acc_w = None
            for w, j in enumerate(win_tiles):
                d = _dot(probs(wbuf_ref[:, tile(w)], m_win), vaug_ref[tile(j), LANES:2 * LANES])
                acc_w = d if acc_w is None else acc_w + d
            acc_s = None
            for j in range(n + 1):
                d = _dot(probs(sbuf_ref[:, tile(j)], m_sel), vaug_ref[tile(j), 0:LANES])
                acc_s = d if acc_s is None else acc_s + d
            o = (gate(0, slot) * part_ref[slot] + gate(2, slot) * acc_w / pltpu.roll(acc_w, hd, 1)
                 + gate(1, slot) * acc_s / pltpu.roll(acc_s, hd, 1))
            o_ref[0] = jnp.concatenate([o[i * tq:(i + 1) * tq, :hd] for i in range(r)],
                                       axis=1).astype(o_ref.dtype)
        return branch

    lax.switch(qi, [attend(n) for n in range(n_tiles)])


def nsa_attention(p, kvc, gexp, biasn, biasc, ovt, expand, *, tq, q_block, kv_block, gl_block):
    b, s, _ = p.shape
    g = N_KV_GROUPS
    nq = s // tq
    rows = HEADS_PER_GROUP * tq
    assert WINDOW == 2 * tq and tq % LANES == 0
    return pl.pallas_call(
        functools.partial(_nsa_kernel, tq=tq),
        grid=(b, g, nq),
        in_specs=[pl.BlockSpec((1, tq, GROUP_WIDTH), lambda bi, gi, qi: (bi, qi, q_block + gi)),
                  pl.BlockSpec((1, s, GROUP_WIDTH), lambda bi, gi, qi: (bi, 0, kv_block + gi)),
                  pl.BlockSpec((1, 1, N_CMP_PAD, 2 * HEAD_DIM), lambda bi, gi, qi: (bi, gi, 0, 0)),
                  pl.BlockSpec((1, tq, LANES), lambda bi, gi, qi: (bi, qi, gl_block)),
                  pl.BlockSpec((1,) + gexp.shape[1:], lambda bi, gi, qi: (gi, 0, 0)),
                  pl.BlockSpec((1, 2, rows, tq), lambda bi, gi, qi: (gi, 0, 0, 0)),
                  pl.BlockSpec((1, HEADS_PER_GROUP, 1, tq, N_CMP_PAD), lambda bi, gi, qi: (gi, 0, qi, 0, 0)),
                  pl.BlockSpec(ovt.shape, lambda bi, gi, qi: (0, 0)),
                  pl.BlockSpec(expand.shape, lambda bi, gi, qi: (0, 0)),
                  pl.BlockSpec((1, tq, GROUP_WIDTH),
                               lambda bi, gi, qi: (bi, jnp.minimum(qi + 1, nq - 1), q_block + gi)),
                  pl.BlockSpec((1, tq, LANES), lambda bi, gi, qi: (bi, jnp.minimum(qi + 1, nq - 1), gl_block)),
                  pl.BlockSpec((1, HEADS_PER_GROUP, 1, tq, N_CMP_PAD),
                               lambda bi, gi, qi: (gi, 0, jnp.minimum(qi + 1, nq - 1), 0, 0))],
        out_specs=pl.BlockSpec((1, tq, GROUP_WIDTH), lambda bi, gi, qi: (bi, qi, gi)),
        out_shape=jax.ShapeDtypeStruct((b, s, g * GROUP_WIDTH), BF16),
        scratch_shapes=[pltpu.VMEM((2, rows, LANES), BF16),
                        pltpu.VMEM((s, LANES), BF16),
                        pltpu.VMEM((s, 2 * LANES), BF16),
                        pltpu.VMEM((rows, s), F32),
                        pltpu.VMEM((rows, 3 * tq), F32),
                        pltpu.VMEM((2, tq, LANES), F32),
                        pltpu.VMEM((2, rows, LANES), F32)],
        compiler_params=_params(3),
        name="nsa_attention",
    )(p, p, kvc, p, gexp, biasn, biasc, ovt, expand, p, p, biasc)


def _merge_xattn_kernel(gc_ref, ga_ref, conv_ref, nsa_ref, wm_ref, x_ref, g_ref, wq_ref, kv_ref, wo_ref,
                        o_ref):
    y = (_sigmoid(gc_ref[0].astype(F32)) * conv_ref[0].astype(F32)
         + _sigmoid(ga_ref[0].astype(F32)) * nsa_ref[0].astype(F32))
    x = x_ref[0] + _dot(y.astype(BF16), wm_ref[...])
    d = x.shape[-1]
    hd = d // X_HEADS
    h = _rms_norm(x, g_ref[...]).astype(BF16)
    q = (_dot(h, wq_ref[...]) * (1.0 / math.sqrt(hd))).astype(BF16)
    kv = kv_ref[0]
    outs = []
    for i in range(X_HEADS):
        s = _dot_nt(q[:, i * hd:(i + 1) * hd], kv[:, i * hd:(i + 1) * hd])
        p = jnp.exp(s - jnp.max(s, axis=-1, keepdims=True))
        l = jnp.sum(p, axis=-1, keepdims=True)
        outs.append(_dot(p.astype(BF16), kv[:, d + i * hd:d + (i + 1) * hd]) / l)
    o = jnp.concatenate(outs, axis=1).astype(BF16)
    o_ref[0] = x + _dot(o, wo_ref[...])


def merge_xattn_block(p3, conv, nsa, wm, x, g, wq, kv, wo, *, tm, gc_block, ga_block):
    b, s, d = x.shape
    mlen = kv.shape[1]
    tile = lambda blk: pl.BlockSpec((1, tm, d), lambda bi, si: (bi, si, blk))
    mat = lambda: pl.BlockSpec((d, d), lambda bi, si: (0, 0))
    return pl.pallas_call(
        _merge_xattn_kernel,
        grid=(b, s // tm),
        in_specs=[tile(gc_block), tile(ga_block), tile(0), tile(0), mat(), tile(0),
                  pl.BlockSpec((1, d), lambda bi, si: (0, 0)),
                  mat(),
                  pl.BlockSpec((1, mlen, 2 * d), lambda bi, si: (bi, 0, 0)),
                  mat()],
        out_specs=tile(0),
        out_shape=jax.ShapeDtypeStruct((b, s, d), F32),
        compiler_params=_params(2),
        name="merge_xattn",
    )(p3, p3, conv, nsa, wm, x, g, wq, kv, wo)


def _ffn_kernel(x_ref, g_ref, wa_ref, wb_ref, wo_ref, gout_ref, o_ref, *, norm_out):
    x = x_ref[...]
    xn = _rms_norm(x, g_ref[...]).astype(BF16)
    a = _dot(xn, wa_ref[...])
    bb = _dot(xn, wb_ref[...])
    y = x + _dot((a * _sigmoid(a) * bb).astype(BF16), wo_ref[...])
    o_ref[...] = _rms_norm(y, gout_ref[...]) if norm_out else y


def ffn_block(x, g, w_in, w_out, g_out, *, tm, norm_out):
    m, d = x.shape
    f = w_out.shape[0]
    once = pl.Buffered(1)
    return pl.pallas_call(
        functools.partial(_ffn_kernel, norm_out=norm_out),
        grid=(m // tm,),
        in_specs=[pl.BlockSpec((tm, d), lambda i: (i, 0)),
                  pl.BlockSpec((1, d), lambda i: (0, 0)),
                  pl.BlockSpec((d, f), lambda i: (0, 0), pipeline_mode=once),
                  pl.BlockSpec((d, f), lambda i: (0, 1), pipeline_mode=once),
                  pl.BlockSpec((f, d), lambda i: (0, 0), pipeline_mode=once),
                  pl.BlockSpec((1, d), lambda i: (0, 0))],
        out_specs=pl.BlockSpec((tm, d), lambda i: (i, 0)),
        out_shape=jax.ShapeDtypeStruct((m, d), F32),
        compiler_params=_params(1),
        name="ffn_block",
    )(x, g, w_in, w_in, w_out, g_out)


def _t5_bucket(dist):
    n = jnp.maximum(dist, 0)
    max_exact = REL_BUCKETS // 2
    nf = jnp.maximum(n, 1).astype(F32)
    large = max_exact + (jnp.log(nf / max_exact) / math.log(REL_MAX_DIST / max_exact)
                         * (REL_BUCKETS - max_exact)).astype(jnp.int32)
    large = jnp.minimum(large, REL_BUCKETS - 1)
    return jnp.where(n < max_exact, n, large)


def _bias_lookup(rel_bias, dist):
    onehot = jax.nn.one_hot(_t5_bucket(dist), REL_BUCKETS, dtype=F32)
    return jnp.einsum("bh,...b->h...", rel_bias, onehot, precision=lax.Precision.HIGHEST)


def _bias_tables(rel_bias, s, tq):
    g, r = N_KV_GROUPS, HEADS_PER_GROUP
    nq = s // tq
    rel_bias = rel_bias * LOG2E
    far = _bias_lookup(rel_bias, jnp.asarray(s, jnp.int32))
    i = jnp.arange(tq)[:, None]
    j = jnp.arange(tq)[None, :]

    def near(dist):
        bt = _bias_lookup(rel_bias, dist) - far[:, None, None]
        bt = jnp.where((dist >= 0)[None], bt, NEG)
        return bt.reshape(g, r * tq, tq)

    biasn = jnp.stack([near(i - j), near(i - j + tq)], axis=1)

    t = jnp.arange(s)[:, None]
    c_end = jnp.arange(N_CMP_PAD)[None, :] * CMP_STRIDE + CMP_BLOCK - 1
    bc = _bias_lookup(rel_bias, t - c_end) + jnp.where(t >= c_end, 0.0, NEG)[None]
    biasc = bc.reshape(g, r, nq, tq, N_CMP_PAD)
    return biasn, biasc


def _static_tables(s, tq):
    n_sel = s // SEL_BLOCK
    jj = np.arange(N_CMP_PAD)[None, :] * CMP_STRIDE
    mm0 = np.arange(n_sel)[:, None] * SEL_BLOCK
    n_cmp = (s - CMP_BLOCK) // CMP_STRIDE + 1
    ovt = ((jj < mm0 + SEL_BLOCK) & (jj + CMP_BLOCK > mm0) & (np.arange(N_CMP_PAD)[None, :] < n_cmp))
    assert n_sel <= HEAD_DIM
    expand = np.zeros((s, HEAD_DIM), np.float32)
    expand[np.arange(s), np.arange(s) // SEL_BLOCK] = NEG
    gexp = np.zeros((N_KV_GROUPS, LANES, LANES), np.float32)
    for g in range(N_KV_GROUPS):
        for j in range(3):
            for i in range(HEADS_PER_GROUP):
                gexp[g, j * N_HEADS + g * HEADS_PER_GROUP + i, j * HEADS_PER_GROUP + i] = 1.0
    d = np.arange(LANES + 1, s + 1).astype(np.float32)
    big = 16 + (np.log(d / 16) / math.log(REL_MAX_DIST / 16) * 16).astype(np.int32)
    assert np.all(np.minimum(big, REL_BUCKETS - 1) == REL_BUCKETS - 1)
    return (jnp.asarray(ovt.astype(np.float32), BF16), jnp.asarray(expand, BF16), jnp.asarray(gexp, BF16))


def _in_proj_perm(d):
    g, hd = N_KV_GROUPS, HEAD_DIM
    kvw = g * hd
    o_conv, o_q = 0, 2 * d
    o_kc = o_q + N_HEADS * hd
    o_vc, o_ks, o_vs, o_kw, o_vw = (o_kc + i * kvw for i in range(1, 6))
    o_gate = o_kc + 6 * kvw
    o_gc = o_gate + 3 * N_HEADS
    o_ga = o_gc + d
    segs = [(o_conv, 2 * d, None), (o_q, N_HEADS * hd, LOG2E / math.sqrt(hd))]
    for gi in range(g):
        segs += [(base + gi * hd, hd, None) for base in (o_ks, o_vs, o_kw, o_vw)]
    segs += [(o_gc, 2 * d, None), (o_kc, 2 * kvw, None), (o_gate, 3 * N_HEADS, None)]
    n_used = sum(n for _, n, _ in segs)
    return segs, (-n_used) % LANES


def _permute_in_proj(w_in):
    segs, n_pad = _in_proj_perm(w_in.shape[1])
    wt = jnp.swapaxes(w_in, 1, 2)
    parts = [(wt[:, a:a + n] if sc is None else wt[:, a:a + n] * sc).astype(BF16) for a, n, sc in segs]
    parts.append(jnp.zeros((wt.shape[0], n_pad, wt.shape[2]), BF16))
    return jnp.concatenate(parts, axis=1)


def kernel(x, mem, norm_mix_g, w_in, conv_dw_w, conv_dw_b, conv_ln_g, conv_ln_b, conv_pw_w, cmp_pos,
           cmp_w1, cmp_w2, w_out, norm_x_g, xq_w, xkv_w, xo_w, norm_ffn_g, ffn_in_w, ffn_out_w,
           rel_bias, final_norm_g):
    b, s, d = x.shape
    depth = w_in.shape[0]
    mlen = mem.shape[1]
    m = b * s
    tq = NSA_Q_TILE

    w_in_pt = _permute_in_proj(w_in)
    n_p = w_in_pt.shape[1]
    q_block = (2 * d) // GROUP_WIDTH
    kv_block = q_block + N_KV_GROUPS
    gc_block = (3 * d + N_KV_GROUPS * GROUP_WIDTH) // d
    ga_block = gc_block + 1
    cmp_block = (6 * d) // (2 * N_KV_GROUPS * HEAD_DIM)
    gl_block = (6 * d + 2 * N_KV_GROUPS * HEAD_DIM) // LANES

    ovt, expand, gexp = _static_tables(s, tq)
    biasn, biasc = _bias_tables(rel_bias, s, tq)

    row = lambda v: v.reshape(1, -1)

    xf = x.reshape(m, d)
    for l in range(depth):
        p = norm_matmul(xf, row(norm_mix_g[l]), w_in_pt, l, tm=ROW_TILE)
        p3 = p.reshape(b, s, n_p)
        conv = conv_module(p3, conv_dw_w[l], row(conv_dw_b[l]), row(conv_ln_g[l]), row(conv_ln_b[l]),
                           conv_pw_w[l].astype(BF16), t=ROW_TILE)
        pos4 = jnp.tile(cmp_pos[l], (1, 1, N_KV_GROUPS))
        kvc = compress(p3, cmp_block, pos4, cmp_w1[l].astype(BF16), cmp_w2[l].astype(BF16))
        nsa = nsa_attention(p3, kvc, gexp, biasn, biasc, ovt, expand, tq=tq,
                            q_block=q_block, kv_block=kv_block, gl_block=gl_block)
        kvx = matmul(mem.reshape(b * mlen, d), xkv_w[l].astype(BF16), tm=mlen)
        xf = merge_xattn_block(p3, conv, nsa, w_out[l].astype(BF16), xf.reshape(b, s, d),
                               row(norm_x_g[l]), xq_w[l].astype(BF16), kvx.reshape(b, mlen, 2 * d),
                               xo_w[l].astype(BF16), tm=ROW_TILE, gc_block=gc_block,
                               ga_block=ga_block).reshape(m, d)
        xf = ffn_block(xf, row(norm_ffn_g[l]), ffn_in_w[l].astype(BF16), ffn_out_w[l].astype(BF16),
                       row(final_norm_g), tm=ROW_TILE, norm_out=(l == depth - 1))
    return xf.reshape(b, s, d)
```

```python
import functools
import math

import numpy as np
import jax
import jax.numpy as jnp
from jax import lax
from jax.experimental import pallas as pl
from jax.experimental.pallas import tpu as pltpu

F32 = jnp.float32
BF16 = jnp.bfloat16

HEAD_DIM = 64
N_KV_GROUPS = 4
HEADS_PER_GROUP = 4
N_HEADS = N_KV_GROUPS * HEADS_PER_GROUP
GROUP_WIDTH = HEADS_PER_GROUP * HEAD_DIM
CMP_BLOCK = 32
CMP_STRIDE = 16
SEL_BLOCK = 64
SEL_TOP_N = 16
WINDOW = 512
CONV_WIDTH = 31
X_HEADS = 4
REL_BUCKETS = 32
REL_MAX_DIST = 128
EPS = 1e-6
NEG = -1e30
MAX_FLOOR = -1e20
LOG2E = math.log2(math.e)

LANES = 128
SUBLANES = 8
CONV_ROWS = 64
CONV_HALO = 32
N_CMP_PAD = 128
VMEM_LIMIT = 56 * 1024 * 1024

ROW_TILE = 512
NSA_Q_TILE = WINDOW // 2


def _sigmoid(x):
    return 0.5 * jnp.tanh(0.5 * x) + 0.5


def _dot(a, b):
    return jnp.dot(a, b, preferred_element_type=F32)


def _dot_nt(a, b):
    return lax.dot_general(a, b, (((1,), (1,)), ((), ())), preferred_element_type=F32)


def _rms_norm(x, g):
    return x * lax.rsqrt(jnp.mean(x * x, axis=-1, keepdims=True) + EPS) * g


def _params(n_axes):
    return pltpu.CompilerParams(dimension_semantics=("arbitrary",) * n_axes,
                                vmem_limit_bytes=VMEM_LIMIT)


def _norm_matmul_kernel(x_ref, g_ref, wt_ref, o_ref):
    xn = _rms_norm(x_ref[...], g_ref[...]).astype(BF16)
    o_ref[...] = _dot_nt(xn, wt_ref[...]).astype(o_ref.dtype)


def norm_matmul(x, g, wt, layer, *, tm):
    m, k = x.shape
    n = wt.shape[1]
    return pl.pallas_call(
        _norm_matmul_kernel,
        grid=(m // tm,),
        in_specs=[pl.BlockSpec((tm, k), lambda i: (i, 0)),
                  pl.BlockSpec((1, k), lambda i: (0, 0)),
                  pl.BlockSpec((None, n, k), lambda i: (layer, 0, 0), pipeline_mode=pl.Buffered(1))],
        out_specs=pl.BlockSpec((tm, n), lambda i: (i, 0)),
        out_shape=jax.ShapeDtypeStruct((m, n), BF16),
        compiler_params=_params(1),
        name="norm_matmul",
    )(x, g, wt)


def _matmul_kernel(x_ref, w_ref, o_ref):
    o_ref[...] = _dot(x_ref[...].astype(BF16), w_ref[...]).astype(o_ref.dtype)


def matmul(x, w, layer, *, tm):
    m, k = x.shape
    n = w.shape[2]
    return pl.pallas_call(
        _matmul_kernel,
        grid=(m // tm,),
        in_specs=[pl.BlockSpec((tm, k), lambda i: (i, 0)),
                  pl.BlockSpec((None, k, n), lambda i: (layer, 0, 0))],
        out_specs=pl.BlockSpec((tm, n), lambda i: (i, 0)),
        out_shape=jax.ShapeDtypeStruct((m, n), BF16),
        compiler_params=_params(1),
        name="matmul",
    )(x, w)


def _conv_kernel(a_ref, gt_ref, dww_ref, dwb_ref, lng_ref, lnb_ref, pw_ref, o_ref, buf_ref, cbuf_ref, *, t):
    s = pl.program_id(1)

    @pl.when(s == 0)
    def _():
        buf_ref[0:CONV_HALO, :] = jnp.zeros((CONV_HALO, buf_ref.shape[1]), F32)

    @pl.when(s > 0)
    def _():
        buf_ref[0:CONV_HALO, :] = buf_ref[t:t + CONV_HALO, :]

    a = a_ref[0].astype(F32)
    gt = gt_ref[0].astype(F32)
    buf_ref[CONV_HALO:CONV_HALO + t, :] = a * _sigmoid(gt)

    base = CONV_HALO - (CONV_WIDTH - 1)

    def conv_rows(ci, carry):
        r0 = pl.multiple_of(ci * CONV_ROWS, CONV_ROWS)
        for c in range(buf_ref.shape[1] // LANES):
            lanes = slice(c * LANES, (c + 1) * LANES)
            acc = dwb_ref[:, lanes]
            for b in range(SUBLANES):
                n = CONV_ROWS + (SUBLANES if b else 0)
                part = None
                for k in range(CONV_WIDTH):
                    if (base + k) % SUBLANES != b:
                        continue
                    term = dww_ref[k:k + 1, lanes] * buf_ref[pl.ds(r0 + (base + k - b), n), lanes]
                    part = term if part is None else part + term
                acc = acc + part[b:b + CONV_ROWS]
            cbuf_ref[pl.ds(r0, CONV_ROWS), lanes] = acc
        return carry

    lax.fori_loop(0, t // CONV_ROWS, conv_rows, 0)
    v = cbuf_ref[...]

    mu = jnp.mean(v, axis=-1, keepdims=True)
    vc = v - mu
    y = vc * lax.rsqrt(jnp.mean(vc * vc, axis=-1, keepdims=True) + EPS)
    y = y * lng_ref[...] + lnb_ref[...]
    y = y * _sigmoid(y)
    o_ref[0] = _dot(y.astype(BF16), pw_ref[...]).astype(o_ref.dtype)


def conv_module(p, dww, dwb, lng, lnb, pw, layer, *, t):
    b, s, _ = p.shape
    c = pw.shape[1]
    vec = lambda: pl.BlockSpec((1, c), lambda bi, si: (0, 0))
    return pl.pallas_call(
        functools.partial(_conv_kernel, t=t),
        grid=(b, s // t),
        in_specs=[pl.BlockSpec((1, t, c), lambda bi, si: (bi, si, 0)),
                  pl.BlockSpec((1, t, c), lambda bi, si: (bi, si, 1)),
                  pl.BlockSpec((CONV_WIDTH, c), lambda bi, si: (0, 0)),
                  vec(), vec(), vec(),
                  pl.BlockSpec((None, c, c), lambda bi, si: (layer, 0, 0))],
        out_specs=pl.BlockSpec((1, t, c), lambda bi, si: (bi, si, 0)),
        out_shape=jax.ShapeDtypeStruct((b, s, c), BF16),
        scratch_shapes=[pltpu.VMEM((CONV_HALO + t, c), F32), pltpu.VMEM((t, c), F32)],
        compiler_params=_params(2),
        name="conv_module",
    )(p, p, dww, dwb, lng, lnb, pw)


def _compress_kernel(x_ref, pos_ref, w1s_ref, w2s_ref, o_ref, xf_ref, w1_ref, w2_ref):
    half = CMP_BLOCK // 2
    gw = N_KV_GROUPS * HEAD_DIM
    n_chunk = gw // LANES

    @pl.when(pl.program_id(0) == 0)
    def _():
        w1_ref[...] = jnp.zeros(w1_ref.shape, BF16)
        w2_ref[...] = jnp.zeros(w2_ref.shape, BF16)
        for g in range(N_KV_GROUPS):
            blk = slice(g * HEAD_DIM, (g + 1) * HEAD_DIM)
            w1_ref[:, :, blk, blk] = w1s_ref[...]
            w2_ref[:, blk, blk] = w2s_ref[...]

    for c in range(2 * n_chunk):
        xf_ref[c] = x_ref[0, :, c * LANES:(c + 1) * LANES].astype(F32)
    for kv in range(2):
        acc_a = jnp.zeros((N_CMP_PAD, gw), F32)
        acc_b = jnp.zeros((N_CMP_PAD, gw), F32)
        for l in range(half):
            rows = jnp.concatenate(
                [xf_ref[kv * n_chunk + c, pl.ds(l, N_CMP_PAD, stride=CMP_STRIDE), :] for c in range(n_chunk)],
                axis=1)
            ra = (rows + pos_ref[kv, l:l + 1, :]).astype(BF16)
            rb = (rows + pos_ref[kv, half + l:half + l + 1, :]).astype(BF16)
            acc_a = acc_a + _dot(ra, w1_ref[kv, l])
            acc_b = acc_b + _dot(rb, w1_ref[kv, half + l])
        z = acc_a + pltpu.roll(acc_b, N_CMP_PAD - 1, 0)
        h = z * _sigmoid(z)
        out = _dot(h.astype(BF16), w2_ref[kv]).astype(o_ref.dtype)
        for g in range(N_KV_GROUPS):
            o_ref[0, g, :, (1 - kv) * HEAD_DIM:(2 - kv) * HEAD_DIM] = out[:, g * HEAD_DIM:(g + 1) * HEAD_DIM]


def compress(p, col_block, pos4, w1, w2):
    b, s, _ = p.shape
    assert s == N_CMP_PAD * CMP_STRIDE
    gw = N_KV_GROUPS * HEAD_DIM
    return pl.pallas_call(
        _compress_kernel,
        grid=(b,),
        in_specs=[pl.BlockSpec((1, s, 2 * gw), lambda bi: (bi, 0, col_block)),
                  pl.BlockSpec(pos4.shape, lambda bi: (0, 0, 0)),
                  pl.BlockSpec(w1.shape, lambda bi: (0, 0, 0, 0)),
                  pl.BlockSpec(w2.shape, lambda bi: (0, 0, 0))],
        out_specs=pl.BlockSpec((1, N_KV_GROUPS, N_CMP_PAD, 2 * HEAD_DIM), lambda bi: (bi, 0, 0, 0)),
        out_shape=jax.ShapeDtypeStruct((b, N_KV_GROUPS, N_CMP_PAD, 2 * HEAD_DIM), BF16),
        scratch_shapes=[pltpu.VMEM((2 * gw // LANES, s, LANES), F32),
                        pltpu.VMEM((2, CMP_BLOCK, gw, gw), BF16),
                        pltpu.VMEM((2, gw, gw), BF16)],
        compiler_params=_params(1),
        name="nsa_compress",
    )(p, pos4, w1, w2)


def _nsa_kernel(q_ref, kv_ref, kvc_ref, gl_ref, gexp_ref, biasn_ref, biasc_ref, ovt_ref, kmask_ref,
                qn_ref, gln_ref, biascn_ref, o_ref, qsa_ref, kaug_ref, vaug_ref, sbuf_ref, wbuf_ref, sig_ref, part_ref,
                *, tq):
    qi = pl.program_id(2)
    r = HEADS_PER_GROUP
    rows = r * tq
    hd = HEAD_DIM
    n_lt = tq // LANES

    def prepare(q, gl, bc, t_base, slot):
        qs = jnp.concatenate([q[:, i * hd:(i + 1) * hd] for i in range(r)], axis=0)

        sig_ref[slot] = _sigmoid(_dot(gl, gexp_ref[0]))

        kvc = kvc_ref[0, 0]
        s = _dot_nt(qs, kvc[:, hd:]) + bc.reshape(rows, N_CMP_PAD)
        m = jnp.maximum(jnp.max(s, axis=-1, keepdims=True), MAX_FLOOR)
        p = jnp.exp2(s - m)
        p = p / jnp.maximum(jnp.sum(p, axis=-1, keepdims=True), 1e-30)
        part_ref[slot] = _dot(p.astype(BF16), kvc)

        ps = p[0:tq] + p[tq:2 * tq] + p[2 * tq:3 * tq] + p[3 * tq:4 * tq]
        ps_hi = ps.astype(BF16)
        ps_lo = (ps - ps_hi.astype(F32)).astype(BF16)
        ovt = ovt_ref[...]
        n_sel = ovt.shape[0]
        imp_t = _dot_nt(ovt, ps_hi) + _dot_nt(ovt, ps_lo)
        mi = lax.broadcasted_iota(jnp.int32, (n_sel, tq), 0)
        blk = (t_base + lax.broadcasted_iota(jnp.int32, (n_sel, tq), 1)) >> int(math.log2(SEL_BLOCK))
        forced = (mi == 0) | (mi == blk) | (mi == blk - 1)
        score = jnp.where(forced, jnp.inf, jnp.where(mi <= blk, imp_t, -jnp.inf))
        n_grp = n_sel // SUBLANES
        grp = [score[SUBLANES * a:SUBLANES * (a + 1)] for a in range(n_grp)]
        cnt = [jnp.zeros((SUBLANES, tq), F32) for _ in range(n_grp)]
        for mp in range(n_sel):
            other = score[mp:mp + 1, :]
            for a in range(n_grp):
                if SUBLANES * a > mp:
                    ahead = other >= grp[a]
                elif SUBLANES * (a + 1) - 1 <= mp:
                    ahead = other > grp[a]
                else:
                    later = (lax.broadcasted_iota(jnp.int32, (SUBLANES, tq), 0) + SUBLANES * a) > mp
                    ahead = (other > grp[a]) | ((other == grp[a]) & later)
                cnt[a] = cnt[a] + jnp.where(ahead, 1.0, 0.0)
        cnt = jnp.concatenate(cnt, axis=0)
        drop_t = jnp.where(cnt < float(min(SEL_TOP_N, n_sel)), 0.0, 1.0)
        drop_t = jnp.concatenate([jnp.zeros((hd, tq), F32), drop_t,
                                  jnp.zeros((LANES - hd - n_sel, tq), F32)], axis=0)
        drop = drop_t.T.astype(BF16)[:, hd:]
        qsa_ref[slot] = jnp.concatenate(
            [jnp.concatenate([q[:, i * hd:(i + 1) * hd], drop], axis=1) for i in range(r)], axis=0)

    def gate(j, slot):
        sig = sig_ref[slot]
        return jnp.concatenate(
            [jnp.broadcast_to(sig[:, j * r + i:j * r + i + 1], (tq, LANES)) for i in range(r)],
            axis=0)

    @pl.when(qi == 0)
    def _():
        ones = jnp.ones((kv_ref.shape[1], hd), BF16)
        vaug_ref[...] = jnp.concatenate(
            [kv_ref[0, :, hd:2 * hd], ones, kv_ref[0, :, 3 * hd:4 * hd], ones], axis=1)
        kaug_ref[...] = jnp.concatenate([kv_ref[0, :, 0:hd], kmask_ref[...]], axis=1)
        prepare(q_ref[0], gl_ref[0], biasc_ref[0, :, 0], 0, 0)


    def lane_max(sc):
        mx = sc[:, 0:LANES]
        for c in range(1, n_lt):
            mx = jnp.maximum(mx, sc[:, c * LANES:(c + 1) * LANES])
        return mx

    def probs(sc, m_rep):
        return jnp.concatenate([jnp.exp2(sc[:, c * LANES:(c + 1) * LANES] - m_rep)
                                for c in range(n_lt)], axis=1).astype(BF16)

    def row_max(mx):
        return jnp.broadcast_to(jnp.max(mx, axis=-1, keepdims=True), (rows, LANES))

    ri = lax.broadcasted_iota(jnp.int32, (rows, tq), 0) & (tq - 1)
    ci = lax.broadcasted_iota(jnp.int32, (rows, tq), 1)

    n_tiles = kv_ref.shape[1] // tq

    def attend(n):
        def branch():
            tile = lambda j: slice(j * tq, (j + 1) * tq)
            slot = n % 2
            if n + 1 < n_tiles:
                prepare(qn_ref[0], gln_ref[0], biascn_ref[0, :, 0], (n + 1) * tq, 1 - slot)
            mx_s = None
            for j in range(n + 1):
                sc = _dot_nt(qsa_ref[slot], kaug_ref[tile(j), :])
                if j == n:
                    sc = sc + biasn_ref[0, 0]
                elif j == n - 1:
                    sc = sc + biasn_ref[0, 1]
                sbuf_ref[:, tile(j)] = sc
                mx_s = lane_max(sc) if mx_s is None else jnp.maximum(mx_s, lane_max(sc))
            win_tiles = [j for j in (n - 2, n - 1, n) if j >= 0]
            mx_w = None
            for w, j in enumerate(win_tiles):
                if j == n:
                    add = biasn_ref[0, 0]
                elif j == n - 1:
                    add = biasn_ref[0, 1]
                else:
                    add = jnp.where(ci > ri, 0.0, NEG)
                sc = _dot_nt(qsa_ref[slot, :, 0:hd], kv_ref[0, tile(j), 2 * hd:3 * hd]) + add
                wbuf_ref[:, tile(w)] = sc
                mx_w = lane_max(sc) if mx_w is None else jnp.maximum(mx_w, lane_max(sc))
            m_sel = row_max(mx_s)
            m_win = row_max(mx_w)
            acc_w = None
            for w, j in enumerate(win_tiles):
                d = _dot(probs(wbuf_ref[:, tile(w)], m_win), vaug_ref[tile(j), LANES:2 * LANES])
                acc_w = d if acc_w is None else acc_w + d
            acc_s = None
            for j in range(n + 1):
                d = _dot(probs(sbuf_ref[:, tile(j)], m_sel), vaug_ref[tile(j), 0:LANES])
                acc_s = d if acc_s is None else acc_s + d
            o = (gate(0, slot) * part_ref[slot] + gate(2, slot) * acc_w / pltpu.roll(acc_w, hd, 1)
                 + gate(1, slot) * acc_s / pltpu.roll(acc_s, hd, 1))
            o_ref[0] = jnp.concatenate([o[i * tq:(i + 1) * tq, :hd] for i in range(r)],
                                       axis=1).astype(o_ref.dtype)
        return branch

    lax.switch(qi, [attend(n) for n in range(n_tiles)])


def nsa_attention(p, kvc, gexp, biasn, biasc, ovt, expand, *, tq, q_block, kv_block, gl_block):
    b, s, _ = p.shape
    g = N_KV_GROUPS
    nq = s // tq
    rows = HEADS_PER_GROUP * tq
    assert WINDOW == 2 * tq and tq % LANES == 0
    return pl.pallas_call(
        functools.partial(_nsa_kernel, tq=tq),
        grid=(b, g, nq),
        in_specs=[pl.BlockSpec((1, tq, GROUP_WIDTH), lambda bi, gi, qi: (bi, qi, q_block + gi)),
                  pl.BlockSpec((1, s, GROUP_WIDTH), lambda bi, gi, qi: (bi, 0, kv_block + gi)),
                  pl.BlockSpec((1, 1, N_CMP_PAD, 2 * HEAD_DIM), lambda bi, gi, qi: (bi, gi, 0, 0)),
                  pl.BlockSpec((1, tq, LANES), lambda bi, gi, qi: (bi, qi, gl_block)),
                  pl.BlockSpec((1,) + gexp.shape[1:], lambda bi, gi, qi: (gi, 0, 0)),
                  pl.BlockSpec((1, 2, rows, tq), lambda bi, gi, qi: (gi, 0, 0, 0)),
                  pl.BlockSpec((1, HEADS_PER_GROUP, 1, tq, N_CMP_PAD), lambda bi, gi, qi: (gi, 0, qi, 0, 0)),
                  pl.BlockSpec(ovt.shape, lambda bi, gi, qi: (0, 0)),
                  pl.BlockSpec(expand.shape, lambda bi, gi, qi: (0, 0)),
                  pl.BlockSpec((1, tq, GROUP_WIDTH),
                               lambda bi, gi, qi: (bi, jnp.minimum(qi + 1, nq - 1), q_block + gi)),
                  pl.BlockSpec((1, tq, LANES), lambda bi, gi, qi: (bi, jnp.minimum(qi + 1, nq - 1), gl_block)),
                  pl.BlockSpec((1, HEADS_PER_GROUP, 1, tq, N_CMP_PAD),
                               lambda bi, gi, qi: (gi, 0, jnp.minimum(qi + 1, nq - 1), 0, 0))],
        out_specs=pl.BlockSpec((1, tq, GROUP_WIDTH), lambda bi, gi, qi: (bi, qi, gi)),
        out_shape=jax.ShapeDtypeStruct((b, s, g * GROUP_WIDTH), BF16),
        scratch_shapes=[pltpu.VMEM((2, rows, LANES), BF16),
                        pltpu.VMEM((s, LANES), BF16),
                        pltpu.VMEM((s, 2 * LANES), BF16),
                        pltpu.VMEM((rows, s), F32),
                        pltpu.VMEM((rows, 3 * tq), F32),
                        pltpu.VMEM((2, tq, LANES), F32),
                        pltpu.VMEM((2, rows, LANES), F32)],
        compiler_params=_params(3),
        name="nsa_attention",
    )(p, p, kvc, p, gexp, biasn, biasc, ovt, expand, p, p, biasc)


def _merge_xattn_kernel(gc_ref, ga_ref, conv_ref, nsa_ref, wm_ref, x_ref, g_ref, wq_ref, kv_ref, wo_ref,
                        o_ref):
    y = (_sigmoid(gc_ref[0].astype(F32)) * conv_ref[0].astype(F32)
         + _sigmoid(ga_ref[0].astype(F32)) * nsa_ref[0].astype(F32))
    x = x_ref[0] + _dot(y.astype(BF16), wm_ref[...])
    d = x.shape[-1]
    hd = d // X_HEADS
    h = _rms_norm(x, g_ref[...]).astype(BF16)
    q = (_dot(h, wq_ref[...]) * (1.0 / math.sqrt(hd))).astype(BF16)
    kv = kv_ref[0]
    outs = []
    for i in range(X_HEADS):
        s = _dot_nt(q[:, i * hd:(i + 1) * hd], kv[:, i * hd:(i + 1) * hd])
        p = jnp.exp(s - jnp.max(s, axis=-1, keepdims=True))
        l = jnp.sum(p, axis=-1, keepdims=True)
        outs.append(_dot(p.astype(BF16), kv[:, d + i * hd:d + (i + 1) * hd]) / l)
    o = jnp.concatenate(outs, axis=1).astype(BF16)
    o_ref[0] = x + _dot(o, wo_ref[...])


def merge_xattn_block(p3, conv, nsa, wm, x, g, wq, kv, wo, layer, *, tm, gc_block, ga_block):
    b, s, d = x.shape
    mlen = kv.shape[1]
    tile = lambda blk: pl.BlockSpec((1, tm, d), lambda bi, si: (bi, si, blk))
    mat = lambda: pl.BlockSpec((None, d, d), lambda bi, si: (layer, 0, 0))
    return pl.pallas_call(
        _merge_xattn_kernel,
        grid=(b, s // tm),
        in_specs=[tile(gc_block), tile(ga_block), tile(0), tile(0), mat(), tile(0),
                  pl.BlockSpec((1, d), lambda bi, si: (0, 0)),
                  mat(),
                  pl.BlockSpec((1, mlen, 2 * d), lambda bi, si: (bi, 0, 0)),
                  mat()],
        out_specs=tile(0),
        out_shape=jax.ShapeDtypeStruct((b, s, d), F32),
        compiler_params=_params(2),
        name="merge_xattn",
    )(p3, p3, conv, nsa, wm, x, g, wq, kv, wo)


def _ffn_kernel(x_ref, g_ref, wa_ref, wb_ref, wo_ref, gout_ref, o_ref, *, norm_out):
    x = x_ref[...]
    xn = _rms_norm(x, g_ref[...]).astype(BF16)
    a = _dot(xn, wa_ref[...])
    bb = _dot(xn, wb_ref[...])
    y = x + _dot((a * _sigmoid(a) * bb).astype(BF16), wo_ref[...])
    o_ref[...] = _rms_norm(y, gout_ref[...]) if norm_out else y


def ffn_block(x, g, w_in, w_out, g_out, layer, *, tm, norm_out):
    m, d = x.shape
    f = w_out.shape[1]
    once = pl.Buffered(1)
    return pl.pallas_call(
        functools.partial(_ffn_kernel, norm_out=norm_out),
        grid=(m // tm,),
        in_specs=[pl.BlockSpec((tm, d), lambda i: (i, 0)),
                  pl.BlockSpec((1, d), lambda i: (0, 0)),
                  pl.BlockSpec((None, d, f), lambda i: (layer, 0, 0), pipeline_mode=once),
                  pl.BlockSpec((None, d, f), lambda i: (layer, 0, 1), pipeline_mode=once),
                  pl.BlockSpec((None, f, d), lambda i: (layer, 0, 0), pipeline_mode=once),
                  pl.BlockSpec((1, d), lambda i: (0, 0))],
        out_specs=pl.BlockSpec((tm, d), lambda i: (i, 0)),
        out_shape=jax.ShapeDtypeStruct((m, d), F32),
        compiler_params=_params(1),
        name="ffn_block",
    )(x, g, w_in, w_in, w_out, g_out)


def _t5_bucket(dist):
    n = jnp.maximum(dist, 0)
    max_exact = REL_BUCKETS // 2
    nf = jnp.maximum(n, 1).astype(F32)
    large = max_exact + (jnp.log(nf / max_exact) / math.log(REL_MAX_DIST / max_exact)
                         * (REL_BUCKETS - max_exact)).astype(jnp.int32)
    large = jnp.minimum(large, REL_BUCKETS - 1)
    return jnp.where(n < max_exact, n, large)


def _bias_lookup(rel_bias, dist):
    onehot = jax.nn.one_hot(_t5_bucket(dist), REL_BUCKETS, dtype=F32)
    return jnp.einsum("bh,...b->h...", rel_bias, onehot, precision=lax.Precision.HIGHEST)


def _bias_tables(rel_bias, s, tq):
    g, r = N_KV_GROUPS, HEADS_PER_GROUP
    nq = s // tq
    rel_bias = rel_bias * LOG2E
    far = _bias_lookup(rel_bias, jnp.asarray(s, jnp.int32))
    i = jnp.arange(tq)[:, None]
    j = jnp.arange(tq)[None, :]

    def near(dist):
        bt = _bias_lookup(rel_bias, dist) - far[:, None, None]
        bt = jnp.where((dist >= 0)[None], bt, NEG)
        return bt.reshape(g, r * tq, tq)

    biasn = jnp.stack([near(i - j), near(i - j + tq)], axis=1)

    t = jnp.arange(s)[:, None]
    c_end = jnp.arange(N_CMP_PAD)[None, :] * CMP_STRIDE + CMP_BLOCK - 1
    bc = _bias_lookup(rel_bias, t - c_end) + jnp.where(t >= c_end, 0.0, NEG)[None]
    biasc = bc.reshape(g, r, nq, tq, N_CMP_PAD)
    return biasn, biasc


def _static_tables(s, tq):
    n_sel = s // SEL_BLOCK
    jj = np.arange(N_CMP_PAD)[None, :] * CMP_STRIDE
    mm0 = np.arange(n_sel)[:, None] * SEL_BLOCK
    n_cmp = (s - CMP_BLOCK) // CMP_STRIDE + 1
    ovt = ((jj < mm0 + SEL_BLOCK) & (jj + CMP_BLOCK > mm0) & (np.arange(N_CMP_PAD)[None, :] < n_cmp))
    assert n_sel <= HEAD_DIM
    expand = np.zeros((s, HEAD_DIM), np.float32)
    expand[np.arange(s), np.arange(s) // SEL_BLOCK] = NEG
    gexp = np.zeros((N_KV_GROUPS, LANES, LANES), np.float32)
    for g in range(N_KV_GROUPS):
        for j in range(3):
            for i in range(HEADS_PER_GROUP):
                gexp[g, j * N_HEADS + g * HEADS_PER_GROUP + i, j * HEADS_PER_GROUP + i] = 1.0
    d = np.arange(LANES + 1, s + 1).astype(np.float32)
    big = 16 + (np.log(d / 16) / math.log(REL_MAX_DIST / 16) * 16).astype(np.int32)
    assert np.all(np.minimum(big, REL_BUCKETS - 1) == REL_BUCKETS - 1)
    return (jnp.asarray(ovt.astype(np.float32), BF16), jnp.asarray(expand, BF16), jnp.asarray(gexp, BF16))


def _in_proj_perm(d):
    g, hd = N_KV_GROUPS, HEAD_DIM
    kvw = g * hd
    o_conv, o_q = 0, 2 * d
    o_kc = o_q + N_HEADS * hd
    o_vc, o_ks, o_vs, o_kw, o_vw = (o_kc + i * kvw for i in range(1, 6))
    o_gate = o_kc + 6 * kvw
    o_gc = o_gate + 3 * N_HEADS
    o_ga = o_gc + d
    segs = [(o_conv, 2 * d, None), (o_q, N_HEADS * hd, LOG2E / math.sqrt(hd))]
    for gi in range(g):
        segs += [(base + gi * hd, hd, None) for base in (o_ks, o_vs, o_kw, o_vw)]
    segs += [(o_gc, 2 * d, None), (o_kc, 2 * kvw, None), (o_gate, 3 * N_HEADS, None)]
    n_used = sum(n for _, n, _ in segs)
    return segs, (-n_used) % LANES


def _permute_in_proj(w_in):
    segs, n_pad = _in_proj_perm(w_in.shape[1])
    wt = jnp.swapaxes(w_in, 1, 2)
    parts = [(wt[:, a:a + n] if sc is None else wt[:, a:a + n] * sc).astype(BF16) for a, n, sc in segs]
    parts.append(jnp.zeros((wt.shape[0], n_pad, wt.shape[2]), BF16))
    return jnp.concatenate(parts, axis=1)


def kernel(x, mem, norm_mix_g, w_in, conv_dw_w, conv_dw_b, conv_ln_g, conv_ln_b, conv_pw_w, cmp_pos,
           cmp_w1, cmp_w2, w_out, norm_x_g, xq_w, xkv_w, xo_w, norm_ffn_g, ffn_in_w, ffn_out_w,
           rel_bias, final_norm_g):
    b, s, d = x.shape
    depth = w_in.shape[0]
    mlen = mem.shape[1]
    m = b * s
    tq = NSA_Q_TILE

    w_in_pt = _permute_in_proj(w_in)
    n_p = w_in_pt.shape[1]
    q_block = (2 * d) // GROUP_WIDTH
    kv_block = q_block + N_KV_GROUPS
    gc_block = (3 * d + N_KV_GROUPS * GROUP_WIDTH) // d
    ga_block = gc_block + 1
    cmp_block = (6 * d) // (2 * N_KV_GROUPS * HEAD_DIM)
    gl_block = (6 * d + 2 * N_KV_GROUPS * HEAD_DIM) // LANES

    ovt, expand, gexp = _static_tables(s, tq)
    biasn, biasc = _bias_tables(rel_bias, s, tq)

    row = lambda v: v.reshape(1, -1)
    pw_b, wm_b, wq_b, wkv_b, wo_b, wfi_b, wfo_b = (
        w.astype(BF16) for w in (conv_pw_w, w_out, xq_w, xkv_w, xo_w, ffn_in_w, ffn_out_w))

    xf = x.reshape(m, d)
    for l in range(depth):
        p = norm_matmul(xf, row(norm_mix_g[l]), w_in_pt, l, tm=ROW_TILE)
        p3 = p.reshape(b, s, n_p)
        conv = conv_module(p3, conv_dw_w[l], row(conv_dw_b[l]), row(conv_ln_g[l]), row(conv_ln_b[l]),
                           pw_b, l, t=ROW_TILE)
        pos4 = jnp.tile(cmp_pos[l], (1, 1, N_KV_GROUPS))
        kvc = compress(p3, cmp_block, pos4, cmp_w1[l].astype(BF16), cmp_w2[l].astype(BF16))
        nsa = nsa_attention(p3, kvc, gexp, biasn, biasc, ovt, expand, tq=tq,
                            q_block=q_block, kv_block=kv_block, gl_block=gl_block)
        kvx = matmul(mem.reshape(b * mlen, d), wkv_b, l, tm=mlen)
        xf = merge_xattn_block(p3, conv, nsa, wm_b, xf.reshape(b, s, d), row(norm_x_g[l]), wq_b,
                               kvx.reshape(b, mlen, 2 * d), wo_b, l, tm=ROW_TILE, gc_block=gc_block,
                               ga_block=ga_block).reshape(m, d)
        xf = ffn_block(xf, row(norm_ffn_g[l]), wfi_b, wfo_b, row(final_norm_g), l,
                       tm=ROW_TILE, norm_out=(l == depth - 1))
    return xf.reshape(b, s, d)
```

```python
import functools
import math

import numpy as np
import jax
import jax.numpy as jnp
from jax import lax
from jax.experimental import pallas as pl
from jax.experimental.pallas import tpu as pltpu

F32 = jnp.float32
BF16 = jnp.bfloat16

HEAD_DIM = 64
N_KV_GROUPS = 4
HEADS_PER_GROUP = 4
N_HEADS = N_KV_GROUPS * HEADS_PER_GROUP
GROUP_WIDTH = HEADS_PER_GROUP * HEAD_DIM
CMP_BLOCK = 32
CMP_STRIDE = 16
SEL_BLOCK = 64
SEL_TOP_N = 16
WINDOW = 512
CONV_WIDTH = 31
X_HEADS = 4
REL_BUCKETS = 32
REL_MAX_DIST = 128
EPS = 1e-6
NEG = -1e30
MAX_FLOOR = -1e20
LOG2E = math.log2(math.e)

LANES = 128
SUBLANES = 8
CONV_ROWS = 64
CONV_HALO = 32
N_CMP_PAD = 128
VMEM_LIMIT = 56 * 1024 * 1024

ROW_TILE = 512
NSA_Q_TILE = WINDOW // 2


def _sigmoid(x):
    return 0.5 * jnp.tanh(0.5 * x) + 0.5


def _dot(a, b):
    return jnp.dot(a, b, preferred_element_type=F32)


def _dot_nt(a, b):
    return lax.dot_general(a, b, (((1,), (1,)), ((), ())), preferred_element_type=F32)


def _rms_norm(x, g):
    return x * lax.rsqrt(jnp.mean(x * x, axis=-1, keepdims=True) + EPS) * g


def _params(n_axes):
    return pltpu.CompilerParams(dimension_semantics=("arbitrary",) * n_axes,
                                vmem_limit_bytes=VMEM_LIMIT)


def _norm_matmul_kernel(x_ref, g_ref, wt_ref, o_ref):
    xn = _rms_norm(x_ref[...], g_ref[...]).astype(BF16)
    o_ref[...] = _dot_nt(xn, wt_ref[...]).astype(o_ref.dtype)


def norm_matmul(x, g, wt, layer, *, tm):
    m, k = x.shape
    n = wt.shape[1]
    return pl.pallas_call(
        _norm_matmul_kernel,
        grid=(m // tm,),
        in_specs=[pl.BlockSpec((tm, k), lambda i: (i, 0)),
                  pl.BlockSpec((1, k), lambda i: (0, 0)),
                  pl.BlockSpec((None, n, k), lambda i: (layer, 0, 0), pipeline_mode=pl.Buffered(1))],
        out_specs=pl.BlockSpec((tm, n), lambda i: (i, 0)),
        out_shape=jax.ShapeDtypeStruct((m, n), BF16),
        compiler_params=_params(1),
        name="norm_matmul",
    )(x, g, wt)


def _matmul_kernel(x_ref, w_ref, o_ref):
    o_ref[...] = _dot(x_ref[...].astype(BF16), w_ref[...]).astype(o_ref.dtype)


def matmul(x, w, layer, *, tm):
    m, k = x.shape
    n = w.shape[2]
    return pl.pallas_call(
        _matmul_kernel,
        grid=(m // tm,),
        in_specs=[pl.BlockSpec((tm, k), lambda i: (i, 0)),
                  pl.BlockSpec((None, k, n), lambda i: (layer, 0, 0))],
        out_specs=pl.BlockSpec((tm, n), lambda i: (i, 0)),
        out_shape=jax.ShapeDtypeStruct((m, n), BF16),
        compiler_params=_params(1),
        name="matmul",
    )(x, w)


def _conv_kernel(a_ref, gt_ref, dww_ref, dwb_ref, lng_ref, lnb_ref, pw_ref, o_ref, buf_ref, cbuf_ref, *, t):
    s = pl.program_id(1)

    @pl.when(s == 0)
    def _():
        buf_ref[0:CONV_HALO, :] = jnp.zeros((CONV_HALO, buf_ref.shape[1]), F32)

    @pl.when(s > 0)
    def _():
        buf_ref[0:CONV_HALO, :] = buf_ref[t:t + CONV_HALO, :]

    a = a_ref[0].astype(F32)
    gt = gt_ref[0].astype(F32)
    buf_ref[CONV_HALO:CONV_HALO + t, :] = a * _sigmoid(gt)

    base = CONV_HALO - (CONV_WIDTH - 1)

    def conv_rows(ci, carry):
        r0 = pl.multiple_of(ci * CONV_ROWS, CONV_ROWS)
        for c in range(buf_ref.shape[1] // LANES):
            lanes = slice(c * LANES, (c + 1) * LANES)
            acc = dwb_ref[:, lanes]
            for b in range(SUBLANES):
                n = CONV_ROWS + (SUBLANES if b else 0)
                part = None
                for k in range(CONV_WIDTH):
                    if (base + k) % SUBLANES != b:
                        continue
                    term = dww_ref[k:k + 1, lanes] * buf_ref[pl.ds(r0 + (base + k - b), n), lanes]
                    part = term if part is None else part + term
                acc = acc + part[b:b + CONV_ROWS]
            cbuf_ref[pl.ds(r0, CONV_ROWS), lanes] = acc
        return carry

    lax.fori_loop(0, t // CONV_ROWS, conv_rows, 0)
    v = cbuf_ref[...]

    mu = jnp.mean(v, axis=-1, keepdims=True)
    vc = v - mu
    y = vc * lax.rsqrt(jnp.mean(vc * vc, axis=-1, keepdims=True) + EPS)
    y = y * lng_ref[...] + lnb_ref[...]
    y = y * _sigmoid(y)
    o_ref[0] = _dot(y.astype(BF16), pw_ref[...]).astype(o_ref.dtype)


def conv_module(p, dww, dwb, lng, lnb, pw, layer, *, t):
    b, s, _ = p.shape
    c = pw.shape[1]
    vec = lambda: pl.BlockSpec((1, c), lambda bi, si: (0, 0))
    return pl.pallas_call(
        functools.partial(_conv_kernel, t=t),
        grid=(b, s // t),
        in_specs=[pl.BlockSpec((1, t, c), lambda bi, si: (bi, si, 0)),
                  pl.BlockSpec((1, t, c), lambda bi, si: (bi, si, 1)),
                  pl.BlockSpec((CONV_WIDTH, c), lambda bi, si: (0, 0)),
                  vec(), vec(), vec(),
                  pl.BlockSpec((None, c, c), lambda bi, si: (layer, 0, 0))],
        out_specs=pl.BlockSpec((1, t, c), lambda bi, si: (bi, si, 0)),
        out_shape=jax.ShapeDtypeStruct((b, s, c), BF16),
        scratch_shapes=[pltpu.VMEM((CONV_HALO + t, c), F32), pltpu.VMEM((t, c), F32)],
        compiler_params=_params(2),
        name="conv_module",
    )(p, p, dww, dwb, lng, lnb, pw)


def _compress_kernel(x_ref, pos_ref, w1s_ref, w2s_ref, o_ref, xf_ref, w1_ref, w2_ref):
    half = CMP_BLOCK // 2
    gw = N_KV_GROUPS * HEAD_DIM
    n_chunk = gw // LANES

    @pl.when(pl.program_id(0) == 0)
    def _():
        w1_ref[...] = jnp.zeros(w1_ref.shape, BF16)
        w2_ref[...] = jnp.zeros(w2_ref.shape, BF16)
        for g in range(N_KV_GROUPS):
            blk = slice(g * HEAD_DIM, (g + 1) * HEAD_DIM)
            w1_ref[:, :, blk, blk] = w1s_ref[...]
            w2_ref[:, blk, blk] = w2s_ref[...]

    for c in range(2 * n_chunk):
        xf_ref[c] = x_ref[0, :, c * LANES:(c + 1) * LANES].astype(F32)
    for kv in range(2):
        acc_a = jnp.zeros((N_CMP_PAD, gw), F32)
        acc_b = jnp.zeros((N_CMP_PAD, gw), F32)
        for l in range(half):
            rows = jnp.concatenate(
                [xf_ref[kv * n_chunk + c, pl.ds(l, N_CMP_PAD, stride=CMP_STRIDE), :] for c in range(n_chunk)],
                axis=1)
            ra = (rows + pos_ref[kv, l:l + 1, :]).astype(BF16)
            rb = (rows + pos_ref[kv, half + l:half + l + 1, :]).astype(BF16)
            acc_a = acc_a + _dot(ra, w1_ref[kv, l])
            acc_b = acc_b + _dot(rb, w1_ref[kv, half + l])
        z = acc_a + pltpu.roll(acc_b, N_CMP_PAD - 1, 0)
        h = z * _sigmoid(z)
        out = _dot(h.astype(BF16), w2_ref[kv]).astype(o_ref.dtype)
        for g in range(N_KV_GROUPS):
            o_ref[0, g, :, (1 - kv) * HEAD_DIM:(2 - kv) * HEAD_DIM] = out[:, g * HEAD_DIM:(g + 1) * HEAD_DIM]


def compress(p, col_block, pos4, w1, w2):
    b, s, _ = p.shape
    assert s == N_CMP_PAD * CMP_STRIDE
    gw = N_KV_GROUPS * HEAD_DIM
    return pl.pallas_call(
        _compress_kernel,
        grid=(b,),
        in_specs=[pl.BlockSpec((1, s, 2 * gw), lambda bi: (bi, 0, col_block)),
                  pl.BlockSpec(pos4.shape, lambda bi: (0, 0, 0)),
                  pl.BlockSpec(w1.shape, lambda bi: (0, 0, 0, 0)),
                  pl.BlockSpec(w2.shape, lambda bi: (0, 0, 0))],
        out_specs=pl.BlockSpec((1, N_KV_GROUPS, N_CMP_PAD, 2 * HEAD_DIM), lambda bi: (bi, 0, 0, 0)),
        out_shape=jax.ShapeDtypeStruct((b, N_KV_GROUPS, N_CMP_PAD, 2 * HEAD_DIM), BF16),
        scratch_shapes=[pltpu.VMEM((2 * gw // LANES, s, LANES), F32),
                        pltpu.VMEM((2, CMP_BLOCK, gw, gw), BF16),
                        pltpu.VMEM((2, gw, gw), BF16)],
        compiler_params=_params(1),
        name="nsa_compress",
    )(p, pos4, w1, w2)


def _nsa_kernel(q_ref, kv_ref, kvc_ref, gl_ref, gexp_ref, biasn_ref, biasc_ref, ovt_ref, kmask_ref,
                qn_ref, gln_ref, biascn_ref, kvcn_ref, gexpn_ref, o_ref, qsa_ref, kaug_ref, vaug_ref, sbuf_ref, wbuf_ref, sig_ref, part_ref,
                *, tq):
    qi = pl.program_id(2)
    r = HEADS_PER_GROUP
    rows = r * tq
    hd = HEAD_DIM
    n_lt = tq // LANES

    def prepare(q, gl, bc, kvc, gexp, t_base, slot):
        qs = jnp.concatenate([q[:, i * hd:(i + 1) * hd] for i in range(r)], axis=0)

        sig_ref[slot] = _sigmoid(_dot(gl, gexp))

        s = _dot_nt(qs, kvc[:, hd:]) + bc.reshape(rows, N_CMP_PAD)
        m = jnp.maximum(jnp.max(s, axis=-1, keepdims=True), MAX_FLOOR)
        p = jnp.exp2(s - m)
        p = p / jnp.maximum(jnp.sum(p, axis=-1, keepdims=True), 1e-30)
        part_ref[slot] = _dot(p.astype(BF16), kvc)

        ps = p[0:tq] + p[tq:2 * tq] + p[2 * tq:3 * tq] + p[3 * tq:4 * tq]
        ps_hi = ps.astype(BF16)
        ps_lo = (ps - ps_hi.astype(F32)).astype(BF16)
        ovt = ovt_ref[...]
        n_sel = ovt.shape[0]
        imp_t = _dot_nt(ovt, ps_hi) + _dot_nt(ovt, ps_lo)
        mi = lax.broadcasted_iota(jnp.int32, (n_sel, tq), 0)
        blk = (t_base + lax.broadcasted_iota(jnp.int32, (n_sel, tq), 1)) >> int(math.log2(SEL_BLOCK))
        forced = (mi == 0) | (mi == blk) | (mi == blk - 1)
        score = jnp.where(forced, jnp.inf, jnp.where(mi <= blk, imp_t, -jnp.inf))
        n_grp = n_sel // SUBLANES
        grp = [score[SUBLANES * a:SUBLANES * (a + 1)] for a in range(n_grp)]
        cnt = [jnp.zeros((SUBLANES, tq), F32) for _ in range(n_grp)]
        for mp in range(n_sel):
            other = score[mp:mp + 1, :]
            for a in range(n_grp):
                if SUBLANES * a > mp:
                    ahead = other >= grp[a]
                elif SUBLANES * (a + 1) - 1 <= mp:
                    ahead = other > grp[a]
                else:
                    later = (lax.broadcasted_iota(jnp.int32, (SUBLANES, tq), 0) + SUBLANES * a) > mp
                    ahead = (other > grp[a]) | ((other == grp[a]) & later)
                cnt[a] = cnt[a] + jnp.where(ahead, 1.0, 0.0)
        cnt = jnp.concatenate(cnt, axis=0)
        drop_t = jnp.where(cnt < float(min(SEL_TOP_N, n_sel)), 0.0, 1.0)
        drop_t = jnp.concatenate([jnp.zeros((hd, tq), F32), drop_t,
                                  jnp.zeros((LANES - hd - n_sel, tq), F32)], axis=0)
        drop = drop_t.T.astype(BF16)[:, hd:]
        qsa_ref[slot] = jnp.concatenate(
            [jnp.concatenate([q[:, i * hd:(i + 1) * hd], drop], axis=1) for i in range(r)], axis=0)

    def gate(j, slot):
        sig = sig_ref[slot]
        return jnp.concatenate(
            [jnp.broadcast_to(sig[:, j * r + i:j * r + i + 1], (tq, LANES)) for i in range(r)],
            axis=0)

    @pl.when(qi == 0)
    def _():
        ones = jnp.ones((kv_ref.shape[1], hd), BF16)
        vaug_ref[...] = jnp.concatenate(
            [kv_ref[0, :, hd:2 * hd], ones, kv_ref[0, :, 3 * hd:4 * hd], ones], axis=1)
        kaug_ref[...] = jnp.concatenate([kv_ref[0, :, 0:hd], kmask_ref[...]], axis=1)

    @pl.when((qi == 0) & (pl.program_id(0) == 0) & (pl.program_id(1) == 0))
    def _():
        prepare(q_ref[0], gl_ref[0], biasc_ref[0, :, 0], kvc_ref[0, 0], gexp_ref[0], 0, 0)


    def lane_max(sc):
        mx = sc[:, 0:LANES]
        for c in range(1, n_lt):
            mx = jnp.maximum(mx, sc[:, c * LANES:(c + 1) * LANES])
        return mx

    def probs(sc, m_rep):
        return jnp.concatenate([jnp.exp2(sc[:, c * LANES:(c + 1) * LANES] - m_rep)
                                for c in range(n_lt)], axis=1).astype(BF16)

    def row_max(mx):
        return jnp.broadcast_to(jnp.max(mx, axis=-1, keepdims=True), (rows, LANES))

    ri = lax.broadcasted_iota(jnp.int32, (rows, tq), 0) & (tq - 1)
    ci = lax.broadcasted_iota(jnp.int32, (rows, tq), 1)

    n_tiles = kv_ref.shape[1] // tq

    def attend(n):
        def branch():
            tile = lambda j: slice(j * tq, (j + 1) * tq)
            slot = n % 2
            prepare(qn_ref[0], gln_ref[0], biascn_ref[0, :, 0], kvcn_ref[0, 0], gexpn_ref[0],
                    ((n + 1) % n_tiles) * tq, 1 - slot)
            mx_s = None
            for j in range(n + 1):
                sc = _dot_nt(qsa_ref[slot], kaug_ref[tile(j), :])
                if j == n:
                    sc = sc + biasn_ref[0, 0]
                elif j == n - 1:
                    sc = sc + biasn_ref[0, 1]
                sbuf_ref[:, tile(j)] = sc
                mx_s = lane_max(sc) if mx_s is None else jnp.maximum(mx_s, lane_max(sc))
            win_tiles = [j for j in (n - 2, n - 1, n) if j >= 0]
            mx_w = None
            for w, j in enumerate(win_tiles):
                if j == n:
                    add = biasn_ref[0, 0]
                elif j == n - 1:
                    add = biasn_ref[0, 1]
                else:
                    add = jnp.where(ci > ri, 0.0, NEG)
                sc = _dot_nt(qsa_ref[slot, :, 0:hd], kv_ref[0, tile(j), 2 * hd:3 * hd]) + add
                wbuf_ref[:, tile(w)] = sc
                mx_w = lane_max(sc) if mx_w is None else jnp.maximum(mx_w, lane_max(sc))
            m_sel = row_max(mx_s)
            m_win = row_max(mx_w)
            acc_w = None
            for w, j in enumerate(win_tiles):
                d = _dot(probs(wbuf_ref[:, tile(w)], m_win), vaug_ref[tile(j), LANES:2 * LANES])
                acc_w = d if acc_w is None else acc_w + d
            acc_s = None
            for j in range(n + 1):
                d = _dot(probs(sbuf_ref[:, tile(j)], m_sel), vaug_ref[tile(j), 0:LANES])
                acc_s = d if acc_s is None else acc_s + d
            o = (gate(0, slot) * part_ref[slot] + gate(2, slot) * acc_w / pltpu.roll(acc_w, hd, 1)
                 + gate(1, slot) * acc_s / pltpu.roll(acc_s, hd, 1))
            o_ref[0] = jnp.concatenate([o[i * tq:(i + 1) * tq, :hd] for i in range(r)],
                                       axis=1).astype(o_ref.dtype)
        return branch

    lax.switch(qi, [attend(n) for n in range(n_tiles)])


def nsa_attention(p, kvc, gexp, biasn, biasc, ovt, expand, *, tq, q_block, kv_block, gl_block):
    b, s, _ = p.shape
    g = N_KV_GROUPS
    nq = s // tq
    rows = HEADS_PER_GROUP * tq
    assert WINDOW == 2 * tq and tq % LANES == 0 and nq % 2 == 0

    def nxt(bi, gi, qi):
        wrap = qi == nq - 1
        g2 = jnp.where(wrap, gi + 1, gi)
        b2 = jnp.minimum(jnp.where(g2 == g, bi + 1, bi), b - 1)
        return b2, g2 % g, jnp.where(wrap, 0, qi + 1)

    return pl.pallas_call(
        functools.partial(_nsa_kernel, tq=tq),
        grid=(b, g, nq),
        in_specs=[pl.BlockSpec((1, tq, GROUP_WIDTH), lambda bi, gi, qi: (bi, qi, q_block + gi)),
                  pl.BlockSpec((1, s, GROUP_WIDTH), lambda bi, gi, qi: (bi, 0, kv_block + gi)),
                  pl.BlockSpec((1, 1, N_CMP_PAD, 2 * HEAD_DIM), lambda bi, gi, qi: (bi, gi, 0, 0)),
                  pl.BlockSpec((1, tq, LANES), lambda bi, gi, qi: (bi, qi, gl_block)),
                  pl.BlockSpec((1,) + gexp.shape[1:], lambda bi, gi, qi: (gi, 0, 0)),
                  pl.BlockSpec((1, 2, rows, tq), lambda bi, gi, qi: (gi, 0, 0, 0)),
                  pl.BlockSpec((1, HEADS_PER_GROUP, 1, tq, N_CMP_PAD), lambda bi, gi, qi: (gi, 0, qi, 0, 0)),
                  pl.BlockSpec(ovt.shape, lambda bi, gi, qi: (0, 0)),
                  pl.BlockSpec(expand.shape, lambda bi, gi, qi: (0, 0)),
                  pl.BlockSpec((1, tq, GROUP_WIDTH), lambda *i: (nxt(*i)[0], nxt(*i)[2], q_block + nxt(*i)[1])),
                  pl.BlockSpec((1, tq, LANES), lambda *i: (nxt(*i)[0], nxt(*i)[2], gl_block)),
                  pl.BlockSpec((1, HEADS_PER_GROUP, 1, tq, N_CMP_PAD),
                               lambda *i: (nxt(*i)[1], 0, nxt(*i)[2], 0, 0)),
                  pl.BlockSpec((1, 1, N_CMP_PAD, 2 * HEAD_DIM), lambda *i: (nxt(*i)[0], nxt(*i)[1], 0, 0)),
                  pl.BlockSpec((1,) + gexp.shape[1:], lambda *i: (nxt(*i)[1], 0, 0))],
        out_specs=pl.BlockSpec((1, tq, GROUP_WIDTH), lambda bi, gi, qi: (bi, qi, gi)),
        out_shape=jax.ShapeDtypeStruct((b, s, g * GROUP_WIDTH), BF16),
        scratch_shapes=[pltpu.VMEM((2, rows, LANES), BF16),
                        pltpu.VMEM((s, LANES), BF16),
                        pltpu.VMEM((s, 2 * LANES), BF16),
                        pltpu.VMEM((rows, s), F32),
                        pltpu.VMEM((rows, 3 * tq), F32),
                        pltpu.VMEM((2, tq, LANES), F32),
                        pltpu.VMEM((2, rows, LANES), F32)],
        compiler_params=_params(3),
        name="nsa_attention",
    )(p, p, kvc, p, gexp, biasn, biasc, ovt, expand, p, p, biasc, kvc, gexp)


def _merge_xattn_kernel(gc_ref, ga_ref, conv_ref, nsa_ref, wm_ref, x_ref, g_ref, wq_ref, kv_ref, wo_ref,
                        o_ref):
    y = (_sigmoid(gc_ref[0].astype(F32)) * conv_ref[0].astype(F32)
         + _sigmoid(ga_ref[0].astype(F32)) * nsa_ref[0].astype(F32))
    x = x_ref[0] + _dot(y.astype(BF16), wm_ref[...])
    d = x.shape[-1]
    hd = d // X_HEADS
    h = _rms_norm(x, g_ref[...]).astype(BF16)
    q = (_dot(h, wq_ref[...]) * (1.0 / math.sqrt(hd))).astype(BF16)
    kv = kv_ref[0]
    outs = []
    for i in range(X_HEADS):
        s = _dot_nt(q[:, i * hd:(i + 1) * hd], kv[:, i * hd:(i + 1) * hd])
        p = jnp.exp(s - jnp.max(s, axis=-1, keepdims=True))
        l = jnp.sum(p, axis=-1, keepdims=True)
        outs.append(_dot(p.astype(BF16), kv[:, d + i * hd:d + (i + 1) * hd]) / l)
    o = jnp.concatenate(outs, axis=1).astype(BF16)
    o_ref[0] = x + _dot(o, wo_ref[...])


def merge_xattn_block(p3, conv, nsa, wm, x, g, wq, kv, wo, layer, *, tm, gc_block, ga_block):
    b, s, d = x.shape
    mlen = kv.shape[1]
    tile = lambda blk: pl.BlockSpec((1, tm, d), lambda bi, si: (bi, si, blk))
    mat = lambda: pl.BlockSpec((None, d, d), lambda bi, si: (layer, 0, 0))
    return pl.pallas_call(
        _merge_xattn_kernel,
        grid=(b, s // tm),
        in_specs=[tile(gc_block), tile(ga_block), tile(0), tile(0), mat(), tile(0),
                  pl.BlockSpec((1, d), lambda bi, si: (0, 0)),
                  mat(),
                  pl.BlockSpec((1, mlen, 2 * d), lambda bi, si: (bi, 0, 0)),
                  mat()],
        out_specs=tile(0),
        out_shape=jax.ShapeDtypeStruct((b, s, d), F32),
        compiler_params=_params(2),
        name="merge_xattn",
    )(p3, p3, conv, nsa, wm, x, g, wq, kv, wo)


def _ffn_kernel(x_ref, g_ref, wa_ref, wb_ref, wo_ref, gout_ref, o_ref, *, norm_out):
    x = x_ref[...]
    xn = _rms_norm(x, g_ref[...]).astype(BF16)
    a = _dot(xn, wa_ref[...])
    bb = _dot(xn, wb_ref[...])
    y = x + _dot((a * _sigmoid(a) * bb).astype(BF16), wo_ref[...])
    o_ref[...] = _rms_norm(y, gout_ref[...]) if norm_out else y


def ffn_block(x, g, w_in, w_out, g_out, layer, *, tm, norm_out):
    m, d = x.shape
    f = w_out.shape[1]
    once = pl.Buffered(1)
    return pl.pallas_call(
        functools.partial(_ffn_kernel, norm_out=norm_out),
        grid=(m // tm,),
        in_specs=[pl.BlockSpec((tm, d), lambda i: (i, 0)),
                  pl.BlockSpec((1, d), lambda i: (0, 0)),
                  pl.BlockSpec((None, d, f), lambda i: (layer, 0, 0), pipeline_mode=once),
                  pl.BlockSpec((None, d, f), lambda i: (layer, 0, 1), pipeline_mode=once),
                  pl.BlockSpec((None, f, d), lambda i: (layer, 0, 0), pipeline_mode=once),
                  pl.BlockSpec((1, d), lambda i: (0, 0))],
        out_specs=pl.BlockSpec((tm, d), lambda i: (i, 0)),
        out_shape=jax.ShapeDtypeStruct((m, d), F32),
        compiler_params=_params(1),
        name="ffn_block",
    )(x, g, w_in, w_in, w_out, g_out)


def _t5_bucket(dist):
    n = jnp.maximum(dist, 0)
    max_exact = REL_BUCKETS // 2
    nf = jnp.maximum(n, 1).astype(F32)
    large = max_exact + (jnp.log(nf / max_exact) / math.log(REL_MAX_DIST / max_exact)
                         * (REL_BUCKETS - max_exact)).astype(jnp.int32)
    large = jnp.minimum(large, REL_BUCKETS - 1)
    return jnp.where(n < max_exact, n, large)


def _bias_lookup(rel_bias, dist):
    onehot = jax.nn.one_hot(_t5_bucket(dist), REL_BUCKETS, dtype=F32)
    return jnp.einsum("bh,...b->h...", rel_bias, onehot, precision=lax.Precision.HIGHEST)


def _bias_tables(rel_bias, s, tq):
    g, r = N_KV_GROUPS, HEADS_PER_GROUP
    nq = s // tq
    rel_bias = rel_bias * LOG2E
    far = _bias_lookup(rel_bias, jnp.asarray(s, jnp.int32))
    i = jnp.arange(tq)[:, None]
    j = jnp.arange(tq)[None, :]

    def near(dist):
        bt = _bias_lookup(rel_bias, dist) - far[:, None, None]
        bt = jnp.where((dist >= 0)[None], bt, NEG)
        return bt.reshape(g, r * tq, tq)

    biasn = jnp.stack([near(i - j), near(i - j + tq)], axis=1)

    t = jnp.arange(s)[:, None]
    c_end = jnp.arange(N_CMP_PAD)[None, :] * CMP_STRIDE + CMP_BLOCK - 1
    bc = _bias_lookup(rel_bias, t - c_end) + jnp.where(t >= c_end, 0.0, NEG)[None]
    biasc = bc.reshape(g, r, nq, tq, N_CMP_PAD)
    return biasn, biasc


def _static_tables(s, tq):
    n_sel = s // SEL_BLOCK
    jj = np.arange(N_CMP_PAD)[None, :] * CMP_STRIDE
    mm0 = np.arange(n_sel)[:, None] * SEL_BLOCK
    n_cmp = (s - CMP_BLOCK) // CMP_STRIDE + 1
    ovt = ((jj < mm0 + SEL_BLOCK) & (jj + CMP_BLOCK > mm0) & (np.arange(N_CMP_PAD)[None, :] < n_cmp))
    assert n_sel <= HEAD_DIM
    expand = np.zeros((s, HEAD_DIM), np.float32)
    expand[np.arange(s), np.arange(s) // SEL_BLOCK] = NEG
    gexp = np.zeros((N_KV_GROUPS, LANES, LANES), np.float32)
    for g in range(N_KV_GROUPS):
        for j in range(3):
            for i in range(HEADS_PER_GROUP):
                gexp[g, j * N_HEADS + g * HEADS_PER_GROUP + i, j * HEADS_PER_GROUP + i] = 1.0
    d = np.arange(LANES + 1, s + 1).astype(np.float32)
    big = 16 + (np.log(d / 16) / math.log(REL_MAX_DIST / 16) * 16).astype(np.int32)
    assert np.all(np.minimum(big, REL_BUCKETS - 1) == REL_BUCKETS - 1)
    return (jnp.asarray(ovt.astype(np.float32), BF16), jnp.asarray(expand, BF16), jnp.asarray(gexp, BF16))


def _in_proj_perm(d):
    g, hd = N_KV_GROUPS, HEAD_DIM
    kvw = g * hd
    o_conv, o_q = 0, 2 * d
    o_kc = o_q + N_HEADS * hd
    o_vc, o_ks, o_vs, o_kw, o_vw = (o_kc + i * kvw for i in range(1, 6))
    o_gate = o_kc + 6 * kvw
    o_gc = o_gate + 3 * N_HEADS
    o_ga = o_gc + d
    segs = [(o_conv, 2 * d, None), (o_q, N_HEADS * hd, LOG2E / math.sqrt(hd))]
    for gi in range(g):
        segs += [(base + gi * hd, hd, None) for base in (o_ks, o_vs, o_kw, o_vw)]
    segs += [(o_gc, 2 * d, None), (o_kc, 2 * kvw, None), (o_gate, 3 * N_HEADS, None)]
    n_used = sum(n for _, n, _ in segs)
    return segs, (-n_used) % LANES


def _permute_in_proj(w_in):
    segs, n_pad = _in_proj_perm(w_in.shape[1])
    wt = jnp.swapaxes(w_in, 1, 2)
    parts = [(wt[:, a:a + n] if sc is None else wt[:, a:a + n] * sc).astype(BF16) for a, n, sc in segs]
    parts.append(jnp.zeros((wt.shape[0], n_pad, wt.shape[2]), BF16))
    return jnp.concatenate(parts, axis=1)


def kernel(x, mem, norm_mix_g, w_in, conv_dw_w, conv_dw_b, conv_ln_g, conv_ln_b, conv_pw_w, cmp_pos,
           cmp_w1, cmp_w2, w_out, norm_x_g, xq_w, xkv_w, xo_w, norm_ffn_g, ffn_in_w, ffn_out_w,
           rel_bias, final_norm_g):
    b, s, d = x.shape
    depth = w_in.shape[0]
    mlen = mem.shape[1]
    m = b * s
    tq = NSA_Q_TILE

    w_in_pt = _permute_in_proj(w_in)
    n_p = w_in_pt.shape[1]
    q_block = (2 * d) // GROUP_WIDTH
    kv_block = q_block + N_KV_GROUPS
    gc_block = (3 * d + N_KV_GROUPS * GROUP_WIDTH) // d
    ga_block = gc_block + 1
    cmp_block = (6 * d) // (2 * N_KV_GROUPS * HEAD_DIM)
    gl_block = (6 * d + 2 * N_KV_GROUPS * HEAD_DIM) // LANES

    ovt, expand, gexp = _static_tables(s, tq)
    biasn, biasc = _bias_tables(rel_bias, s, tq)

    row = lambda v: v.reshape(1, -1)
    pw_b, wm_b, wq_b, wkv_b, wo_b, wfi_b, wfo_b = (
        w.astype(BF16) for w in (conv_pw_w, w_out, xq_w, xkv_w, xo_w, ffn_in_w, ffn_out_w))

    xf = x.reshape(m, d)
    for l in range(depth):
        p = norm_matmul(xf, row(norm_mix_g[l]), w_in_pt, l, tm=ROW_TILE)
        p3 = p.reshape(b, s, n_p)
        conv = conv_module(p3, conv_dw_w[l], row(conv_dw_b[l]), row(conv_ln_g[l]), row(conv_ln_b[l]),
                           pw_b, l, t=ROW_TILE)
        pos4 = jnp.tile(cmp_pos[l], (1, 1, N_KV_GROUPS))
        kvc = compress(p3, cmp_block, pos4, cmp_w1[l].astype(BF16), cmp_w2[l].astype(BF16))
        nsa = nsa_attention(p3, kvc, gexp, biasn, biasc, ovt, expand, tq=tq,
                            q_block=q_block, kv_block=kv_block, gl_block=gl_block)
        kvx = matmul(mem.reshape(b * mlen, d), wkv_b, l, tm=mlen)
        xf = merge_xattn_block(p3, conv, nsa, wm_b, xf.reshape(b, s, d), row(norm_x_g[l]), wq_b,
                               kvx.reshape(b, mlen, 2 * d), wo_b, l, tm=ROW_TILE, gc_block=gc_block,
                               ga_block=ga_block).reshape(m, d)
        xf = ffn_block(xf, row(norm_ffn_g[l]), wfi_b, wfo_b, row(final_norm_g), l,
                       tm=ROW_TILE, norm_out=(l == depth - 1))
    return xf.reshape(b, s, d)
```

```python
import functools
import math

import numpy as np
import jax
import jax.numpy as jnp
from jax import lax
from jax.experimental import pallas as pl
from jax.experimental.pallas import tpu as pltpu

F32 = jnp.float32
BF16 = jnp.bfloat16

HEAD_DIM = 64
N_KV_GROUPS = 4
HEADS_PER_GROUP = 4
N_HEADS = N_KV_GROUPS * HEADS_PER_GROUP
GROUP_WIDTH = HEADS_PER_GROUP * HEAD_DIM
CMP_BLOCK = 32
CMP_STRIDE = 16
SEL_BLOCK = 64
SEL_TOP_N = 16
WINDOW = 512
CONV_WIDTH = 31
X_HEADS = 4
REL_BUCKETS = 32
REL_MAX_DIST = 128
EPS = 1e-6
NEG = -1e30
MAX_FLOOR = -1e20
LOG2E = math.log2(math.e)

LANES = 128
SUBLANES = 8
CONV_ROWS = 64
CONV_HALO = 32
N_CMP_PAD = 128
VMEM_LIMIT = 56 * 1024 * 1024

ROW_TILE = 512
MATMUL_ROW_TILE = 256
NSA_Q_TILE = WINDOW // 2


def _sigmoid(x):
    return 0.5 * jnp.tanh(0.5 * x) + 0.5


def _dot(a, b):
    return jnp.dot(a, b, preferred_element_type=F32)


def _dot_nt(a, b):
    return lax.dot_general(a, b, (((1,), (1,)), ((), ())), preferred_element_type=F32)


def _rms_norm(x, g):
    return x * lax.rsqrt(jnp.mean(x * x, axis=-1, keepdims=True) + EPS) * g


def _params(n_axes):
    return pltpu.CompilerParams(dimension_semantics=("arbitrary",) * n_axes,
                                vmem_limit_bytes=VMEM_LIMIT)


def _norm_matmul_kernel(x_ref, g_ref, wt_ref, o_ref):
    xn = _rms_norm(x_ref[...], g_ref[...]).astype(BF16)
    o_ref[...] = _dot_nt(xn, wt_ref[...]).astype(o_ref.dtype)


def norm_matmul(x, g, wt, layer, *, tm):
    m, k = x.shape
    n = wt.shape[1]
    return pl.pallas_call(
        _norm_matmul_kernel,
        grid=(m // tm,),
        in_specs=[pl.BlockSpec((tm, k), lambda i: (i, 0)),
                  pl.BlockSpec((1, k), lambda i: (0, 0)),
                  pl.BlockSpec((None, n, k), lambda i: (layer, 0, 0), pipeline_mode=pl.Buffered(1))],
        out_specs=pl.BlockSpec((tm, n), lambda i: (i, 0)),
        out_shape=jax.ShapeDtypeStruct((m, n), BF16),
        compiler_params=_params(1),
        name="norm_matmul",
    )(x, g, wt)


def _matmul_kernel(x_ref, w_ref, o_ref):
    o_ref[...] = _dot(x_ref[...].astype(BF16), w_ref[...]).astype(o_ref.dtype)


def matmul(x, w, layer, *, tm):
    m, k = x.shape
    n = w.shape[2]
    return pl.pallas_call(
        _matmul_kernel,
        grid=(m // tm,),
        in_specs=[pl.BlockSpec((tm, k), lambda i: (i, 0)),
                  pl.BlockSpec((None, k, n), lambda i: (layer, 0, 0))],
        out_specs=pl.BlockSpec((tm, n), lambda i: (i, 0)),
        out_shape=jax.ShapeDtypeStruct((m, n), BF16),
        compiler_params=_params(1),
        name="matmul",
    )(x, w)


def _conv_kernel(a_ref, gt_ref, dww_ref, dwb_ref, lng_ref, lnb_ref, pw_ref, o_ref, buf_ref, cbuf_ref, *, t):
    s = pl.program_id(1)

    @pl.when(s == 0)
    def _():
        buf_ref[0:CONV_HALO, :] = jnp.zeros((CONV_HALO, buf_ref.shape[1]), F32)

    @pl.when(s > 0)
    def _():
        buf_ref[0:CONV_HALO, :] = buf_ref[t:t + CONV_HALO, :]

    a = a_ref[0].astype(F32)
    gt = gt_ref[0].astype(F32)
    buf_ref[CONV_HALO:CONV_HALO + t, :] = a * _sigmoid(gt)

    base = CONV_HALO - (CONV_WIDTH - 1)

    def conv_rows(ci, carry):
        r0 = pl.multiple_of(ci * CONV_ROWS, CONV_ROWS)
        for c in range(buf_ref.shape[1] // LANES):
            lanes = slice(c * LANES, (c + 1) * LANES)
            acc = dwb_ref[:, lanes]
            for b in range(SUBLANES):
                n = CONV_ROWS + (SUBLANES if b else 0)
                part = None
                for k in range(CONV_WIDTH):
                    if (base + k) % SUBLANES != b:
                        continue
                    term = dww_ref[k:k + 1, lanes] * buf_ref[pl.ds(r0 + (base + k - b), n), lanes]
                    part = term if part is None else part + term
                acc = acc + part[b:b + CONV_ROWS]
            cbuf_ref[pl.ds(r0, CONV_ROWS), lanes] = acc
        return carry

    lax.fori_loop(0, t // CONV_ROWS, conv_rows, 0)
    v = cbuf_ref[...]

    mu = jnp.mean(v, axis=-1, keepdims=True)
    vc = v - mu
    y = vc * lax.rsqrt(jnp.mean(vc * vc, axis=-1, keepdims=True) + EPS)
    y = y * lng_ref[...] + lnb_ref[...]
    y = y * _sigmoid(y)
    o_ref[0] = _dot(y.astype(BF16), pw_ref[...]).astype(o_ref.dtype)


def conv_module(p, dww, dwb, lng, lnb, pw, layer, *, t):
    b, s, _ = p.shape
    c = pw.shape[1]
    vec = lambda: pl.BlockSpec((1, c), lambda bi, si: (0, 0))
    return pl.pallas_call(
        functools.partial(_conv_kernel, t=t),
        grid=(b, s // t),
        in_specs=[pl.BlockSpec((1, t, c), lambda bi, si: (bi, si, 0)),
                  pl.BlockSpec((1, t, c), lambda bi, si: (bi, si, 1)),
                  pl.BlockSpec((CONV_WIDTH, c), lambda bi, si: (0, 0)),
                  vec(), vec(), vec(),
                  pl.BlockSpec((None, c, c), lambda bi, si: (layer, 0, 0))],
        out_specs=pl.BlockSpec((1, t, c), lambda bi, si: (bi, si, 0)),
        out_shape=jax.ShapeDtypeStruct((b, s, c), BF16),
        scratch_shapes=[pltpu.VMEM((CONV_HALO + t, c), F32), pltpu.VMEM((t, c), F32)],
        compiler_params=_params(2),
        name="conv_module",
    )(p, p, dww, dwb, lng, lnb, pw)


def _compress_kernel(x_ref, pos_ref, w1s_ref, w2s_ref, o_ref, xf_ref, w1_ref, w2_ref):
    half = CMP_BLOCK // 2
    gw = N_KV_GROUPS * HEAD_DIM
    n_chunk = gw // LANES

    @pl.when(pl.program_id(0) == 0)
    def _():
        w1_ref[...] = jnp.zeros(w1_ref.shape, BF16)
        w2_ref[...] = jnp.zeros(w2_ref.shape, BF16)
        for g in range(N_KV_GROUPS):
            blk = slice(g * HEAD_DIM, (g + 1) * HEAD_DIM)
            w1_ref[:, :, blk, blk] = w1s_ref[...]
            w2_ref[:, blk, blk] = w2s_ref[...]

    for c in range(2 * n_chunk):
        xf_ref[c] = x_ref[0, :, c * LANES:(c + 1) * LANES].astype(F32)
    for kv in range(2):
        acc_a = jnp.zeros((N_CMP_PAD, gw), F32)
        acc_b = jnp.zeros((N_CMP_PAD, gw), F32)
        for l in range(half):
            rows = jnp.concatenate(
                [xf_ref[kv * n_chunk + c, pl.ds(l, N_CMP_PAD, stride=CMP_STRIDE), :] for c in range(n_chunk)],
                axis=1)
            ra = (rows + pos_ref[kv, l:l + 1, :]).astype(BF16)
            rb = (rows + pos_ref[kv, half + l:half + l + 1, :]).astype(BF16)
            acc_a = acc_a + _dot(ra, w1_ref[kv, l])
            acc_b = acc_b + _dot(rb, w1_ref[kv, half + l])
        z = acc_a + pltpu.roll(acc_b, N_CMP_PAD - 1, 0)
        h = z * _sigmoid(z)
        out = _dot(h.astype(BF16), w2_ref[kv]).astype(o_ref.dtype)
        for g in range(N_KV_GROUPS):
            o_ref[0, g, :, (1 - kv) * HEAD_DIM:(2 - kv) * HEAD_DIM] = out[:, g * HEAD_DIM:(g + 1) * HEAD_DIM]


def compress(p, col_block, pos4, w1, w2):
    b, s, _ = p.shape
    assert s == N_CMP_PAD * CMP_STRIDE
    gw = N_KV_GROUPS * HEAD_DIM
    return pl.pallas_call(
        _compress_kernel,
        grid=(b,),
        in_specs=[pl.BlockSpec((1, s, 2 * gw), lambda bi: (bi, 0, col_block)),
                  pl.BlockSpec(pos4.shape, lambda bi: (0, 0, 0)),
                  pl.BlockSpec(w1.shape, lambda bi: (0, 0, 0, 0)),
                  pl.BlockSpec(w2.shape, lambda bi: (0, 0, 0))],
        out_specs=pl.BlockSpec((1, N_KV_GROUPS, N_CMP_PAD, 2 * HEAD_DIM), lambda bi: (bi, 0, 0, 0)),
        out_shape=jax.ShapeDtypeStruct((b, N_KV_GROUPS, N_CMP_PAD, 2 * HEAD_DIM), BF16),
        scratch_shapes=[pltpu.VMEM((2 * gw // LANES, s, LANES), F32),
                        pltpu.VMEM((2, CMP_BLOCK, gw, gw), BF16),
                        pltpu.VMEM((2, gw, gw), BF16)],
        compiler_params=_params(1),
        name="nsa_compress",
    )(p, pos4, w1, w2)


def _nsa_kernel(q_ref, kv_ref, kvc_ref, gl_ref, gexp_ref, biasn_ref, biasc_ref, ovt_ref, kmask_ref,
                qn_ref, gln_ref, biascn_ref, o_ref, qsa_ref, kaug_ref, vaug_ref, sbuf_ref, wbuf_ref, sig_ref, part_ref,
                *, tq):
    qi = pl.program_id(2)
    r = HEADS_PER_GROUP
    rows = r * tq
    hd = HEAD_DIM
    n_lt = tq // LANES

    def prepare(q, gl, bc, t_base, slot):
        qs = jnp.concatenate([q[:, i * hd:(i + 1) * hd] for i in range(r)], axis=0)

        sig_ref[slot] = _sigmoid(_dot(gl, gexp_ref[0]))

        kvc = kvc_ref[0, 0]
        s = _dot_nt(qs, kvc[:, hd:]) + bc.reshape(rows, N_CMP_PAD)
        m = jnp.maximum(jnp.max(s, axis=-1, keepdims=True), MAX_FLOOR)
        p = jnp.exp2(s - m)
        p = p / jnp.maximum(jnp.sum(p, axis=-1, keepdims=True), 1e-30)
        part_ref[slot] = _dot(p.astype(BF16), kvc)

        ps = p[0:tq] + p[tq:2 * tq] + p[2 * tq:3 * tq] + p[3 * tq:4 * tq]
        ps_hi = ps.astype(BF16)
        ps_lo = (ps - ps_hi.astype(F32)).astype(BF16)
        ovt = ovt_ref[...]
        n_sel = ovt.shape[0]
        imp_t = _dot_nt(ovt, ps_hi) + _dot_nt(ovt, ps_lo)
        mi = lax.broadcasted_iota(jnp.int32, (n_sel, tq), 0)
        blk = (t_base + lax.broadcasted_iota(jnp.int32, (n_sel, tq), 1)) >> int(math.log2(SEL_BLOCK))
        forced = (mi == 0) | (mi == blk) | (mi == blk - 1)
        score = jnp.where(forced, jnp.inf, jnp.where(mi <= blk, imp_t, -jnp.inf))
        n_grp = n_sel // SUBLANES
        grp = [score[SUBLANES * a:SUBLANES * (a + 1)] for a in range(n_grp)]
        cnt = [jnp.zeros((SUBLANES, tq), F32) for _ in range(n_grp)]
        for mp in range(n_sel):
            other = score[mp:mp + 1, :]
            for a in range(n_grp):
                if SUBLANES * a > mp:
                    ahead = other >= grp[a]
                elif SUBLANES * (a + 1) - 1 <= mp:
                    ahead = other > grp[a]
                else:
                    later = (lax.broadcasted_iota(jnp.int32, (SUBLANES, tq), 0) + SUBLANES * a) > mp
                    ahead = (other > grp[a]) | ((other == grp[a]) & later)
                cnt[a] = cnt[a] + jnp.where(ahead, 1.0, 0.0)
        cnt = jnp.concatenate(cnt, axis=0)
        drop_t = jnp.where(cnt < float(min(SEL_TOP_N, n_sel)), 0.0, 1.0)
        drop_t = jnp.concatenate([jnp.zeros((hd, tq), F32), drop_t,
                                  jnp.zeros((LANES - hd - n_sel, tq), F32)], axis=0)
        drop = drop_t.T.astype(BF16)[:, hd:]
        qsa_ref[slot] = jnp.concatenate(
            [jnp.concatenate([q[:, i * hd:(i + 1) * hd], drop], axis=1) for i in range(r)], axis=0)

    def gate(j, slot):
        sig = sig_ref[slot]
        return jnp.concatenate(
            [jnp.broadcast_to(sig[:, j * r + i:j * r + i + 1], (tq, LANES)) for i in range(r)],
            axis=0)

    @pl.when(qi == 0)
    def _():
        ones = jnp.ones((kv_ref.shape[1], hd), BF16)
        vaug_ref[...] = jnp.concatenate(
            [kv_ref[0, :, hd:2 * hd], ones, kv_ref[0, :, 3 * hd:4 * hd], ones], axis=1)
        kaug_ref[...] = jnp.concatenate([kv_ref[0, :, 0:hd], kmask_ref[...]], axis=1)
        prepare(q_ref[0], gl_ref[0], biasc_ref[0, :, 0], 0, 0)


    def lane_max(sc):
        mx = sc[:, 0:LANES]
        for c in range(1, n_lt):
            mx = jnp.maximum(mx, sc[:, c * LANES:(c + 1) * LANES])
        return mx

    def probs(sc, m_rep):
        return jnp.concatenate([jnp.exp2(sc[:, c * LANES:(c + 1) * LANES] - m_rep)
                                for c in range(n_lt)], axis=1).astype(BF16)

    def row_max(mx):
        return jnp.broadcast_to(jnp.max(mx, axis=-1, keepdims=True), (rows, LANES))

    ri = lax.broadcasted_iota(jnp.int32, (rows, tq), 0) & (tq - 1)
    ci = lax.broadcasted_iota(jnp.int32, (rows, tq), 1)

    n_tiles = kv_ref.shape[1] // tq

    def attend(n):
        def branch():
            tile = lambda j: slice(j * tq, (j + 1) * tq)
            slot = n % 2
            if n + 1 < n_tiles:
                prepare(qn_ref[0], gln_ref[0], biascn_ref[0, :, 0], (n + 1) * tq, 1 - slot)
            mx_s = None
            for j in range(n + 1):
                sc = _dot_nt(qsa_ref[slot], kaug_ref[tile(j), :])
                if j == n:
                    sc = sc + biasn_ref[0, 0]
                elif j == n - 1:
                    sc = sc + biasn_ref[0, 1]
                sbuf_ref[:, tile(j)] = sc
                mx_s = lane_max(sc) if mx_s is None else jnp.maximum(mx_s, lane_max(sc))
            win_tiles = [j for j in (n - 2, n - 1, n) if j >= 0]
            mx_w = None
            for w, j in enumerate(win_tiles):
                if j == n:
                    add = biasn_ref[0, 0]
                elif j == n - 1:
                    add = biasn_ref[0, 1]
                else:
                    add = jnp.where(ci > ri, 0.0, NEG)
                sc = _dot_nt(qsa_ref[slot, :, 0:hd], kv_ref[0, tile(j), 2 * hd:3 * hd]) + add
                wbuf_ref[:, tile(w)] = sc
                mx_w = lane_max(sc) if mx_w is None else jnp.maximum(mx_w, lane_max(sc))
            m_sel = row_max(mx_s)
            m_win = row_max(mx_w)
            acc_w = None
            for w, j in enumerate(win_tiles):
                d = _dot(probs(wbuf_ref[:, tile(w)], m_win), vaug_ref[tile(j), LANES:2 * LANES])
                acc_w = d if acc_w is None else acc_w + d
            acc_s = None
            for j in range(n + 1):
                d = _dot(probs(sbuf_ref[:, tile(j)], m_sel), vaug_ref[tile(j), 0:LANES])
                acc_s = d if acc_s is None else acc_s + d
            o = (gate(0, slot) * part_ref[slot] + gate(2, slot) * acc_w / pltpu.roll(acc_w, hd, 1)
                 + gate(1, slot) * acc_s / pltpu.roll(acc_s, hd, 1))
            o_ref[0] = jnp.concatenate([o[i * tq:(i + 1) * tq, :hd] for i in range(r)],
                                       axis=1).astype(o_ref.dtype)
        return branch

    lax.switch(qi, [attend(n) for n in range(n_tiles)])


def nsa_attention(p, kvc, gexp, biasn, biasc, ovt, expand, *, tq, q_block, kv_block, gl_block):
    b, s, _ = p.shape
    g = N_KV_GROUPS
    nq = s // tq
    rows = HEADS_PER_GROUP * tq
    assert WINDOW == 2 * tq and tq % LANES == 0
    return pl.pallas_call(
        functools.partial(_nsa_kernel, tq=tq),
        grid=(b, g, nq),
        in_specs=[pl.BlockSpec((1, tq, GROUP_WIDTH), lambda bi, gi, qi: (bi, qi, q_block + gi)),
                  pl.BlockSpec((1, s, GROUP_WIDTH), lambda bi, gi, qi: (bi, 0, kv_block + gi)),
                  pl.BlockSpec((1, 1, N_CMP_PAD, 2 * HEAD_DIM), lambda bi, gi, qi: (bi, gi, 0, 0)),
                  pl.BlockSpec((1, tq, LANES), lambda bi, gi, qi: (bi, qi, gl_block)),
                  pl.BlockSpec((1,) + gexp.shape[1:], lambda bi, gi, qi: (gi, 0, 0)),
                  pl.BlockSpec((1, 2, rows, tq), lambda bi, gi, qi: (gi, 0, 0, 0)),
                  pl.BlockSpec((1, HEADS_PER_GROUP, 1, tq, N_CMP_PAD), lambda bi, gi, qi: (gi, 0, qi, 0, 0)),
                  pl.BlockSpec(ovt.shape, lambda bi, gi, qi: (0, 0)),
                  pl.BlockSpec(expand.shape, lambda bi, gi, qi: (0, 0)),
                  pl.BlockSpec((1, tq, GROUP_WIDTH),
                               lambda bi, gi, qi: (bi, jnp.minimum(qi + 1, nq - 1), q_block + gi)),
                  pl.BlockSpec((1, tq, LANES), lambda bi, gi, qi: (bi, jnp.minimum(qi + 1, nq - 1), gl_block)),
                  pl.BlockSpec((1, HEADS_PER_GROUP, 1, tq, N_CMP_PAD),
                               lambda bi, gi, qi: (gi, 0, jnp.minimum(qi + 1, nq - 1), 0, 0))],
        out_specs=pl.BlockSpec((1, tq, GROUP_WIDTH), lambda bi, gi, qi: (bi, qi, gi)),
        out_shape=jax.ShapeDtypeStruct((b, s, g * GROUP_WIDTH), BF16),
        scratch_shapes=[pltpu.VMEM((2, rows, LANES), BF16),
                        pltpu.VMEM((s, LANES), BF16),
                        pltpu.VMEM((s, 2 * LANES), BF16),
                        pltpu.VMEM((rows, s), F32),
                        pltpu.VMEM((rows, 3 * tq), F32),
                        pltpu.VMEM((2, tq, LANES), F32),
                        pltpu.VMEM((2, rows, LANES), F32)],
        compiler_params=_params(3),
        name="nsa_attention",
    )(p, p, kvc, p, gexp, biasn, biasc, ovt, expand, p, p, biasc)


def _merge_xattn_kernel(gc_ref, ga_ref, conv_ref, nsa_ref, wm_ref, x_ref, g_ref, wq_ref, kv_ref, wo_ref,
                        o_ref):
    y = (_sigmoid(gc_ref[0].astype(F32)) * conv_ref[0].astype(F32)
         + _sigmoid(ga_ref[0].astype(F32)) * nsa_ref[0].astype(F32))
    x = x_ref[0] + _dot(y.astype(BF16), wm_ref[...])
    d = x.shape[-1]
    hd = d // X_HEADS
    h = _rms_norm(x, g_ref[...]).astype(BF16)
    q = (_dot(h, wq_ref[...]) * (1.0 / math.sqrt(hd))).astype(BF16)
    kv = kv_ref[0]
    outs = []
    for i in range(X_HEADS):
        s = _dot_nt(q[:, i * hd:(i + 1) * hd], kv[:, i * hd:(i + 1) * hd])
        p = jnp.exp(s - jnp.max(s, axis=-1, keepdims=True))
        l = jnp.sum(p, axis=-1, keepdims=True)
        outs.append(_dot(p.astype(BF16), kv[:, d + i * hd:d + (i + 1) * hd]) / l)
    o = jnp.concatenate(outs, axis=1).astype(BF16)
    o_ref[0] = x + _dot(o, wo_ref[...])


def merge_xattn_block(p3, conv, nsa, wm, x, g, wq, kv, wo, layer, *, tm, gc_block, ga_block):
    b, s, d = x.shape
    mlen = kv.shape[1]
    tile = lambda blk: pl.BlockSpec((1, tm, d), lambda bi, si: (bi, si, blk))
    mat = lambda: pl.BlockSpec((None, d, d), lambda bi, si: (layer, 0, 0))
    return pl.pallas_call(
        _merge_xattn_kernel,
        grid=(b, s // tm),
        in_specs=[tile(gc_block), tile(ga_block), tile(0), tile(0), mat(), tile(0),
                  pl.BlockSpec((1, d), lambda bi, si: (0, 0)),
                  mat(),
                  pl.BlockSpec((1, mlen, 2 * d), lambda bi, si: (bi, 0, 0)),
                  mat()],
        out_specs=tile(0),
        out_shape=jax.ShapeDtypeStruct((b, s, d), F32),
        compiler_params=_params(2),
        name="merge_xattn",
    )(p3, p3, conv, nsa, wm, x, g, wq, kv, wo)


def _ffn_kernel(x_ref, g_ref, wa_ref, wb_ref, wo_ref, gout_ref, o_ref, *, norm_out):
    x = x_ref[...]
    xn = _rms_norm(x, g_ref[...]).astype(BF16)
    a = _dot(xn, wa_ref[...])
    bb = _dot(xn, wb_ref[...])
    y = x + _dot((a * _sigmoid(a) * bb).astype(BF16), wo_ref[...])
    o_ref[...] = _rms_norm(y, gout_ref[...]) if norm_out else y


def ffn_block(x, g, w_in, w_out, g_out, layer, *, tm, norm_out):
    m, d = x.shape
    f = w_out.shape[1]
    once = pl.Buffered(1)
    return pl.pallas_call(
        functools.partial(_ffn_kernel, norm_out=norm_out),
        grid=(m // tm,),
        in_specs=[pl.BlockSpec((tm, d), lambda i: (i, 0)),
                  pl.BlockSpec((1, d), lambda i: (0, 0)),
                  pl.BlockSpec((None, d, f), lambda i: (layer, 0, 0), pipeline_mode=once),
                  pl.BlockSpec((None, d, f), lambda i: (layer, 0, 1), pipeline_mode=once),
                  pl.BlockSpec((None, f, d), lambda i: (layer, 0, 0), pipeline_mode=once),
                  pl.BlockSpec((1, d), lambda i: (0, 0))],
        out_specs=pl.BlockSpec((tm, d), lambda i: (i, 0)),
        out_shape=jax.ShapeDtypeStruct((m, d), F32),
        compiler_params=_params(1),
        name="ffn_block",
    )(x, g, w_in, w_in, w_out, g_out)


def _t5_bucket(dist):
    n = jnp.maximum(dist, 0)
    max_exact = REL_BUCKETS // 2
    nf = jnp.maximum(n, 1).astype(F32)
    large = max_exact + (jnp.log(nf / max_exact) / math.log(REL_MAX_DIST / max_exact)
                         * (REL_BUCKETS - max_exact)).astype(jnp.int32)
    large = jnp.minimum(large, REL_BUCKETS - 1)
    return jnp.where(n < max_exact, n, large)


def _bias_lookup(rel_bias, dist):
    onehot = jax.nn.one_hot(_t5_bucket(dist), REL_BUCKETS, dtype=F32)
    return jnp.einsum("bh,...b->h...", rel_bias, onehot, precision=lax.Precision.HIGHEST)


def _bias_tables(rel_bias, s, tq):
    g, r = N_KV_GROUPS, HEADS_PER_GROUP
    nq = s // tq
    rel_bias = rel_bias * LOG2E
    far = _bias_lookup(rel_bias, jnp.asarray(s, jnp.int32))
    i = jnp.arange(tq)[:, None]
    j = jnp.arange(tq)[None, :]

    def near(dist):
        bt = _bias_lookup(rel_bias, dist) - far[:, None, None]
        bt = jnp.where((dist >= 0)[None], bt, NEG)
        return bt.reshape(g, r * tq, tq)

    biasn = jnp.stack([near(i - j), near(i - j + tq)], axis=1)

    t = jnp.arange(s)[:, None]
    c_end = jnp.arange(N_CMP_PAD)[None, :] * CMP_STRIDE + CMP_BLOCK - 1
    bc = _bias_lookup(rel_bias, t - c_end) + jnp.where(t >= c_end, 0.0, NEG)[None]
    biasc = bc.reshape(g, r, nq, tq, N_CMP_PAD)
    return biasn, biasc


def _static_tables(s, tq):
    n_sel = s // SEL_BLOCK
    jj = np.arange(N_CMP_PAD)[None, :] * CMP_STRIDE
    mm0 = np.arange(n_sel)[:, None] * SEL_BLOCK
    n_cmp = (s - CMP_BLOCK) // CMP_STRIDE + 1
    ovt = ((jj < mm0 + SEL_BLOCK) & (jj + CMP_BLOCK > mm0) & (np.arange(N_CMP_PAD)[None, :] < n_cmp))
    assert n_sel <= HEAD_DIM
    expand = np.zeros((s, HEAD_DIM), np.float32)
    expand[np.arange(s), np.arange(s) // SEL_BLOCK] = NEG
    gexp = np.zeros((N_KV_GROUPS, LANES, LANES), np.float32)
    for g in range(N_KV_GROUPS):
        for j in range(3):
            for i in range(HEADS_PER_GROUP):
                gexp[g, j * N_HEADS + g * HEADS_PER_GROUP + i, j * HEADS_PER_GROUP + i] = 1.0
    d = np.arange(LANES + 1, s + 1).astype(np.float32)
    half = REL_BUCKETS // 2
    big = half + (np.log(d / half) / math.log(REL_MAX_DIST / half) * (REL_BUCKETS - half)).astype(np.int32)
    assert np.all(np.minimum(big, REL_BUCKETS - 1) == REL_BUCKETS - 1)
    return (jnp.asarray(ovt.astype(np.float32), BF16), jnp.asarray(expand, BF16), jnp.asarray(gexp, BF16))


def _in_proj_perm(d):
    g, hd = N_KV_GROUPS, HEAD_DIM
    kvw = g * hd
    o_conv, o_q = 0, 2 * d
    o_kc = o_q + N_HEADS * hd
    o_vc, o_ks, o_vs, o_kw, o_vw = (o_kc + i * kvw for i in range(1, 6))
    o_gate = o_kc + 6 * kvw
    o_gc = o_gate + 3 * N_HEADS
    o_ga = o_gc + d
    segs = [(o_conv, 2 * d, None), (o_q, N_HEADS * hd, LOG2E / math.sqrt(hd))]
    for gi in range(g):
        segs += [(base + gi * hd, hd, None) for base in (o_ks, o_vs, o_kw, o_vw)]
    segs += [(o_gc, 2 * d, None), (o_kc, 2 * kvw, None), (o_gate, 3 * N_HEADS, None)]
    n_used = sum(n for _, n, _ in segs)
    return segs, (-n_used) % LANES


def _permute_in_proj(w_in):
    segs, n_pad = _in_proj_perm(w_in.shape[1])
    wt = jnp.swapaxes(w_in, 1, 2)
    parts = [(wt[:, a:a + n] if sc is None else wt[:, a:a + n] * sc).astype(BF16) for a, n, sc in segs]
    parts.append(jnp.zeros((wt.shape[0], n_pad, wt.shape[2]), BF16))
    return jnp.concatenate(parts, axis=1)


def kernel(x, mem, norm_mix_g, w_in, conv_dw_w, conv_dw_b, conv_ln_g, conv_ln_b, conv_pw_w, cmp_pos,
           cmp_w1, cmp_w2, w_out, norm_x_g, xq_w, xkv_w, xo_w, norm_ffn_g, ffn_in_w, ffn_out_w,
           rel_bias, final_norm_g):
    b, s, d = x.shape
    depth = w_in.shape[0]
    mlen = mem.shape[1]
    m = b * s
    tq = NSA_Q_TILE

    w_in_pt = _permute_in_proj(w_in)
    n_p = w_in_pt.shape[1]
    q_block = (2 * d) // GROUP_WIDTH
    kv_block = q_block + N_KV_GROUPS
    gc_block = (3 * d + N_KV_GROUPS * GROUP_WIDTH) // d
    ga_block = gc_block + 1
    cmp_block = (6 * d) // (2 * N_KV_GROUPS * HEAD_DIM)
    gl_block = (6 * d + 2 * N_KV_GROUPS * HEAD_DIM) // LANES

    ovt, expand, gexp = _static_tables(s, tq)
    biasn, biasc = _bias_tables(rel_bias, s, tq)

    row = lambda v: v.reshape(1, -1)
    pw_b, wm_b, wq_b, wkv_b, wo_b, wfi_b, wfo_b = (
        w.astype(BF16) for w in (conv_pw_w, w_out, xq_w, xkv_w, xo_w, ffn_in_w, ffn_out_w))

    xf = x.reshape(m, d)
    for l in range(depth):
        p = norm_matmul(xf, row(norm_mix_g[l]), w_in_pt, l, tm=MATMUL_ROW_TILE)
        p3 = p.reshape(b, s, n_p)
        conv = conv_module(p3, conv_dw_w[l], row(conv_dw_b[l]), row(conv_ln_g[l]), row(conv_ln_b[l]),
                           pw_b, l, t=ROW_TILE)
        pos4 = jnp.tile(cmp_pos[l], (1, 1, N_KV_GROUPS))
        kvc = compress(p3, cmp_block, pos4, cmp_w1[l].astype(BF16), cmp_w2[l].astype(BF16))
        nsa = nsa_attention(p3, kvc, gexp, biasn, biasc, ovt, expand, tq=tq,
                            q_block=q_block, kv_block=kv_block, gl_block=gl_block)
        kvx = matmul(mem.reshape(b * mlen, d), wkv_b, l, tm=mlen)
        xf = merge_xattn_block(p3, conv, nsa, wm_b, xf.reshape(b, s, d), row(norm_x_g[l]), wq_b,
                               kvx.reshape(b, mlen, 2 * d), wo_b, l, tm=ROW_TILE, gc_block=gc_block,
                               ga_block=ga_block).reshape(m, d)
        xf = ffn_block(xf, row(norm_ffn_g[l]), wfi_b, wfo_b, row(final_norm_g), l,
                       tm=MATMUL_ROW_TILE, norm_out=(l == depth - 1))
    return xf.reshape(b, s, d)
```

```python
import functools
import math

import numpy as np
import jax
import jax.numpy as jnp
from jax import lax
from jax.experimental import pallas as pl
from jax.experimental.pallas import tpu as pltpu

F32 = jnp.float32
BF16 = jnp.bfloat16

HEAD_DIM = 64
N_KV_GROUPS = 4
HEADS_PER_GROUP = 4
N_HEADS = N_KV_GROUPS * HEADS_PER_GROUP
GROUP_WIDTH = HEADS_PER_GROUP * HEAD_DIM
CMP_BLOCK = 32
CMP_STRIDE = 16
SEL_BLOCK = 64
SEL_TOP_N = 16
WINDOW = 512
CONV_WIDTH = 31
X_HEADS = 4
REL_BUCKETS = 32
REL_MAX_DIST = 128
EPS = 1e-6
NEG = -1e30
MAX_FLOOR = -1e20
LOG2E = math.log2(math.e)

LANES = 128
SUBLANES = 8
CONV_ROWS = 64
CONV_HALO = 32
N_CMP_PAD = 128
VMEM_LIMIT = 56 * 1024 * 1024

ROW_TILE = 512
NSA_Q_TILE = WINDOW // 2


def _sigmoid(x):
    return 0.5 * jnp.tanh(0.5 * x) + 0.5


def _dot(a, b):
    return jnp.dot(a, b, preferred_element_type=F32)


def _dot_nt(a, b):
    return lax.dot_general(a, b, (((1,), (1,)), ((), ())), preferred_element_type=F32)


def _rms_norm(x, g):
    return x * lax.rsqrt(jnp.mean(x * x, axis=-1, keepdims=True) + EPS) * g


def _params(n_axes):
    return pltpu.CompilerParams(dimension_semantics=("arbitrary",) * n_axes,
                                vmem_limit_bytes=VMEM_LIMIT)


def _norm_matmul_kernel(x_ref, g_ref, wt_ref, o_ref):
    xn = _rms_norm(x_ref[...], g_ref[...]).astype(BF16)
    o_ref[...] = _dot_nt(xn, wt_ref[...]).astype(o_ref.dtype)


def norm_matmul(x, g, wt, layer, *, tm):
    m, k = x.shape
    n = wt.shape[1]
    return pl.pallas_call(
        _norm_matmul_kernel,
        grid=(m // tm,),
        in_specs=[pl.BlockSpec((tm, k), lambda i: (i, 0)),
                  pl.BlockSpec((1, k), lambda i: (0, 0)),
                  pl.BlockSpec((None, n, k), lambda i: (layer, 0, 0), pipeline_mode=pl.Buffered(1))],
        out_specs=pl.BlockSpec((tm, n), lambda i: (i, 0)),
        out_shape=jax.ShapeDtypeStruct((m, n), BF16),
        compiler_params=_params(1),
        name="norm_matmul",
    )(x, g, wt)


def _matmul_kernel(x_ref, w_ref, o_ref):
    o_ref[...] = _dot(x_ref[...].astype(BF16), w_ref[...]).astype(o_ref.dtype)


def matmul(x, w, layer, *, tm):
    m, k = x.shape
    n = w.shape[2]
    return pl.pallas_call(
        _matmul_kernel,
        grid=(m // tm,),
        in_specs=[pl.BlockSpec((tm, k), lambda i: (i, 0)),
                  pl.BlockSpec((None, k, n), lambda i: (layer, 0, 0))],
        out_specs=pl.BlockSpec((tm, n), lambda i: (i, 0)),
        out_shape=jax.ShapeDtypeStruct((m, n), BF16),
        compiler_params=_params(1),
        name="matmul",
    )(x, w)


def _conv_kernel(a_ref, gt_ref, dww_ref, dwb_ref, lng_ref, lnb_ref, pw_ref, o_ref, buf_ref, cbuf_ref, *, t):
    s = pl.program_id(1)

    @pl.when(s == 0)
    def _():
        buf_ref[0:CONV_HALO, :] = jnp.zeros((CONV_HALO, buf_ref.shape[1]), F32)

    @pl.when(s > 0)
    def _():
        buf_ref[0:CONV_HALO, :] = buf_ref[t:t + CONV_HALO, :]

    a = a_ref[0].astype(F32)
    gt = gt_ref[0].astype(F32)
    buf_ref[CONV_HALO:CONV_HALO + t, :] = a * _sigmoid(gt)

    base = CONV_HALO - (CONV_WIDTH - 1)

    def conv_rows(ci, carry):
        r0 = pl.multiple_of(ci * CONV_ROWS, CONV_ROWS)
        for c in range(buf_ref.shape[1] // LANES):
            lanes = slice(c * LANES, (c + 1) * LANES)
            acc = dwb_ref[:, lanes]
            for b in range(SUBLANES):
                n = CONV_ROWS + (SUBLANES if b else 0)
                part = None
                for k in range(CONV_WIDTH):
                    if (base + k) % SUBLANES != b:
                        continue
                    term = dww_ref[k:k + 1, lanes] * buf_ref[pl.ds(r0 + (base + k - b), n), lanes]
                    part = term if part is None else part + term
                acc = acc + part[b:b + CONV_ROWS]
            cbuf_ref[pl.ds(r0, CONV_ROWS), lanes] = acc
        return carry

    lax.fori_loop(0, t // CONV_ROWS, conv_rows, 0)
    v = cbuf_ref[...]

    mu = jnp.mean(v, axis=-1, keepdims=True)
    vc = v - mu
    y = vc * lax.rsqrt(jnp.mean(vc * vc, axis=-1, keepdims=True) + EPS)
    y = y * lng_ref[...] + lnb_ref[...]
    y = y * _sigmoid(y)
    o_ref[0] = _dot(y.astype(BF16), pw_ref[...]).astype(o_ref.dtype)


def conv_module(p, dww, dwb, lng, lnb, pw, layer, *, t):
    b, s, _ = p.shape
    c = pw.shape[1]
    vec = lambda: pl.BlockSpec((1, c), lambda bi, si: (0, 0))
    return pl.pallas_call(
        functools.partial(_conv_kernel, t=t),
        grid=(b, s // t),
        in_specs=[pl.BlockSpec((1, t, c), lambda bi, si: (bi, si, 0)),
                  pl.BlockSpec((1, t, c), lambda bi, si: (bi, si, 1)),
                  pl.BlockSpec((CONV_WIDTH, c), lambda bi, si: (0, 0)),
                  vec(), vec(), vec(),
                  pl.BlockSpec((None, c, c), lambda bi, si: (layer, 0, 0))],
        out_specs=pl.BlockSpec((1, t, c), lambda bi, si: (bi, si, 0)),
        out_shape=jax.ShapeDtypeStruct((b, s, c), BF16),
        scratch_shapes=[pltpu.VMEM((CONV_HALO + t, c), F32), pltpu.VMEM((t, c), F32)],
        compiler_params=_params(2),
        name="conv_module",
    )(p, p, dww, dwb, lng, lnb, pw)


def _compress_kernel(x_ref, pos_ref, w1s_ref, w2s_ref, o_ref, xf_ref, w1_ref, w2_ref):
    half = CMP_BLOCK // 2
    gw = N_KV_GROUPS * HEAD_DIM
    n_chunk = gw // LANES

    @pl.when(pl.program_id(0) == 0)
    def _():
        w1_ref[...] = jnp.zeros(w1_ref.shape, BF16)
        w2_ref[...] = jnp.zeros(w2_ref.shape, BF16)
        for g in range(N_KV_GROUPS):
            blk = slice(g * HEAD_DIM, (g + 1) * HEAD_DIM)
            w1_ref[:, :, blk, blk] = w1s_ref[...]
            w2_ref[:, blk, blk] = w2s_ref[...]

    for c in range(2 * n_chunk):
        xf_ref[c] = x_ref[0, :, c * LANES:(c + 1) * LANES].astype(F32)
    for kv in range(2):
        acc_a = jnp.zeros((N_CMP_PAD, gw), F32)
        acc_b = jnp.zeros((N_CMP_PAD, gw), F32)
        for l in range(half):
            rows = jnp.concatenate(
                [xf_ref[kv * n_chunk + c, pl.ds(l, N_CMP_PAD, stride=CMP_STRIDE), :] for c in range(n_chunk)],
                axis=1)
            ra = (rows + pos_ref[kv, l:l + 1, :]).astype(BF16)
            rb = (rows + pos_ref[kv, half + l:half + l + 1, :]).astype(BF16)
            acc_a = acc_a + _dot(ra, w1_ref[kv, l])
            acc_b = acc_b + _dot(rb, w1_ref[kv, half + l])
        z = acc_a + pltpu.roll(acc_b, N_CMP_PAD - 1, 0)
        h = z * _sigmoid(z)
        out = _dot(h.astype(BF16), w2_ref[kv]).astype(o_ref.dtype)
        for g in range(N_KV_GROUPS):
            o_ref[0, g, :, (1 - kv) * HEAD_DIM:(2 - kv) * HEAD_DIM] = out[:, g * HEAD_DIM:(g + 1) * HEAD_DIM]


def compress(p, col_block, pos4, w1, w2):
    b, s, _ = p.shape
    assert s == N_CMP_PAD * CMP_STRIDE
    gw = N_KV_GROUPS * HEAD_DIM
    return pl.pallas_call(
        _compress_kernel,
        grid=(b,),
        in_specs=[pl.BlockSpec((1, s, 2 * gw), lambda bi: (bi, 0, col_block)),
                  pl.BlockSpec(pos4.shape, lambda bi: (0, 0, 0)),
                  pl.BlockSpec(w1.shape, lambda bi: (0, 0, 0, 0)),
                  pl.BlockSpec(w2.shape, lambda bi: (0, 0, 0))],
        out_specs=pl.BlockSpec((1, N_KV_GROUPS, N_CMP_PAD, 2 * HEAD_DIM), lambda bi: (bi, 0, 0, 0)),
        out_shape=jax.ShapeDtypeStruct((b, N_KV_GROUPS, N_CMP_PAD, 2 * HEAD_DIM), BF16),
        scratch_shapes=[pltpu.VMEM((2 * gw // LANES, s, LANES), F32),
                        pltpu.VMEM((2, CMP_BLOCK, gw, gw), BF16),
                        pltpu.VMEM((2, gw, gw), BF16)],
        compiler_params=_params(1),
        name="nsa_compress",
    )(p, pos4, w1, w2)


def _nsa_kernel(q_ref, kv_ref, kvc_ref, gl_ref, gexp_ref, biasn_ref, biasc_ref, ovt_ref, kmask_ref,
                qn_ref, gln_ref, biascn_ref, o_ref, qsa_ref, kaug_ref, vaug_ref, sbuf_ref, wbuf_ref, sig_ref, part_ref,
                *, tq):
    qi = pl.program_id(2)
    r = HEADS_PER_GROUP
    rows = r * tq
    hd = HEAD_DIM
    n_lt = tq // LANES

    def prepare(q, gl, bc, t_base, slot):
        qs = jnp.concatenate([q[:, i * hd:(i + 1) * hd] for i in range(r)], axis=0)

        sig_ref[slot] = _sigmoid(_dot(gl, gexp_ref[0]))

        kvc = kvc_ref[0, 0]
        s = _dot_nt(qs, kvc[:, hd:]) + bc.reshape(rows, N_CMP_PAD)
        m = jnp.maximum(jnp.max(s, axis=-1, keepdims=True), MAX_FLOOR)
        p = jnp.exp2(s - m)
        p = p / jnp.maximum(jnp.sum(p, axis=-1, keepdims=True), 1e-30)
        part_ref[slot] = _dot(p.astype(BF16), kvc)

        ps = p[0:tq] + p[tq:2 * tq] + p[2 * tq:3 * tq] + p[3 * tq:4 * tq]
        ps_hi = ps.astype(BF16)
        ps_lo = (ps - ps_hi.astype(F32)).astype(BF16)
        ovt = ovt_ref[...]
        n_sel = ovt.shape[0]
        imp_t = _dot_nt(ovt, ps_hi) + _dot_nt(ovt, ps_lo)
        mi = lax.broadcasted_iota(jnp.int32, (n_sel, tq), 0)
        blk = (t_base + lax.broadcasted_iota(jnp.int32, (n_sel, tq), 1)) >> int(math.log2(SEL_BLOCK))
        forced = (mi == 0) | (mi == blk) | (mi == blk - 1)
        score = jnp.where(forced, jnp.inf, jnp.where(mi <= blk, imp_t, -jnp.inf))
        n_grp = n_sel // SUBLANES
        grp = [score[SUBLANES * a:SUBLANES * (a + 1)] for a in range(n_grp)]
        cnt = [jnp.zeros((SUBLANES, tq), F32) for _ in range(n_grp)]
        for mp in range(n_sel):
            other = score[mp:mp + 1, :]
            for a in range(n_grp):
                if SUBLANES * a > mp:
                    ahead = other >= grp[a]
                elif SUBLANES * (a + 1) - 1 <= mp:
                    ahead = other > grp[a]
                else:
                    later = (lax.broadcasted_iota(jnp.int32, (SUBLANES, tq), 0) + SUBLANES * a) > mp
                    ahead = (other > grp[a]) | ((other == grp[a]) & later)
                cnt[a] = cnt[a] + jnp.where(ahead, 1.0, 0.0)
        cnt = jnp.concatenate(cnt, axis=0)
        drop_t = jnp.where(cnt < float(min(SEL_TOP_N, n_sel)), 0.0, 1.0)
        drop_t = jnp.concatenate([jnp.zeros((hd, tq), F32), drop_t,
                                  jnp.zeros((LANES - hd - n_sel, tq), F32)], axis=0)
        drop = drop_t.T.astype(BF16)[:, hd:]
        qsa_ref[slot] = jnp.concatenate(
            [jnp.concatenate([q[:, i * hd:(i + 1) * hd], drop], axis=1) for i in range(r)], axis=0)

    def gate(j, slot):
        sig = sig_ref[slot]
        return jnp.concatenate(
            [jnp.broadcast_to(sig[:, j * r + i:j * r + i + 1], (tq, LANES)) for i in range(r)],
            axis=0)

    @pl.when(qi == 0)
    def _():
        ones = jnp.ones((kv_ref.shape[1], hd), BF16)
        vaug_ref[...] = jnp.concatenate(
            [kv_ref[0, :, hd:2 * hd], ones, kv_ref[0, :, 3 * hd:4 * hd], ones], axis=1)
        kaug_ref[...] = jnp.concatenate([kv_ref[0, :, 0:hd], kmask_ref[...]], axis=1)
        prepare(q_ref[0], gl_ref[0], biasc_ref[0, :, 0], 0, 0)


    def lane_max(sc):
        mx = sc[:, 0:LANES]
        for c in range(1, n_lt):
            mx = jnp.maximum(mx, sc[:, c * LANES:(c + 1) * LANES])
        return mx

    def probs(sc, m_rep):
        return jnp.concatenate([jnp.exp2(sc[:, c * LANES:(c + 1) * LANES] - m_rep)
                                for c in range(n_lt)], axis=1).astype(BF16)

    def row_max(mx):
        return jnp.broadcast_to(jnp.max(mx, axis=-1, keepdims=True), (rows, LANES))

    ri = lax.broadcasted_iota(jnp.int32, (rows, tq), 0) & (tq - 1)
    ci = lax.broadcasted_iota(jnp.int32, (rows, tq), 1)

    n_tiles = kv_ref.shape[1] // tq

    def attend(n):
        def branch():
            tile = lambda j: slice(j * tq, (j + 1) * tq)
            slot = n % 2
            if n + 1 < n_tiles:
                prepare(qn_ref[0], gln_ref[0], biascn_ref[0, :, 0], (n + 1) * tq, 1 - slot)
            mx_s = None
            for j in range(n + 1):
                sc = _dot_nt(qsa_ref[slot], kaug_ref[tile(j), :])
                if j == n:
                    sc = sc + biasn_ref[0, 0]
                elif j == n - 1:
                    sc = sc + biasn_ref[0, 1]
                sbuf_ref[:, tile(j)] = sc
                mx_s = lane_max(sc) if mx_s is None else jnp.maximum(mx_s, lane_max(sc))
            win_tiles = [j for j in (n - 2, n - 1, n) if j >= 0]
            mx_w = None
            for w, j in enumerate(win_tiles):
                if j == n:
                    add = biasn_ref[0, 0]
                elif j == n - 1:
                    add = biasn_ref[0, 1]
                else:
                    add = jnp.where(ci > ri, 0.0, NEG)
                sc = _dot_nt(qsa_ref[slot, :, 0:hd], kv_ref[0, tile(j), 2 * hd:3 * hd]) + add
                wbuf_ref[:, tile(w)] = sc
                mx_w = lane_max(sc) if mx_w is None else jnp.maximum(mx_w, lane_max(sc))
            m_sel = row_max(mx_s)
            m_win = row_max(mx_w)
            acc_s = None
            for j in range(n + 1):
                d = _dot(probs(sbuf_ref[:, tile(j)], m_sel), vaug_ref[tile(j), 0:LANES])
                acc_s = d if acc_s is None else acc_s + d
            acc_w = None
            for w, j in enumerate(win_tiles):
                d = _dot(probs(wbuf_ref[:, tile(w)], m_win), vaug_ref[tile(j), LANES:2 * LANES])
                acc_w = d if acc_w is None else acc_w + d
            o = (gate(0, slot) * part_ref[slot] + gate(2, slot) * acc_w / pltpu.roll(acc_w, hd, 1)
                 + gate(1, slot) * acc_s / pltpu.roll(acc_s, hd, 1))
            o_ref[0] = jnp.concatenate([o[i * tq:(i + 1) * tq, :hd] for i in range(r)],
                                       axis=1).astype(o_ref.dtype)
        return branch

    lax.switch(qi, [attend(n) for n in range(n_tiles)])


def nsa_attention(p, kvc, gexp, biasn, biasc, ovt, expand, *, tq, q_block, kv_block, gl_block):
    b, s, _ = p.shape
    g = N_KV_GROUPS
    nq = s // tq
    rows = HEADS_PER_GROUP * tq
    assert WINDOW == 2 * tq and tq % LANES == 0
    return pl.pallas_call(
        functools.partial(_nsa_kernel, tq=tq),
        grid=(b, g, nq),
        in_specs=[pl.BlockSpec((1, tq, GROUP_WIDTH), lambda bi, gi, qi: (bi, qi, q_block + gi)),
                  pl.BlockSpec((1, s, GROUP_WIDTH), lambda bi, gi, qi: (bi, 0, kv_block + gi)),
                  pl.BlockSpec((1, 1, N_CMP_PAD, 2 * HEAD_DIM), lambda bi, gi, qi: (bi, gi, 0, 0)),
                  pl.BlockSpec((1, tq, LANES), lambda bi, gi, qi: (bi, qi, gl_block)),
                  pl.BlockSpec((1,) + gexp.shape[1:], lambda bi, gi, qi: (gi, 0, 0)),
                  pl.BlockSpec((1, 2, rows, tq), lambda bi, gi, qi: (gi, 0, 0, 0)),
                  pl.BlockSpec((1, HEADS_PER_GROUP, 1, tq, N_CMP_PAD), lambda bi, gi, qi: (gi, 0, qi, 0, 0)),
                  pl.BlockSpec(ovt.shape, lambda bi, gi, qi: (0, 0)),
                  pl.BlockSpec(expand.shape, lambda bi, gi, qi: (0, 0)),
                  pl.BlockSpec((1, tq, GROUP_WIDTH),
                               lambda bi, gi, qi: (bi, jnp.minimum(qi + 1, nq - 1), q_block + gi)),
                  pl.BlockSpec((1, tq, LANES), lambda bi, gi, qi: (bi, jnp.minimum(qi + 1, nq - 1), gl_block)),
                  pl.BlockSpec((1, HEADS_PER_GROUP, 1, tq, N_CMP_PAD),
                               lambda bi, gi, qi: (gi, 0, jnp.minimum(qi + 1, nq - 1), 0, 0))],
        out_specs=pl.BlockSpec((1, tq, GROUP_WIDTH), lambda bi, gi, qi: (bi, qi, gi)),
        out_shape=jax.ShapeDtypeStruct((b, s, g * GROUP_WIDTH), BF16),
        scratch_shapes=[pltpu.VMEM((2, rows, LANES), BF16),
                        pltpu.VMEM((s, LANES), BF16),
                        pltpu.VMEM((s, 2 * LANES), BF16),
                        pltpu.VMEM((rows, s), F32),
                        pltpu.VMEM((rows, 3 * tq), F32),
                        pltpu.VMEM((2, tq, LANES), F32),
                        pltpu.VMEM((2, rows, LANES), F32)],
        compiler_params=_params(3),
        name="nsa_attention",
    )(p, p, kvc, p, gexp, biasn, biasc, ovt, expand, p, p, biasc)


def _merge_xattn_kernel(gc_ref, ga_ref, conv_ref, nsa_ref, wm_ref, x_ref, g_ref, wq_ref, kv_ref, wo_ref,
                        o_ref):
    y = (_sigmoid(gc_ref[0].astype(F32)) * conv_ref[0].astype(F32)
         + _sigmoid(ga_ref[0].astype(F32)) * nsa_ref[0].astype(F32))
    x = x_ref[0] + _dot(y.astype(BF16), wm_ref[...])
    d = x.shape[-1]
    hd = d // X_HEADS
    h = _rms_norm(x, g_ref[...]).astype(BF16)
    q = (_dot(h, wq_ref[...]) * (1.0 / math.sqrt(hd))).astype(BF16)
    kv = kv_ref[0]
    outs = []
    for i in range(X_HEADS):
        s = _dot_nt(q[:, i * hd:(i + 1) * hd], kv[:, i * hd:(i + 1) * hd])
        p = jnp.exp(s - jnp.max(s, axis=-1, keepdims=True))
        l = jnp.sum(p, axis=-1, keepdims=True)
        outs.append(_dot(p.astype(BF16), kv[:, d + i * hd:d + (i + 1) * hd]) / l)
    o = jnp.concatenate(outs, axis=1).astype(BF16)
    o_ref[0] = x + _dot(o, wo_ref[...])


def merge_xattn_block(p3, conv, nsa, wm, x, g, wq, kv, wo, layer, *, tm, gc_block, ga_block):
    b, s, d = x.shape
    mlen = kv.shape[1]
    tile = lambda blk: pl.BlockSpec((1, tm, d), lambda bi, si: (bi, si, blk))
    mat = lambda: pl.BlockSpec((None, d, d), lambda bi, si: (layer, 0, 0))
    return pl.pallas_call(
        _merge_xattn_kernel,
        grid=(b, s // tm),
        in_specs=[tile(gc_block), tile(ga_block), tile(0), tile(0), mat(), tile(0),
                  pl.BlockSpec((1, d), lambda bi, si: (0, 0)),
                  mat(),
                  pl.BlockSpec((1, mlen, 2 * d), lambda bi, si: (bi, 0, 0)),
                  mat()],
        out_specs=tile(0),
        out_shape=jax.ShapeDtypeStruct((b, s, d), F32),
        compiler_params=_params(2),
        name="merge_xattn",
    )(p3, p3, conv, nsa, wm, x, g, wq, kv, wo)


def _ffn_kernel(x_ref, g_ref, wa_ref, wb_ref, wo_ref, gout_ref, o_ref, *, norm_out):
    x = x_ref[...]
    xn = _rms_norm(x, g_ref[...]).astype(BF16)
    a = _dot(xn, wa_ref[...])
    bb = _dot(xn, wb_ref[...])
    y = x + _dot((a * _sigmoid(a) * bb).astype(BF16), wo_ref[...])
    o_ref[...] = _rms_norm(y, gout_ref[...]) if norm_out else y


def ffn_block(x, g, w_in, w_out, g_out, layer, *, tm, norm_out):
    m, d = x.shape
    f = w_out.shape[1]
    once = pl.Buffered(1)
    return pl.pallas_call(
        functools.partial(_ffn_kernel, norm_out=norm_out),
        grid=(m // tm,),
        in_specs=[pl.BlockSpec((tm, d), lambda i: (i, 0)),
                  pl.BlockSpec((1, d), lambda i: (0, 0)),
                  pl.BlockSpec((None, d, f), lambda i: (layer, 0, 0), pipeline_mode=once),
                  pl.BlockSpec((None, d, f), lambda i: (layer, 0, 1), pipeline_mode=once),
                  pl.BlockSpec((None, f, d), lambda i: (layer, 0, 0), pipeline_mode=once),
                  pl.BlockSpec((1, d), lambda i: (0, 0))],
        out_specs=pl.BlockSpec((tm, d), lambda i: (i, 0)),
        out_shape=jax.ShapeDtypeStruct((m, d), F32),
        compiler_params=_params(1),
        name="ffn_block",
    )(x, g, w_in, w_in, w_out, g_out)


def _t5_bucket(dist):
    n = jnp.maximum(dist, 0)
    max_exact = REL_BUCKETS // 2
    nf = jnp.maximum(n, 1).astype(F32)
    large = max_exact + (jnp.log(nf / max_exact) / math.log(REL_MAX_DIST / max_exact)
                         * (REL_BUCKETS - max_exact)).astype(jnp.int32)
    large = jnp.minimum(large, REL_BUCKETS - 1)
    return jnp.where(n < max_exact, n, large)


def _bias_lookup(rel_bias, dist):
    onehot = jax.nn.one_hot(_t5_bucket(dist), REL_BUCKETS, dtype=F32)
    return jnp.einsum("bh,...b->h...", rel_bias, onehot, precision=lax.Precision.HIGHEST)


def _bias_tables(rel_bias, s, tq):
    g, r = N_KV_GROUPS, HEADS_PER_GROUP
    nq = s // tq
    rel_bias = rel_bias * LOG2E
    far = _bias_lookup(rel_bias, jnp.asarray(s, jnp.int32))
    i = jnp.arange(tq)[:, None]
    j = jnp.arange(tq)[None, :]

    def near(dist):
        bt = _bias_lookup(rel_bias, dist) - far[:, None, None]
        bt = jnp.where((dist >= 0)[None], bt, NEG)
        return bt.reshape(g, r * tq, tq)

    biasn = jnp.stack([near(i - j), near(i - j + tq)], axis=1)

    t = jnp.arange(s)[:, None]
    c_end = jnp.arange(N_CMP_PAD)[None, :] * CMP_STRIDE + CMP_BLOCK - 1
    bc = _bias_lookup(rel_bias, t - c_end) + jnp.where(t >= c_end, 0.0, NEG)[None]
    biasc = bc.reshape(g, r, nq, tq, N_CMP_PAD)
    return biasn, biasc


def _static_tables(s, tq):
    n_sel = s // SEL_BLOCK
    jj = np.arange(N_CMP_PAD)[None, :] * CMP_STRIDE
    mm0 = np.arange(n_sel)[:, None] * SEL_BLOCK
    n_cmp = (s - CMP_BLOCK) // CMP_STRIDE + 1
    ovt = ((jj < mm0 + SEL_BLOCK) & (jj + CMP_BLOCK > mm0) & (np.arange(N_CMP_PAD)[None, :] < n_cmp))
    assert n_sel <= HEAD_DIM
    expand = np.zeros((s, HEAD_DIM), np.float32)
    expand[np.arange(s), np.arange(s) // SEL_BLOCK] = NEG
    gexp = np.zeros((N_KV_GROUPS, LANES, LANES), np.float32)
    for g in range(N_KV_GROUPS):
        for j in range(3):
            for i in range(HEADS_PER_GROUP):
                gexp[g, j * N_HEADS + g * HEADS_PER_GROUP + i, j * HEADS_PER_GROUP + i] = 1.0
    d = np.arange(LANES + 1, s + 1).astype(np.float32)
    big = 16 + (np.log(d / 16) / math.log(REL_MAX_DIST / 16) * 16).astype(np.int32)
    assert np.all(np.minimum(big, REL_BUCKETS - 1) == REL_BUCKETS - 1)
    return (jnp.asarray(ovt.astype(np.float32), BF16), jnp.asarray(expand, BF16), jnp.asarray(gexp, BF16))


def _in_proj_perm(d):
    g, hd = N_KV_GROUPS, HEAD_DIM
    kvw = g * hd
    o_conv, o_q = 0, 2 * d
    o_kc = o_q + N_HEADS * hd
    o_vc, o_ks, o_vs, o_kw, o_vw = (o_kc + i * kvw for i in range(1, 6))
    o_gate = o_kc + 6 * kvw
    o_gc = o_gate + 3 * N_HEADS
    o_ga = o_gc + d
    segs = [(o_conv, 2 * d, None), (o_q, N_HEADS * hd, LOG2E / math.sqrt(hd))]
    for gi in range(g):
        segs += [(base + gi * hd, hd, None) for base in (o_ks, o_vs, o_kw, o_vw)]
    segs += [(o_gc, 2 * d, None), (o_kc, 2 * kvw, None), (o_gate, 3 * N_HEADS, None)]
    n_used = sum(n for _, n, _ in segs)
    return segs, (-n_used) % LANES


def _permute_in_proj(w_in):
    segs, n_pad = _in_proj_perm(w_in.shape[1])
    wt = jnp.swapaxes(w_in, 1, 2)
    parts = [(wt[:, a:a + n] if sc is None else wt[:, a:a + n] * sc).astype(BF16) for a, n, sc in segs]
    parts.append(jnp.zeros((wt.shape[0], n_pad, wt.shape[2]), BF16))
    return jnp.concatenate(parts, axis=1)


def kernel(x, mem, norm_mix_g, w_in, conv_dw_w, conv_dw_b, conv_ln_g, conv_ln_b, conv_pw_w, cmp_pos,
           cmp_w1, cmp_w2, w_out, norm_x_g, xq_w, xkv_w, xo_w, norm_ffn_g, ffn_in_w, ffn_out_w,
           rel_bias, final_norm_g):
    b, s, d = x.shape
    depth = w_in.shape[0]
    mlen = mem.shape[1]
    m = b * s
    tq = NSA_Q_TILE

    w_in_pt = _permute_in_proj(w_in)
    n_p = w_in_pt.shape[1]
    q_block = (2 * d) // GROUP_WIDTH
    kv_block = q_block + N_KV_GROUPS
    gc_block = (3 * d + N_KV_GROUPS * GROUP_WIDTH) // d
    ga_block = gc_block + 1
    cmp_block = (6 * d) // (2 * N_KV_GROUPS * HEAD_DIM)
    gl_block = (6 * d + 2 * N_KV_GROUPS * HEAD_DIM) // LANES

    ovt, expand, gexp = _static_tables(s, tq)
    biasn, biasc = _bias_tables(rel_bias, s, tq)

    row = lambda v: v.reshape(1, -1)
    pw_b, wm_b, wq_b, wkv_b, wo_b, wfi_b, wfo_b = (
        w.astype(BF16) for w in (conv_pw_w, w_out, xq_w, xkv_w, xo_w, ffn_in_w, ffn_out_w))

    xf = x.reshape(m, d)
    for l in range(depth):
        p = norm_matmul(xf, row(norm_mix_g[l]), w_in_pt, l, tm=ROW_TILE)
        p3 = p.reshape(b, s, n_p)
        conv = conv_module(p3, conv_dw_w[l], row(conv_dw_b[l]), row(conv_ln_g[l]), row(conv_ln_b[l]),
                           pw_b, l, t=ROW_TILE)
        pos4 = jnp.tile(cmp_pos[l], (1, 1, N_KV_GROUPS))
        kvc = compress(p3, cmp_block, pos4, cmp_w1[l].astype(BF16), cmp_w2[l].astype(BF16))
        nsa = nsa_attention(p3, kvc, gexp, biasn, biasc, ovt, expand, tq=tq,
                            q_block=q_block, kv_block=kv_block, gl_block=gl_block)
        kvx = matmul(mem.reshape(b * mlen, d), wkv_b, l, tm=mlen)
        xf = merge_xattn_block(p3, conv, nsa, wm_b, xf.reshape(b, s, d), row(norm_x_g[l]), wq_b,
                               kvx.reshape(b, mlen, 2 * d), wo_b, l, tm=ROW_TILE, gc_block=gc_block,
                               ga_block=ga_block).reshape(m, d)
        xf = ffn_block(xf, row(norm_ffn_g[l]), wfi_b, wfo_b, row(final_norm_g), l,
                       tm=ROW_TILE, norm_out=(l == depth - 1))
    return xf.reshape(b, s, d)
```

```python
import functools
import math

import numpy as np
import jax
import jax.numpy as jnp
from jax import lax
from jax.experimental import pallas as pl
from jax.experimental.pallas import tpu as pltpu

F32 = jnp.float32
BF16 = jnp.bfloat16

HEAD_DIM = 64
N_KV_GROUPS = 4
HEADS_PER_GROUP = 4
N_HEADS = N_KV_GROUPS * HEADS_PER_GROUP
GROUP_WIDTH = HEADS_PER_GROUP * HEAD_DIM
CMP_BLOCK = 32
CMP_STRIDE = 16
SEL_BLOCK = 64
SEL_TOP_N = 16
WINDOW = 512
CONV_WIDTH = 31
X_HEADS = 4
REL_BUCKETS = 32
REL_MAX_DIST = 128
EPS = 1e-6
NEG = -1e30
MAX_FLOOR = -1e20
LOG2E = math.log2(math.e)

LANES = 128
SUBLANES = 8
CONV_ROWS = 64
CONV_HALO = 32
N_CMP_PAD = 128
VMEM_LIMIT = 56 * 1024 * 1024

ROW_TILE = 512
MERGE_ROW_TILE = 1024
NSA_Q_TILE = WINDOW // 2


def _sigmoid(x):
    return 0.5 * jnp.tanh(0.5 * x) + 0.5


def _dot(a, b):
    return jnp.dot(a, b, preferred_element_type=F32)


def _dot_nt(a, b):
    return lax.dot_general(a, b, (((1,), (1,)), ((), ())), preferred_element_type=F32)


def _rms_norm(x, g):
    return x * lax.rsqrt(jnp.mean(x * x, axis=-1, keepdims=True) + EPS) * g


def _params(n_axes):
    return pltpu.CompilerParams(dimension_semantics=("arbitrary",) * n_axes,
                                vmem_limit_bytes=VMEM_LIMIT)


def _norm_matmul_kernel(x_ref, g_ref, wt_ref, o_ref):
    xn = _rms_norm(x_ref[...], g_ref[...]).astype(BF16)
    o_ref[...] = _dot_nt(xn, wt_ref[...]).astype(o_ref.dtype)


def norm_matmul(x, g, wt, layer, *, tm):
    m, k = x.shape
    n = wt.shape[1]
    return pl.pallas_call(
        _norm_matmul_kernel,
        grid=(m // tm,),
        in_specs=[pl.BlockSpec((tm, k), lambda i: (i, 0)),
                  pl.BlockSpec((1, k), lambda i: (0, 0)),
                  pl.BlockSpec((None, n, k), lambda i: (layer, 0, 0), pipeline_mode=pl.Buffered(1))],
        out_specs=pl.BlockSpec((tm, n), lambda i: (i, 0)),
        out_shape=jax.ShapeDtypeStruct((m, n), BF16),
        compiler_params=_params(1),
        name="norm_matmul",
    )(x, g, wt)


def _matmul_kernel(x_ref, w_ref, o_ref):
    o_ref[...] = _dot(x_ref[...].astype(BF16), w_ref[...]).astype(o_ref.dtype)


def matmul(x, w, layer, *, tm):
    m, k = x.shape
    n = w.shape[2]
    return pl.pallas_call(
        _matmul_kernel,
        grid=(m // tm,),
        in_specs=[pl.BlockSpec((tm, k), lambda i: (i, 0)),
                  pl.BlockSpec((None, k, n), lambda i: (layer, 0, 0))],
        out_specs=pl.BlockSpec((tm, n), lambda i: (i, 0)),
        out_shape=jax.ShapeDtypeStruct((m, n), BF16),
        compiler_params=_params(1),
        name="matmul",
    )(x, w)


def _conv_kernel(a_ref, gt_ref, dww_ref, dwb_ref, lng_ref, lnb_ref, pw_ref, o_ref, buf_ref, cbuf_ref, *, t):
    s = pl.program_id(1)

    @pl.when(s == 0)
    def _():
        buf_ref[0:CONV_HALO, :] = jnp.zeros((CONV_HALO, buf_ref.shape[1]), F32)

    @pl.when(s > 0)
    def _():
        buf_ref[0:CONV_HALO, :] = buf_ref[t:t + CONV_HALO, :]

    a = a_ref[0].astype(F32)
    gt = gt_ref[0].astype(F32)
    buf_ref[CONV_HALO:CONV_HALO + t, :] = a * _sigmoid(gt)

    base = CONV_HALO - (CONV_WIDTH - 1)

    def conv_rows(ci, carry):
        r0 = pl.multiple_of(ci * CONV_ROWS, CONV_ROWS)
        for c in range(buf_ref.shape[1] // LANES):
            lanes = slice(c * LANES, (c + 1) * LANES)
            acc = dwb_ref[:, lanes]
            for b in range(SUBLANES):
                n = CONV_ROWS + (SUBLANES if b else 0)
                part = None
                for k in range(CONV_WIDTH):
                    if (base + k) % SUBLANES != b:
                        continue
                    term = dww_ref[k:k + 1, lanes] * buf_ref[pl.ds(r0 + (base + k - b), n), lanes]
                    part = term if part is None else part + term
                acc = acc + part[b:b + CONV_ROWS]
            cbuf_ref[pl.ds(r0, CONV_ROWS), lanes] = acc
        return carry

    lax.fori_loop(0, t // CONV_ROWS, conv_rows, 0)
    v = cbuf_ref[...]

    mu = jnp.mean(v, axis=-1, keepdims=True)
    vc = v - mu
    y = vc * lax.rsqrt(jnp.mean(vc * vc, axis=-1, keepdims=True) + EPS)
    y = y * lng_ref[...] + lnb_ref[...]
    y = y * _sigmoid(y)
    o_ref[0] = _dot(y.astype(BF16), pw_ref[...]).astype(o_ref.dtype)


def conv_module(p, dww, dwb, lng, lnb, pw, layer, *, t):
    b, s, _ = p.shape
    c = pw.shape[1]
    vec = lambda: pl.BlockSpec((1, c), lambda bi, si: (0, 0))
    return pl.pallas_call(
        functools.partial(_conv_kernel, t=t),
        grid=(b, s // t),
        in_specs=[pl.BlockSpec((1, t, c), lambda bi, si: (bi, si, 0)),
                  pl.BlockSpec((1, t, c), lambda bi, si: (bi, si, 1)),
                  pl.BlockSpec((CONV_WIDTH, c), lambda bi, si: (0, 0)),
                  vec(), vec(), vec(),
                  pl.BlockSpec((None, c, c), lambda bi, si: (layer, 0, 0))],
        out_specs=pl.BlockSpec((1, t, c), lambda bi, si: (bi, si, 0)),
        out_shape=jax.ShapeDtypeStruct((b, s, c), BF16),
        scratch_shapes=[pltpu.VMEM((CONV_HALO + t, c), F32), pltpu.VMEM((t, c), F32)],
        compiler_params=_params(2),
        name="conv_module",
    )(p, p, dww, dwb, lng, lnb, pw)


def _compress_kernel(x_ref, pos_ref, w1s_ref, w2s_ref, o_ref, xf_ref, w1_ref, w2_ref):
    half = CMP_BLOCK // 2
    gw = N_KV_GROUPS * HEAD_DIM
    n_chunk = gw // LANES

    @pl.when(pl.program_id(0) == 0)
    def _():
        w1_ref[...] = jnp.zeros(w1_ref.shape, BF16)
        w2_ref[...] = jnp.zeros(w2_ref.shape, BF16)
        for g in range(N_KV_GROUPS):
            blk = slice(g * HEAD_DIM, (g + 1) * HEAD_DIM)
            w1_ref[:, :, blk, blk] = w1s_ref[...]
            w2_ref[:, blk, blk] = w2s_ref[...]

    for c in range(2 * n_chunk):
        xf_ref[c] = x_ref[0, :, c * LANES:(c + 1) * LANES].astype(F32)
    for kv in range(2):
        acc_a = jnp.zeros((N_CMP_PAD, gw), F32)
        acc_b = jnp.zeros((N_CMP_PAD, gw), F32)
        for l in range(half):
            rows = jnp.concatenate(
                [xf_ref[kv * n_chunk + c, pl.ds(l, N_CMP_PAD, stride=CMP_STRIDE), :] for c in range(n_chunk)],
                axis=1)
            ra = (rows + pos_ref[kv, l:l + 1, :]).astype(BF16)
            rb = (rows + pos_ref[kv, half + l:half + l + 1, :]).astype(BF16)
            acc_a = acc_a + _dot(ra, w1_ref[kv, l])
            acc_b = acc_b + _dot(rb, w1_ref[kv, half + l])
        z = acc_a + pltpu.roll(acc_b, N_CMP_PAD - 1, 0)
        h = z * _sigmoid(z)
        out = _dot(h.astype(BF16), w2_ref[kv]).astype(o_ref.dtype)
        for g in range(N_KV_GROUPS):
            o_ref[0, g, :, (1 - kv) * HEAD_DIM:(2 - kv) * HEAD_DIM] = out[:, g * HEAD_DIM:(g + 1) * HEAD_DIM]


def compress(p, col_block, pos4, w1, w2):
    b, s, _ = p.shape
    assert s == N_CMP_PAD * CMP_STRIDE
    gw = N_KV_GROUPS * HEAD_DIM
    return pl.pallas_call(
        _compress_kernel,
        grid=(b,),
        in_specs=[pl.BlockSpec((1, s, 2 * gw), lambda bi: (bi, 0, col_block)),
                  pl.BlockSpec(pos4.shape, lambda bi: (0, 0, 0)),
                  pl.BlockSpec(w1.shape, lambda bi: (0, 0, 0, 0)),
                  pl.BlockSpec(w2.shape, lambda bi: (0, 0, 0))],
        out_specs=pl.BlockSpec((1, N_KV_GROUPS, N_CMP_PAD, 2 * HEAD_DIM), lambda bi: (bi, 0, 0, 0)),
        out_shape=jax.ShapeDtypeStruct((b, N_KV_GROUPS, N_CMP_PAD, 2 * HEAD_DIM), BF16),
        scratch_shapes=[pltpu.VMEM((2 * gw // LANES, s, LANES), F32),
                        pltpu.VMEM((2, CMP_BLOCK, gw, gw), BF16),
                        pltpu.VMEM((2, gw, gw), BF16)],
        compiler_params=_params(1),
        name="nsa_compress",
    )(p, pos4, w1, w2)


def _nsa_kernel(q_ref, kv_ref, kvc_ref, gl_ref, gexp_ref, biasn_ref, biasc_ref, ovt_ref, kmask_ref,
                qn_ref, gln_ref, biascn_ref, o_ref, qsa_ref, kaug_ref, vaug_ref, sbuf_ref, wbuf_ref, sig_ref, part_ref,
                *, tq):
    qi = pl.program_id(2)
    r = HEADS_PER_GROUP
    rows = r * tq
    hd = HEAD_DIM
    n_lt = tq // LANES

    def prepare(q, gl, bc, t_base, slot):
        qs = jnp.concatenate([q[:, i * hd:(i + 1) * hd] for i in range(r)], axis=0)

        sig_ref[slot] = _sigmoid(_dot(gl, gexp_ref[0]))

        kvc = kvc_ref[0, 0]
        s = _dot_nt(qs, kvc[:, hd:]) + bc.reshape(rows, N_CMP_PAD)
        m = jnp.maximum(jnp.max(s, axis=-1, keepdims=True), MAX_FLOOR)
        p = jnp.exp2(s - m)
        p = p / jnp.maximum(jnp.sum(p, axis=-1, keepdims=True), 1e-30)
        part_ref[slot] = _dot(p.astype(BF16), kvc)

        ps = p[0:tq] + p[tq:2 * tq] + p[2 * tq:3 * tq] + p[3 * tq:4 * tq]
        ps_hi = ps.astype(BF16)
        ps_lo = (ps - ps_hi.astype(F32)).astype(BF16)
        ovt = ovt_ref[...]
        n_sel = ovt.shape[0]
        imp_t = _dot_nt(ovt, ps_hi) + _dot_nt(ovt, ps_lo)
        mi = lax.broadcasted_iota(jnp.int32, (n_sel, tq), 0)
        blk = (t_base + lax.broadcasted_iota(jnp.int32, (n_sel, tq), 1)) >> int(math.log2(SEL_BLOCK))
        forced = (mi == 0) | (mi == blk) | (mi == blk - 1)
        score = jnp.where(forced, jnp.inf, jnp.where(mi <= blk, imp_t, -jnp.inf))
        n_grp = n_sel // SUBLANES
        grp = [score[SUBLANES * a:SUBLANES * (a + 1)] for a in range(n_grp)]
        cnt = [jnp.zeros((SUBLANES, tq), F32) for _ in range(n_grp)]
        for mp in range(n_sel):
            other = score[mp:mp + 1, :]
            for a in range(n_grp):
                if SUBLANES * a > mp:
                    ahead = other >= grp[a]
                elif SUBLANES * (a + 1) - 1 <= mp:
                    ahead = other > grp[a]
                else:
                    later = (lax.broadcasted_iota(jnp.int32, (SUBLANES, tq), 0) + SUBLANES * a) > mp
                    ahead = (other > grp[a]) | ((other == grp[a]) & later)
                cnt[a] = cnt[a] + jnp.where(ahead, 1.0, 0.0)
        cnt = jnp.concatenate(cnt, axis=0)
        drop_t = jnp.where(cnt < float(min(SEL_TOP_N, n_sel)), 0.0, 1.0)
        drop_t = jnp.concatenate([jnp.zeros((hd, tq), F32), drop_t,
                                  jnp.zeros((LANES - hd - n_sel, tq), F32)], axis=0)
        drop = drop_t.T.astype(BF16)[:, hd:]
        qsa_ref[slot] = jnp.concatenate(
            [jnp.concatenate([q[:, i * hd:(i + 1) * hd], drop], axis=1) for i in range(r)], axis=0)

    def gate(j, slot):
        sig = sig_ref[slot]
        return jnp.concatenate(
            [jnp.broadcast_to(sig[:, j * r + i:j * r + i + 1], (tq, LANES)) for i in range(r)],
            axis=0)

    @pl.when(qi == 0)
    def _():
        ones = jnp.ones((kv_ref.shape[1], hd), BF16)
        vaug_ref[...] = jnp.concatenate(
            [kv_ref[0, :, hd:2 * hd], ones, kv_ref[0, :, 3 * hd:4 * hd], ones], axis=1)
        kaug_ref[...] = jnp.concatenate([kv_ref[0, :, 0:hd], kmask_ref[...]], axis=1)
        prepare(q_ref[0], gl_ref[0], biasc_ref[0, :, 0], 0, 0)


    def lane_max(sc):
        mx = sc[:, 0:LANES]
        for c in range(1, n_lt):
            mx = jnp.maximum(mx, sc[:, c * LANES:(c + 1) * LANES])
        return mx

    def probs(sc, m_rep):
        return jnp.concatenate([jnp.exp2(sc[:, c * LANES:(c + 1) * LANES] - m_rep)
                                for c in range(n_lt)], axis=1).astype(BF16)

    def row_max(mx):
        return jnp.broadcast_to(jnp.max(mx, axis=-1, keepdims=True), (rows, LANES))

    ri = lax.broadcasted_iota(jnp.int32, (rows, tq), 0) & (tq - 1)
    ci = lax.broadcasted_iota(jnp.int32, (rows, tq), 1)

    n_tiles = kv_ref.shape[1] // tq

    def attend(n):
        def branch():
            tile = lambda j: slice(j * tq, (j + 1) * tq)
            slot = n % 2
            if n + 1 < n_tiles:
                prepare(qn_ref[0], gln_ref[0], biascn_ref[0, :, 0], (n + 1) * tq, 1 - slot)
            mx_s = None
            for j in range(n + 1):
                sc = _dot_nt(qsa_ref[slot], kaug_ref[tile(j), :])
                if j == n:
                    sc = sc + biasn_ref[0, 0]
                elif j == n - 1:
                    sc = sc + biasn_ref[0, 1]
                sbuf_ref[:, tile(j)] = sc
                mx_s = lane_max(sc) if mx_s is None else jnp.maximum(mx_s, lane_max(sc))
            win_tiles = [j for j in (n - 2, n - 1, n) if j >= 0]
            mx_w = None
            for w, j in enumerate(win_tiles):
                if j == n:
                    add = biasn_ref[0, 0]
                elif j == n - 1:
                    add = biasn_ref[0, 1]
                else:
                    add = jnp.where(ci > ri, 0.0, NEG)
                sc = _dot_nt(qsa_ref[slot, :, 0:hd], kv_ref[0, tile(j), 2 * hd:3 * hd]) + add
                wbuf_ref[:, tile(w)] = sc
                mx_w = lane_max(sc) if mx_w is None else jnp.maximum(mx_w, lane_max(sc))
            m_sel = row_max(mx_s)
            m_win = row_max(mx_w)
            acc_s = None
            for j in range(n + 1):
                d = _dot(probs(sbuf_ref[:, tile(j)], m_sel), vaug_ref[tile(j), 0:LANES])
                acc_s = d if acc_s is None else acc_s + d
            acc_w = None
            for w, j in enumerate(win_tiles):
                d = _dot(probs(wbuf_ref[:, tile(w)], m_win), vaug_ref[tile(j), LANES:2 * LANES])
                acc_w = d if acc_w is None else acc_w + d
            o = (gate(0, slot) * part_ref[slot] + gate(2, slot) * acc_w / pltpu.roll(acc_w, hd, 1)
                 + gate(1, slot) * acc_s / pltpu.roll(acc_s, hd, 1))
            o_ref[0] = jnp.concatenate([o[i * tq:(i + 1) * tq, :hd] for i in range(r)],
                                       axis=1).astype(o_ref.dtype)
        return branch

    lax.switch(qi, [attend(n) for n in range(n_tiles)])


def nsa_attention(p, kvc, gexp, biasn, biasc, ovt, expand, *, tq, q_block, kv_block, gl_block):
    b, s, _ = p.shape
    g = N_KV_GROUPS
    nq = s // tq
    rows = HEADS_PER_GROUP * tq
    assert WINDOW == 2 * tq and tq % LANES == 0
    return pl.pallas_call(
        functools.partial(_nsa_kernel, tq=tq),
        grid=(b, g, nq),
        in_specs=[pl.BlockSpec((1, tq, GROUP_WIDTH), lambda bi, gi, qi: (bi, qi, q_block + gi)),
                  pl.BlockSpec((1, s, GROUP_WIDTH), lambda bi, gi, qi: (bi, 0, kv_block + gi)),
                  pl.BlockSpec((1, 1, N_CMP_PAD, 2 * HEAD_DIM), lambda bi, gi, qi: (bi, gi, 0, 0)),
                  pl.BlockSpec((1, tq, LANES), lambda bi, gi, qi: (bi, qi, gl_block)),
                  pl.BlockSpec((1,) + gexp.shape[1:], lambda bi, gi, qi: (gi, 0, 0)),
                  pl.BlockSpec((1, 2, rows, tq), lambda bi, gi, qi: (gi, 0, 0, 0)),
                  pl.BlockSpec((1, HEADS_PER_GROUP, 1, tq, N_CMP_PAD), lambda bi, gi, qi: (gi, 0, qi, 0, 0)),
                  pl.BlockSpec(ovt.shape, lambda bi, gi, qi: (0, 0)),
                  pl.BlockSpec(expand.shape, lambda bi, gi, qi: (0, 0)),
                  pl.BlockSpec((1, tq, GROUP_WIDTH),
                               lambda bi, gi, qi: (bi, jnp.minimum(qi + 1, nq - 1), q_block + gi)),
                  pl.BlockSpec((1, tq, LANES), lambda bi, gi, qi: (bi, jnp.minimum(qi + 1, nq - 1), gl_block)),
                  pl.BlockSpec((1, HEADS_PER_GROUP, 1, tq, N_CMP_PAD),
                               lambda bi, gi, qi: (gi, 0, jnp.minimum(qi + 1, nq - 1), 0, 0))],
        out_specs=pl.BlockSpec((1, tq, GROUP_WIDTH), lambda bi, gi, qi: (bi, qi, gi)),
        out_shape=jax.ShapeDtypeStruct((b, s, g * GROUP_WIDTH), BF16),
        scratch_shapes=[pltpu.VMEM((2, rows, LANES), BF16),
                        pltpu.VMEM((s, LANES), BF16),
                        pltpu.VMEM((s, 2 * LANES), BF16),
                        pltpu.VMEM((rows, s), F32),
                        pltpu.VMEM((rows, 3 * tq), F32),
                        pltpu.VMEM((2, tq, LANES), F32),
                        pltpu.VMEM((2, rows, LANES), F32)],
        compiler_params=_params(3),
        name="nsa_attention",
    )(p, p, kvc, p, gexp, biasn, biasc, ovt, expand, p, p, biasc)


def _merge_xattn_kernel(gc_ref, ga_ref, conv_ref, nsa_ref, wm_ref, x_ref, g_ref, wq_ref, kv_ref, wo_ref,
                        o_ref):
    y = (_sigmoid(gc_ref[0].astype(F32)) * conv_ref[0].astype(F32)
         + _sigmoid(ga_ref[0].astype(F32)) * nsa_ref[0].astype(F32))
    x = x_ref[0] + _dot(y.astype(BF16), wm_ref[...])
    d = x.shape[-1]
    hd = d // X_HEADS
    h = _rms_norm(x, g_ref[...]).astype(BF16)
    q = (_dot(h, wq_ref[...]) * (1.0 / math.sqrt(hd))).astype(BF16)
    kv = kv_ref[0]
    outs = []
    for i in range(X_HEADS):
        s = _dot_nt(q[:, i * hd:(i + 1) * hd], kv[:, i * hd:(i + 1) * hd])
        p = jnp.exp(s - jnp.max(s, axis=-1, keepdims=True))
        l = jnp.sum(p, axis=-1, keepdims=True)
        outs.append(_dot(p.astype(BF16), kv[:, d + i * hd:d + (i + 1) * hd]) / l)
    o = jnp.concatenate(outs, axis=1).astype(BF16)
    o_ref[0] = x + _dot(o, wo_ref[...])


def merge_xattn_block(p3, conv, nsa, wm, x, g, wq, kv, wo, layer, *, tm, gc_block, ga_block):
    b, s, d = x.shape
    mlen = kv.shape[1]
    tile = lambda blk: pl.BlockSpec((1, tm, d), lambda bi, si: (bi, si, blk))
    mat = lambda: pl.BlockSpec((None, d, d), lambda bi, si: (layer, 0, 0))
    return pl.pallas_call(
        _merge_xattn_kernel,
        grid=(b, s // tm),
        in_specs=[tile(gc_block), tile(ga_block), tile(0), tile(0), mat(), tile(0),
                  pl.BlockSpec((1, d), lambda bi, si: (0, 0)),
                  mat(),
                  pl.BlockSpec((1, mlen, 2 * d), lambda bi, si: (bi, 0, 0)),
                  mat()],
        out_specs=tile(0),
        out_shape=jax.ShapeDtypeStruct((b, s, d), F32),
        compiler_params=_params(2),
        name="merge_xattn",
    )(p3, p3, conv, nsa, wm, x, g, wq, kv, wo)


def _ffn_kernel(x_ref, g_ref, wa_ref, wb_ref, wo_ref, gout_ref, o_ref, *, norm_out):
    x = x_ref[...]
    xn = _rms_norm(x, g_ref[...]).astype(BF16)
    a = _dot(xn, wa_ref[...])
    bb = _dot(xn, wb_ref[...])
    y = x + _dot((a * _sigmoid(a) * bb).astype(BF16), wo_ref[...])
    o_ref[...] = _rms_norm(y, gout_ref[...]) if norm_out else y


def ffn_block(x, g, w_in, w_out, g_out, layer, *, tm, norm_out):
    m, d = x.shape
    f = w_out.shape[1]
    once = pl.Buffered(1)
    return pl.pallas_call(
        functools.partial(_ffn_kernel, norm_out=norm_out),
        grid=(m // tm,),
        in_specs=[pl.BlockSpec((tm, d), lambda i: (i, 0)),
                  pl.BlockSpec((1, d), lambda i: (0, 0)),
                  pl.BlockSpec((None, d, f), lambda i: (layer, 0, 0), pipeline_mode=once),
                  pl.BlockSpec((None, d, f), lambda i: (layer, 0, 1), pipeline_mode=once),
                  pl.BlockSpec((None, f, d), lambda i: (layer, 0, 0), pipeline_mode=once),
                  pl.BlockSpec((1, d), lambda i: (0, 0))],
        out_specs=pl.BlockSpec((tm, d), lambda i: (i, 0)),
        out_shape=jax.ShapeDtypeStruct((m, d), F32),
        compiler_params=_params(1),
        name="ffn_block",
    )(x, g, w_in, w_in, w_out, g_out)


def _t5_bucket(dist):
    n = jnp.maximum(dist, 0)
    max_exact = REL_BUCKETS // 2
    nf = jnp.maximum(n, 1).astype(F32)
    large = max_exact + (jnp.log(nf / max_exact) / math.log(REL_MAX_DIST / max_exact)
                         * (REL_BUCKETS - max_exact)).astype(jnp.int32)
    large = jnp.minimum(large, REL_BUCKETS - 1)
    return jnp.where(n < max_exact, n, large)


def _bias_lookup(rel_bias, dist):
    onehot = jax.nn.one_hot(_t5_bucket(dist), REL_BUCKETS, dtype=F32)
    return jnp.einsum("bh,...b->h...", rel_bias, onehot, precision=lax.Precision.HIGHEST)


def _bias_tables(rel_bias, s, tq):
    g, r = N_KV_GROUPS, HEADS_PER_GROUP
    nq = s // tq
    rel_bias = rel_bias * LOG2E
    far = _bias_lookup(rel_bias, jnp.asarray(s, jnp.int32))
    i = jnp.arange(tq)[:, None]
    j = jnp.arange(tq)[None, :]

    def near(dist):
        bt = _bias_lookup(rel_bias, dist) - far[:, None, None]
        bt = jnp.where((dist >= 0)[None], bt, NEG)
        return bt.reshape(g, r * tq, tq)

    biasn = jnp.stack([near(i - j), near(i - j + tq)], axis=1)

    t = jnp.arange(s)[:, None]
    c_end = jnp.arange(N_CMP_PAD)[None, :] * CMP_STRIDE + CMP_BLOCK - 1
    bc = _bias_lookup(rel_bias, t - c_end) + jnp.where(t >= c_end, 0.0, NEG)[None]
    biasc = bc.reshape(g, r, nq, tq, N_CMP_PAD)
    return biasn, biasc


def _static_tables(s, tq):
    n_sel = s // SEL_BLOCK
    jj = np.arange(N_CMP_PAD)[None, :] * CMP_STRIDE
    mm0 = np.arange(n_sel)[:, None] * SEL_BLOCK
    n_cmp = (s - CMP_BLOCK) // CMP_STRIDE + 1
    ovt = ((jj < mm0 + SEL_BLOCK) & (jj + CMP_BLOCK > mm0) & (np.arange(N_CMP_PAD)[None, :] < n_cmp))
    assert n_sel <= HEAD_DIM
    expand = np.zeros((s, HEAD_DIM), np.float32)
    expand[np.arange(s), np.arange(s) // SEL_BLOCK] = NEG
    gexp = np.zeros((N_KV_GROUPS, LANES, LANES), np.float32)
    for g in range(N_KV_GROUPS):
        for j in range(3):
            for i in range(HEADS_PER_GROUP):
                gexp[g, j * N_HEADS + g * HEADS_PER_GROUP + i, j * HEADS_PER_GROUP + i] = 1.0
    d = np.arange(LANES + 1, s + 1).astype(np.float32)
    big = 16 + (np.log(d / 16) / math.log(REL_MAX_DIST / 16) * 16).astype(np.int32)
    assert np.all(np.minimum(big, REL_BUCKETS - 1) == REL_BUCKETS - 1)
    return (jnp.asarray(ovt.astype(np.float32), BF16), jnp.asarray(expand, BF16), jnp.asarray(gexp, BF16))


def _in_proj_perm(d):
    g, hd = N_KV_GROUPS, HEAD_DIM
    kvw = g * hd
    o_conv, o_q = 0, 2 * d
    o_kc = o_q + N_HEADS * hd
    o_vc, o_ks, o_vs, o_kw, o_vw = (o_kc + i * kvw for i in range(1, 6))
    o_gate = o_kc + 6 * kvw
    o_gc = o_gate + 3 * N_HEADS
    o_ga = o_gc + d
    segs = [(o_conv, 2 * d, None), (o_q, N_HEADS * hd, LOG2E / math.sqrt(hd))]
    for gi in range(g):
        segs += [(base + gi * hd, hd, None) for base in (o_ks, o_vs, o_kw, o_vw)]
    segs += [(o_gc, 2 * d, None), (o_kc, 2 * kvw, None), (o_gate, 3 * N_HEADS, None)]
    n_used = sum(n for _, n, _ in segs)
    return segs, (-n_used) % LANES


def _permute_in_proj(w_in):
    segs, n_pad = _in_proj_perm(w_in.shape[1])
    wt = jnp.swapaxes(w_in, 1, 2)
    parts = [(wt[:, a:a + n] if sc is None else wt[:, a:a + n] * sc).astype(BF16) for a, n, sc in segs]
    parts.append(jnp.zeros((wt.shape[0], n_pad, wt.shape[2]), BF16))
    return jnp.concatenate(parts, axis=1)


def kernel(x, mem, norm_mix_g, w_in, conv_dw_w, conv_dw_b, conv_ln_g, conv_ln_b, conv_pw_w, cmp_pos,
           cmp_w1, cmp_w2, w_out, norm_x_g, xq_w, xkv_w, xo_w, norm_ffn_g, ffn_in_w, ffn_out_w,
           rel_bias, final_norm_g):
    b, s, d = x.shape
    depth = w_in.shape[0]
    mlen = mem.shape[1]
    m = b * s
    tq = NSA_Q_TILE

    w_in_pt = _permute_in_proj(w_in)
    n_p = w_in_pt.shape[1]
    q_block = (2 * d) // GROUP_WIDTH
    kv_block = q_block + N_KV_GROUPS
    gc_block = (3 * d + N_KV_GROUPS * GROUP_WIDTH) // d
    ga_block = gc_block + 1
    cmp_block = (6 * d) // (2 * N_KV_GROUPS * HEAD_DIM)
    gl_block = (6 * d + 2 * N_KV_GROUPS * HEAD_DIM) // LANES

    ovt, expand, gexp = _static_tables(s, tq)
    biasn, biasc = _bias_tables(rel_bias, s, tq)

    row = lambda v: v.reshape(1, -1)
    pw_b, wm_b, wq_b, wkv_b, wo_b, wfi_b, wfo_b = (
        w.astype(BF16) for w in (conv_pw_w, w_out, xq_w, xkv_w, xo_w, ffn_in_w, ffn_out_w))

    xf = x.reshape(m, d)
    for l in range(depth):
        p = norm_matmul(xf, row(norm_mix_g[l]), w_in_pt, l, tm=ROW_TILE)
        p3 = p.reshape(b, s, n_p)
        conv = conv_module(p3, conv_dw_w[l], row(conv_dw_b[l]), row(conv_ln_g[l]), row(conv_ln_b[l]),
                           pw_b, l, t=ROW_TILE)
        pos4 = jnp.tile(cmp_pos[l], (1, 1, N_KV_GROUPS))
        kvc = compress(p3, cmp_block, pos4, cmp_w1[l].astype(BF16), cmp_w2[l].astype(BF16))
        nsa = nsa_attention(p3, kvc, gexp, biasn, biasc, ovt, expand, tq=tq,
                            q_block=q_block, kv_block=kv_block, gl_block=gl_block)
        kvx = matmul(mem.reshape(b * mlen, d), wkv_b, l, tm=mlen)
        xf = merge_xattn_block(p3, conv, nsa, wm_b, xf.reshape(b, s, d), row(norm_x_g[l]), wq_b,
                               kvx.reshape(b, mlen, 2 * d), wo_b, l, tm=MERGE_ROW_TILE, gc_block=gc_block,
                               ga_block=ga_block).reshape(m, d)
        xf = ffn_block(xf, row(norm_ffn_g[l]), wfi_b, wfo_b, row(final_norm_g), l,
                       tm=ROW_TILE, norm_out=(l == depth - 1))
    return xf.reshape(b, s, d)
```
